```python
import math
import jax, jax.numpy as jnp
from jax import lax
import numpy as np

D_MODEL = 1024
BATCH = 2
SEQ = 8192
DEPTH = 2
DEC_BATCH = 8
DEC_SEQ = 8192
PAST_LEN = 128

N_META = 16
BLOCK = 128
META_START = BLOCK - N_META
D_MIX = D_MODEL
A_HEADS = 4
A_QK_DIM = 64
A_V_DIM = 2 * A_QK_DIM
B_HEADS = 4
B_KV_HEADS = 2
B_HEAD_DIM = 64
WINDOW = 128
C_HEADS = 4
C_Q_RANK = 256
C_KV_RANK = 128
C_NOPE_DIM = 64
C_ROPE_DIM = 32
C_V_DIM = 64
ROPE_THETA = 10000.0
N_BUCKETS = 32
MAX_DISTANCE = 128
N_BIAS_HEADS = A_HEADS + B_HEADS
D_FF = 2816
EPS = 1e-6

IN_SIZES = (A_HEADS * 2 * A_QK_DIM, A_HEADS * 2 * A_QK_DIM, A_HEADS * A_V_DIM,
            B_HEADS * B_HEAD_DIM, B_KV_HEADS * B_HEAD_DIM, B_KV_HEADS * B_HEAD_DIM,
            C_Q_RANK, C_KV_RANK, C_ROPE_DIM)
D_IN = sum(IN_SIZES)
SPLIT_POINTS = tuple(sum(IN_SIZES[:i + 1]) for i in range(len(IN_SIZES) - 1))

kernel_name = "hymba_style_diff_window_mla_encoder"


def rmsnorm(x, g):
    xf = x.astype(jnp.float32)
    y = xf * lax.rsqrt(jnp.mean(xf * xf, axis=-1, keepdims=True) + EPS)
    return (y * g.astype(jnp.float32)).astype(x.dtype)


def swiglu(x, w_gu, w_down):
    g, u = jnp.split(x @ w_gu, 2, axis=-1)
    return (jax.nn.silu(g) * u) @ w_down


def t5_bucket(rel):
    half = N_BUCKETS // 2
    max_exact = half // 2
    ret = jnp.where(rel > 0, half, 0)
    n = jnp.abs(rel)
    nf = jnp.maximum(n, 1).astype(jnp.float32)
    large = max_exact + (jnp.log(nf / max_exact) / math.log(MAX_DISTANCE / max_exact)
                         * (half - max_exact)).astype(jnp.int32)
    large = jnp.minimum(large, half - 1)
    return ret + jnp.where(n < max_exact, n, large)


def rope_tables(pos):
    inv = ROPE_THETA ** (-jnp.arange(0, C_ROPE_DIM, 2, dtype=jnp.float32) / C_ROPE_DIM)
    ang = pos[:, None] * inv[None, :]
    ang = jnp.concatenate([ang, ang], axis=-1)
    return jnp.cos(ang), jnp.sin(ang)


def apply_rope(x, cos, sin):
    x1, x2 = jnp.split(x, 2, axis=-1)
    rot = jnp.concatenate([-x2, x1], axis=-1)
    return (x.astype(jnp.float32) * cos + rot.astype(jnp.float32) * sin).astype(x.dtype)


def diff_attention(q, k, v, lam, bias_tab, key_ok, slot):
    B, Lp = q.shape[0], q.shape[1]
    nb = Lp // BLOCK
    qb = jnp.moveaxis(q.reshape(B, nb, BLOCK, A_HEADS, 2, A_QK_DIM), 1, 0)
    scale = A_QK_DIM ** -0.5

    def one_block(args):
        qi, start = args
        s = jnp.einsum('bqhmd,bkhmd->bhmqk', qi, k).astype(jnp.float32) * scale
        qslot = start + jnp.arange(BLOCK)
        bias = bias_tab[t5_bucket(slot[None, :] - qslot[:, None])].astype(jnp.float32)
        s = s + jnp.transpose(bias, (2, 0, 1))[None, :, None]
        s = jnp.where(key_ok, s, -jnp.inf)
        p = jax.nn.softmax(s, axis=-1)
        w = p[:, :, 0] - lam * p[:, :, 1]
        return jnp.einsum('bhqk,bkhe->bqhe', w.astype(v.dtype), v)

    o = lax.map(one_block, (qb, jnp.arange(nb) * BLOCK))
    return jnp.moveaxis(o, 0, 1).reshape(B, Lp, A_HEADS, A_V_DIM)


def window_gqa(q, k, v, sinks, bias_tab):
    B, Lp = q.shape[0], q.shape[1]
    nb = Lp // BLOCK
    G = B_HEADS // B_KV_HEADS
    qb = q.reshape(B, nb, BLOCK, B_KV_HEADS, G, B_HEAD_DIM)

    def neighbours(t):
        tb = t.reshape(B, nb, BLOCK, B_KV_HEADS, B_HEAD_DIM)
        tp = jnp.pad(tb, ((0, 0), (1, 1), (0, 0), (0, 0), (0, 0)))
        return jnp.concatenate([tp[:, :-2], tp[:, 1:-1], tp[:, 2:]], axis=2)

    kn, vn = neighbours(k), neighbours(v)
    s = jnp.einsum('bnqhgd,bnkhd->bnhgqk', qb, kn).astype(jnp.float32) * B_HEAD_DIM ** -0.5
    rel = (jnp.arange(3 * BLOCK) - BLOCK)[None, :] - jnp.arange(BLOCK)[:, None]
    bias = bias_tab[t5_bucket(rel)].astype(jnp.float32).reshape(BLOCK, 3 * BLOCK, B_KV_HEADS, G)
    s = s + jnp.transpose(bias, (2, 3, 0, 1))[None, None]
    kslot = (jnp.arange(nb)[:, None] - 1) * BLOCK + jnp.arange(3 * BLOCK)[None, :]
    ok = (jnp.abs(rel) <= WINDOW)[None] & ((kslot >= META_START) & (kslot < Lp))[:, None, :]
    s = jnp.where(ok[None, :, None, None], s, -jnp.inf)
    sink = jnp.broadcast_to(sinks.astype(jnp.float32).reshape(B_KV_HEADS, G, 1, 1), s.shape[:-1] + (1,))
    p = jax.nn.softmax(jnp.concatenate([s, sink], axis=-1), axis=-1)[..., :-1]
    o = jnp.einsum('bnhgqk,bnkhd->bnqhgd', p.astype(v.dtype), vn)
    return o.reshape(B, Lp, B_HEADS * B_HEAD_DIM)


def mla(cq, ckv, kr, g_cq, g_ckv, w_uq, w_ukv, cos, sin, key_ok):
    B, Lp = cq.shape[0], cq.shape[1]
    nb = Lp // BLOCK
    q = (rmsnorm(cq, g_cq) @ w_uq).reshape(B, Lp, C_HEADS, C_NOPE_DIM + C_ROPE_DIM)
    q = jnp.concatenate([q[..., :C_NOPE_DIM],
                         apply_rope(q[..., C_NOPE_DIM:], cos[:, None], sin[:, None])], axis=-1)
    kv = (rmsnorm(ckv, g_ckv) @ w_ukv).reshape(B, Lp, C_HEADS, C_NOPE_DIM + C_V_DIM)
    k_nope, v = kv[..., :C_NOPE_DIM], kv[..., C_NOPE_DIM:]
    k_rope = apply_rope(kr, cos, sin)
    k = jnp.concatenate([k_nope, jnp.broadcast_to(k_rope[:, :, None], (B, Lp, C_HEADS, C_ROPE_DIM))], axis=-1)
    scale = (C_NOPE_DIM + C_ROPE_DIM) ** -0.5
    qb = jnp.moveaxis(q.reshape(B, nb, BLOCK, C_HEADS, C_NOPE_DIM + C_ROPE_DIM), 1, 0)

    def one_block(qi):
        s = jnp.einsum('bqhd,bkhd->bhqk', qi, k).astype(jnp.float32) * scale
        s = jnp.where(key_ok, s, -jnp.inf)
        p = jax.nn.softmax(s, axis=-1)
        return jnp.einsum('bhqk,bkhd->bqhd', p.astype(v.dtype), v)

    o = lax.map(one_block, qb)
    return jnp.moveaxis(o, 0, 1).reshape(B, Lp, C_HEADS * C_V_DIM)


def trunk(x, meta, rel_bias, g_ffn1, w_ffn1_gu, w_ffn1_down, g_mix, w_in, diff_lambda, g_subln,
          sinks, g_cq, g_ckv, w_uq, w_ukv, w_out, g_ffn2, w_ffn2_gu, w_ffn2_down, g_final):
    B, S, D = x.shape
    Lp = BLOCK + S
    lead = jnp.concatenate([jnp.zeros((META_START, D), x.dtype), meta.astype(x.dtype)], axis=0)
    h = jnp.concatenate([jnp.broadcast_to(lead[None], (B, BLOCK, D)), x], axis=1)
    slot = jnp.arange(Lp)
    key_ok = slot >= META_START
    cos, sin = rope_tables((slot - META_START).astype(jnp.float32))
    for l in range(DEPTH):
        h = h + 0.5 * swiglu(rmsnorm(h, g_ffn1[l]), w_ffn1_gu[l], w_ffn1_down[l])
        u = rmsnorm(h, g_mix[l]) @ w_in[l]
        aq, ak, av, bq, bk, bv, cq, ckv, kr = jnp.split(u, SPLIT_POINTS, axis=-1)
        lam_init = 0.8 - 0.6 * math.exp(-0.3 * l)
        dl = diff_lambda[l].astype(jnp.float32)
        lam = jnp.exp(jnp.sum(dl[0] * dl[1])) - jnp.exp(jnp.sum(dl[2] * dl[3])) + lam_init
        oa = diff_attention(aq.reshape(B, Lp, A_HEADS, 2, A_QK_DIM), ak.reshape(B, Lp, A_HEADS, 2, A_QK_DIM),
                            av.reshape(B, Lp, A_HEADS, A_V_DIM), lam, rel_bias[:, :A_HEADS], key_ok, slot)
        oa = (rmsnorm(oa, g_subln[l]) * (1.0 - lam_init)).reshape(B, Lp, A_HEADS * A_V_DIM)
        ob = window_gqa(bq.reshape(B, Lp, B_HEADS, B_HEAD_DIM), bk.reshape(B, Lp, B_KV_HEADS, B_HEAD_DIM),
                        bv.reshape(B, Lp, B_KV_HEADS, B_HEAD_DIM), sinks[l], rel_bias[:, A_HEADS:])
        oc = mla(cq, ckv, kr, g_cq[l], g_ckv[l], w_uq[l], w_ukv[l], cos, sin, key_ok)
        h = h + jnp.concatenate([oa, ob, oc], axis=-1) @ w_out[l]
        h = h + 0.5 * swiglu(rmsnorm(h, g_ffn2[l]), w_ffn2_gu[l], w_ffn2_down[l])
    return rmsnorm(h[:, BLOCK:], g_final)


def setup_inputs(seed: int = 0) -> dict:
    key = jax.random.key(seed)
    ks = jax.random.split(key, 24)

    def nrm(k, shape, scale):
        return jax.random.normal(k, shape, jnp.float32) * scale

    def gain(k, shape):
        return 1.0 + 0.05 * jax.random.normal(k, shape, jnp.float32)

    return {
        "x_prompt": nrm(ks[0], (BATCH, SEQ, D_MODEL), 1.0),
        "x_sample": nrm(ks[1], (DEC_BATCH, DEC_SEQ, D_MODEL), 1.0),
        "meta": nrm(ks[2], (N_META, D_MODEL), 1.0),
        "rel_bias": nrm(ks[3], (N_BUCKETS, N_BIAS_HEADS), 0.5),
        "g_ffn1": gain(ks[4], (DEPTH, D_MODEL)),
        "w_ffn1_gu": nrm(ks[5], (DEPTH, D_MODEL, 2 * D_FF), D_MODEL ** -0.5),
        "w_ffn1_down": nrm(ks[6], (DEPTH, D_FF, D_MODEL), D_FF ** -0.5),
        "g_mix": gain(ks[7], (DEPTH, D_MODEL)),
        "w_in": nrm(ks[8], (DEPTH, D_MODEL, D_IN), D_MODEL ** -0.5),
        "diff_lambda": nrm(ks[9], (DEPTH, 4, A_QK_DIM), 0.1),
        "g_subln": gain(ks[10], (DEPTH, A_V_DIM)),
        "sinks": nrm(ks[11], (DEPTH, B_HEADS), 0.5),
        "g_cq": gain(ks[12], (DEPTH, C_Q_RANK)),
        "g_ckv": gain(ks[13], (DEPTH, C_KV_RANK)),
        "w_uq": nrm(ks[14], (DEPTH, C_Q_RANK, C_HEADS * (C_NOPE_DIM + C_ROPE_DIM)), C_Q_RANK ** -0.5),
        "w_ukv": nrm(ks[15], (DEPTH, C_KV_RANK, C_HEADS * (C_NOPE_DIM + C_V_DIM)), C_KV_RANK ** -0.5),
        "w_out": nrm(ks[16], (DEPTH, D_MIX, D_MODEL), D_MIX ** -0.5),
        "g_ffn2": gain(ks[17], (DEPTH, D_MODEL)),
        "w_ffn2_gu": nrm(ks[18], (DEPTH, D_MODEL, 2 * D_FF), D_MODEL ** -0.5),
        "w_ffn2_down": nrm(ks[19], (DEPTH, D_FF, D_MODEL), D_FF ** -0.5),
        "g_final": gain(ks[20], (D_MODEL,)),
    }


def reference(x_prompt, x_sample, meta, rel_bias, g_ffn1, w_ffn1_gu, w_ffn1_down, g_mix, w_in,
              diff_lambda, g_subln, sinks, g_cq, g_ckv, w_uq, w_ukv, w_out, g_ffn2, w_ffn2_gu,
              w_ffn2_down, g_final):
    y_prompt = trunk(x_prompt, meta, rel_bias, g_ffn1, w_ffn1_gu, w_ffn1_down, g_mix, w_in, diff_lambda,
                     g_subln, sinks, g_cq, g_ckv, w_uq, w_ukv, w_out, g_ffn2, w_ffn2_gu, w_ffn2_down, g_final)
    y_sample = trunk(x_sample, meta, rel_bias, g_ffn1, w_ffn1_gu, w_ffn1_down, g_mix, w_in, diff_lambda,
                     g_subln, sinks, g_cq, g_ckv, w_uq, w_ukv, w_out, g_ffn2, w_ffn2_gu, w_ffn2_down, g_final)
    return (y_prompt, y_sample)
```

```python
import functools
import math

import jax
import jax.numpy as jnp
from jax import lax
from jax.experimental import pallas as pl
from jax.experimental.pallas import tpu as pltpu

D_MODEL = 1024
DEPTH = 2
N_META = 16
BLOCK = 128
META_START = BLOCK - N_META
A_HEADS = 4
A_QK_DIM = 64
A_V_DIM = 2 * A_QK_DIM
B_HEADS = 4
B_KV_HEADS = 2
B_HEAD_DIM = 64
WINDOW = 128
C_HEADS = 4
C_Q_RANK = 256
C_KV_RANK = 128
C_NOPE_DIM = 64
C_ROPE_DIM = 32
C_V_DIM = 64
ROPE_THETA = 10000.0
N_BUCKETS = 32
MAX_DISTANCE = 128
D_FF = 2816
EPS = 1e-6

LANES = 128
TILE = 5 * BLOCK
FF_CHUNK = D_FF // 2
VMEM_LIMIT = 56 * 1024 * 1024

F32 = jnp.float32
BF16 = jnp.bfloat16
NEG_INF = float("-inf")

_W_AQ, _W_AK, _W_AV, _W_BQ = 0, 512, 1024, 1536
_W_BK, _W_BV, _W_BVS = 2048, 2176, 2304
_W_CQ, _W_CKV, _W_KR, _W_KRR = 2432, 2688, 2816, 2944
_W_TOTAL = 3072


def _params(n_axes):
    return pltpu.CompilerParams(dimension_semantics=("arbitrary",) * n_axes, vmem_limit_bytes=VMEM_LIMIT)


def _resident(shape):
    return pl.BlockSpec(shape, lambda *_: (0,) * len(shape), pipeline_mode=pl.Buffered(1))


def _rms(x, g):
    return x * lax.rsqrt(jnp.mean(x * x, axis=-1, keepdims=True) + EPS) * g


def _dot(a, b):
    return jnp.dot(a, b, preferred_element_type=F32)


def _dot_nt(a, b):
    return lax.dot_general(a, b, (((1,), (1,)), ((), ())), preferred_element_type=F32)


def _ffn_kernel(x_ref, g_ref, wgu_ref, wd_ref, *rest, final_norm):
    if final_norm:
        gf_ref, o_ref = rest
    else:
        (o_ref,) = rest
    x = x_ref[...]
    xn = _rms(x, g_ref[...]).astype(BF16)
    acc = None
    for lo in range(0, D_FF, FF_CHUNK):
        g = _dot(xn, wgu_ref[:, lo:lo + FF_CHUNK])
        u = _dot(xn, wgu_ref[:, D_FF + lo:D_FF + lo + FF_CHUNK])
        a = (g * jax.nn.sigmoid(g) * u).astype(BF16)
        d = _dot(a, wd_ref[lo:lo + FF_CHUNK, :])
        acc = d if acc is None else acc + d
    y = x + 0.5 * acc
    if final_norm:
        y = _rms(y, gf_ref[...])
    o_ref[...] = y


def _ffn(h, g, wgu, wd, g_final=None):
    nb, lp, d = h.shape
    tile = pl.BlockSpec((None, TILE, d), lambda b, t: (b, t, 0))
    in_specs = [tile, _resident((1, d)), _resident(wgu.shape), _resident(wd.shape)]
    args = [h, g.reshape(1, d), wgu, wd]
    if g_final is not None:
        in_specs.append(_resident((1, d)))
        args.append(g_final.reshape(1, d))
    return pl.pallas_call(
        functools.partial(_ffn_kernel, final_norm=g_final is not None),
        grid=(nb, lp // TILE),
        in_specs=in_specs,
        out_specs=tile,
        out_shape=jax.ShapeDtypeStruct(h.shape, F32),
        compiler_params=_params(2),
        name="ffn",
    )(*args)


def _inproj_kernel(h_ref, g_ref, w_ref, gcq_ref, gckv_ref, wuqm_ref, wuqr_ref, wukvk_ref, wukvv_ref,
                   cos_ref, sin_ref,
                   aq_ref, ak_ref, av_ref, bq_ref, bk_ref, bv_ref, bvs_ref, qc_ref, kc_ref, vc_ref, *, c_scale):
    xn = _rms(h_ref[...], g_ref[...]).astype(BF16)

    def proj(lo, n):
        return _dot(xn, w_ref[:, lo:lo + n])

    aq_ref[...] = proj(_W_AQ, 512).astype(BF16)
    ak_ref[...] = proj(_W_AK, 512).astype(BF16)
    av_ref[...] = proj(_W_AV, 512).astype(BF16)
    bq_ref[...] = proj(_W_BQ, 512).astype(BF16)
    bk_ref[...] = proj(_W_BK, 128).astype(BF16)
    bv_ref[...] = proj(_W_BV, 128).astype(BF16)
    bvs_ref[...] = proj(_W_BVS, 128).astype(BF16)

    cos = cos_ref[...]
    sin = sin_ref[...]
    cos4 = jnp.concatenate([cos] * C_HEADS, axis=1)
    sin4 = jnp.concatenate([sin] * C_HEADS, axis=1)
    cqn = _rms(proj(_W_CQ, C_Q_RANK), gcq_ref[...]).astype(BF16)
    q = _dot(cqn, wuqm_ref[...]) * cos4 + _dot(cqn, wuqr_ref[...]) * sin4
    qc_ref[...] = (q * c_scale).astype(BF16)

    ckvn = _rms(proj(_W_CKV, C_KV_RANK), gckv_ref[...]).astype(BF16)
    k_rope = proj(_W_KR, LANES) * cos + proj(_W_KRR, LANES) * sin
    kc_ref[...] = (_dot(ckvn, wukvk_ref[...]) + jnp.concatenate([k_rope] * C_HEADS, axis=1)).astype(BF16)
    vc_ref[...] = _dot(ckvn, wukvv_ref[...]).astype(BF16)


def _inproj(h, g, w_big, gcq, gckv, wuqm, wuqr, wukvk, wukvv, cos_t, sin_t):
    nb, lp, d = h.shape

    def tile(n):
        return pl.BlockSpec((None, TILE, n), lambda b, t: (b, t, 0))

    pos = pl.BlockSpec((TILE, LANES), lambda b, t: (t, 0))
    widths = (512, 512, 512, 512, 128, 128, 128, 512, 512, 256)
    return pl.pallas_call(
        functools.partial(_inproj_kernel, c_scale=(C_NOPE_DIM + C_ROPE_DIM) ** -0.5),
        grid=(nb, lp // TILE),
        in_specs=[tile(d), _resident((1, d)), _resident(w_big.shape), _resident((1, C_Q_RANK)),
                  _resident((1, C_KV_RANK)), _resident(wuqm.shape), _resident(wuqr.shape),
                  _resident(wukvk.shape), _resident(wukvv.shape), pos, pos],
        out_specs=[tile(n) for n in widths],
        out_shape=[jax.ShapeDtypeStruct((nb, lp, n), BF16) for n in widths],
        compiler_params=_params(2),
        name="inproj",
    )(h, g.reshape(1, d), w_big, gcq.reshape(1, -1), gckv.reshape(1, -1), wuqm, wuqr, wukvk, wukvv, cos_t, sin_t)


def _flash_chunk(qq, kc, vc, s_ref, m_ref, l_ref, acc_ref, *, first, bias_fn=None, shift=None):
    s_ref[...] = _dot_nt(qq, kc)
    if bias_fn is not None:
        bias_fn()
    s = s_ref[...]
    row_max = jnp.max(s, axis=-1, keepdims=True)
    if shift is not None:
        row_max = row_max + shift
    if first:
        m_new = row_max
    else:
        m_old = m_ref[...]
        m_new = jnp.maximum(m_old, row_max)
    sub = m_new if shift is None else m_new - shift
    p = jnp.exp(s - sub)
    row_sum = jnp.sum(p, axis=-1, keepdims=True)
    pv = _dot(p.astype(BF16), vc)
    if first:
        l_ref[...] = row_sum
        acc_ref[...] = pv
    else:
        alpha = jnp.exp(m_old - m_new)
        l_ref[...] = alpha * l_ref[...] + row_sum
        acc_ref[...] = alpha * acc_ref[...] + pv
    m_ref[...] = m_new


def _attn_a_kernel(far_ref, q_ref, k_ref, v_ref, tab_ref, kmask_ref, dl_ref, gs_ref, o_ref,
                   s_ref, m_ref, l_ref, acc_ref, *, lam_init, n_chunks):
    head = pl.program_id(1)
    i = pl.program_id(2)
    q = q_ref[...]
    lane = lax.broadcasted_iota(jnp.int32, q.shape, 1)
    zero = jnp.zeros_like(q)
    qq = jnp.concatenate([jnp.where(lane < A_QK_DIM, q, zero), jnp.where(lane >= A_QK_DIM, q, zero)], axis=0)
    far_left = far_ref[head, 0]
    far_right = far_ref[head, 1]

    def chunk(j, first):
        row0 = j * TILE if isinstance(j, int) else pl.multiple_of(j * TILE, TILE)
        kc = k_ref[pl.ds(row0, TILE), :]
        vc = v_ref[pl.ds(row0, TILE), :]
        near = jnp.abs(j - i) <= 1

        def bias_fn():
            @pl.when(near)
            def _():
                tb = tab_ref[jnp.clip(j - i + 1, 0, 2)]
                s_ref[0:TILE, :] += tb
                s_ref[TILE:2 * TILE, :] += tb
            if first:
                s_ref[...] += kmask_ref[...]

        shift = jnp.where(near, 0.0, jnp.where(j < i, far_left, far_right))
        _flash_chunk(qq, kc, vc, s_ref, m_ref, l_ref, acc_ref, first=first, bias_fn=bias_fn, shift=shift)

    chunk(0, True)

    def body(j, carry):
        chunk(j, False)
        return carry

    lax.fori_loop(1, n_chunks, body, 0)

    o = acc_ref[...] / l_ref[...]
    dl = dl_ref[...]
    lam = (jnp.exp(jnp.sum(dl[0:1] * dl[1:2], axis=-1, keepdims=True))
           - jnp.exp(jnp.sum(dl[2:3] * dl[3:4], axis=-1, keepdims=True)) + lam_init)
    w = o[0:TILE] - lam * o[TILE:2 * TILE]
    o_ref[...] = (_rms(w, gs_ref[...]) * (1.0 - lam_init)).astype(BF16)


def _attn_a(aq, ak, av, tab, far, kmask, dl, gs, lam_init):
    nb, lp, _ = aq.shape
    n_chunks = lp // TILE
    qspec = pl.BlockSpec((None, TILE, LANES), lambda b, h, t: (b, t, h))
    kvspec = pl.BlockSpec((None, lp, LANES), lambda b, h, t: (b, 0, h))
    return pl.pallas_call(
        functools.partial(_attn_a_kernel, lam_init=lam_init, n_chunks=n_chunks),
        grid=(nb, A_HEADS, n_chunks),
        in_specs=[pl.BlockSpec(memory_space=pltpu.SMEM), qspec, kvspec, kvspec,
                  pl.BlockSpec((None, 3, TILE, TILE), lambda b, h, t: (h, 0, 0, 0)),
                  _resident((1, TILE)), _resident((4, A_QK_DIM)), _resident((1, A_V_DIM))],
        out_specs=qspec,
        out_shape=jax.ShapeDtypeStruct((nb, lp, A_HEADS * A_V_DIM), BF16),
        scratch_shapes=[pltpu.VMEM((2 * TILE, TILE), F32), pltpu.VMEM((2 * TILE, 1), F32),
                        pltpu.VMEM((2 * TILE, 1), F32), pltpu.VMEM((2 * TILE, A_V_DIM), F32)],
        compiler_params=_params(3),
        name="attn_a",
    )(far, aq, ak, av, tab, kmask, dl, gs.reshape(1, A_V_DIM))


def _attn_c_kernel(q_ref, k_ref, v_ref, kmask_ref, o_ref, s_ref, m_ref, l_ref, acc_ref, *, n_chunks):
    outs = []
    for hh in range(2):
        cols = slice(hh * LANES, (hh + 1) * LANES)
        qq = q_ref[:, cols]

        def chunk(j, first):
            row0 = j * TILE if isinstance(j, int) else pl.multiple_of(j * TILE, TILE)
            kc = k_ref[pl.ds(row0, TILE), cols]
            vc = v_ref[pl.ds(row0, TILE), :]

            def bias_fn():
                s_ref[...] += kmask_ref[...]

            _flash_chunk(qq, kc, vc, s_ref, m_ref, l_ref, acc_ref, first=first, bias_fn=bias_fn if first else None)

        chunk(0, True)

        def body(j, carry):
            chunk(j, False)
            return carry

        lax.fori_loop(1, n_chunks, body, 0)
        outs.append(acc_ref[...] / l_ref[...])
    lane = lax.broadcasted_iota(jnp.int32, outs[0].shape, 1)
    o_ref[...] = jnp.where(lane < C_V_DIM, outs[0], outs[1]).astype(BF16)


def _attn_c(qc, kc, vc, kmask):
    nb, lp, _ = qc.shape
    n_chunks = lp // TILE
    return pl.pallas_call(
        functools.partial(_attn_c_kernel, n_chunks=n_chunks),
        grid=(nb, C_HEADS // 2, n_chunks),
        in_specs=[pl.BlockSpec((None, TILE, 2 * LANES), lambda b, p, t: (b, t, p)),
                  pl.BlockSpec((None, lp, 2 * LANES), lambda b, p, t: (b, 0, p)),
                  pl.BlockSpec((None, lp, LANES), lambda b, p, t: (b, 0, p)),
                  _resident((1, TILE))],
        out_specs=pl.BlockSpec((None, TILE, LANES), lambda b, p, t: (b, t, p)),
        out_shape=jax.ShapeDtypeStruct((nb, lp, C_HEADS * C_V_DIM), BF16),
        scratch_shapes=[pltpu.VMEM((TILE, TILE), F32), pltpu.VMEM((TILE, 1), F32),
                        pltpu.VMEM((TILE, 1), F32), pltpu.VMEM((TILE, LANES), F32)],
        compiler_params=_params(3),
        name="attn_c",
    )(qc, kc, vc, kmask)


def _attn_b_kernel(sink_ref, q_ref, k_ref, v_ref, vs_ref, tab_ref, o_ref, *, lp):
    t = pl.program_id(1)
    lane = lax.broadcasted_iota(jnp.int32, (BLOCK, LANES), 1)
    kk = lax.broadcasted_iota(jnp.int32, (1, 3 * BLOCK), 1)
    for blk in range(TILE // BLOCK):
        gblk = t * (TILE // BLOCK) + blk
        row0 = pl.multiple_of(gblk * BLOCK, BLOCK)
        kw = k_ref[pl.ds(row0, 3 * BLOCK), :]
        vw = v_ref[pl.ds(row0, 3 * BLOCK), :]
        vsw = vs_ref[pl.ds(row0, 3 * BLOCK), :]
        kslot = (gblk - 1) * BLOCK + kk
        row_mask = jnp.where((kslot >= META_START) & (kslot < lp), 0.0, NEG_INF)
        rows = slice(blk * BLOCK, (blk + 1) * BLOCK)
        outs = []
        for h in range(B_HEADS):
            qh = q_ref[rows, h * LANES:(h + 1) * LANES]
            s = _dot_nt(qh, kw) + tab_ref[h] + row_mask
            sink = sink_ref[h]
            m = jnp.maximum(jnp.max(s, axis=-1, keepdims=True), sink)
            p = jnp.exp(s - m)
            denom = jnp.sum(p, axis=-1, keepdims=True) + jnp.exp(sink - m)
            outs.append(_dot(p.astype(BF16), vw if h in (0, 3) else vsw) / denom)
        o_ref[rows, 0:LANES] = jnp.where(lane < B_HEAD_DIM, outs[0], outs[1]).astype(BF16)
        o_ref[rows, LANES:2 * LANES] = jnp.where(lane < B_HEAD_DIM, outs[2], outs[3]).astype(BF16)


def _attn_b(bq, bk_pad, bv_pad, bvs_pad, tab, sinks):
    nb, lp, _ = bq.shape
    kvspec = pl.BlockSpec((None, lp + 2 * BLOCK, LANES), lambda b, t: (b, 0, 0))
    return pl.pallas_call(
        functools.partial(_attn_b_kernel, lp=lp),
        grid=(nb, lp // TILE),
        in_specs=[pl.BlockSpec(memory_space=pltpu.SMEM),
                  pl.BlockSpec((None, TILE, B_HEADS * LANES), lambda b, t: (b, t, 0)),
                  kvspec, kvspec, kvspec, _resident(tab.shape)],
        out_specs=pl.BlockSpec((None, TILE, B_HEADS * B_HEAD_DIM), lambda b, t: (b, t, 0)),
        out_shape=jax.ShapeDtypeStruct((nb, lp, B_HEADS * B_HEAD_DIM), BF16),
        compiler_params=_params(2),
        name="attn_b",
    )(sinks, bq, bk_pad, bv_pad, bvs_pad, tab)


def _outproj_kernel(h_ref, oa_ref, ob_ref, oc_ref, wa_ref, wb_ref, wc_ref, o_ref):
    o_ref[...] = (h_ref[...] + _dot(oa_ref[...], wa_ref[...]) + _dot(ob_ref[...], wb_ref[...])
                  + _dot(oc_ref[...], wc_ref[...]))


def _outproj(h, oa, ob, oc, wa, wb, wc):
    nb, lp, d = h.shape

    def tile(n):
        return pl.BlockSpec((None, TILE, n), lambda b, t: (b, t, 0))

    return pl.pallas_call(
        _outproj_kernel,
        grid=(nb, lp // TILE),
        in_specs=[tile(d), tile(oa.shape[-1]), tile(ob.shape[-1]), tile(oc.shape[-1]),
                  _resident(wa.shape), _resident(wb.shape), _resident(wc.shape)],
        out_specs=tile(d),
        out_shape=jax.ShapeDtypeStruct(h.shape, F32),
        compiler_params=_params(2),
        name="outproj",
    )(h, oa, ob, oc, wa, wb, wc)


def _t5_bucket(rel):
    half = N_BUCKETS // 2
    max_exact = half // 2
    ret = jnp.where(rel > 0, half, 0)
    n = jnp.abs(rel)
    nf = jnp.maximum(n, 1).astype(F32)
    large = max_exact + (jnp.log(nf / max_exact) / math.log(MAX_DISTANCE / max_exact)
                         * (half - max_exact)).astype(jnp.int32)
    large = jnp.minimum(large, half - 1)
    return ret + jnp.where(n < max_exact, n, large)


def _rot_cols(w):
    half = w.shape[-1] // 2
    return jnp.concatenate([-w[..., half:], w[..., :half]], axis=-1)


def _bias_tables(rel_bias, lp):
    rb = rel_bias.astype(F32)
    qq = jnp.arange(TILE)[:, None]
    kk = jnp.arange(TILE)[None, :]
    rel = jnp.stack([kk - qq + d * TILE for d in (-1, 0, 1)])
    tab_a = jnp.transpose(rb[:, :A_HEADS][_t5_bucket(rel)], (3, 0, 1, 2))
    far_a = rb[:, :A_HEADS][_t5_bucket(jnp.array([-2 * TILE, 2 * TILE]))].T
    relb = (jnp.arange(3 * BLOCK) - BLOCK)[None, :] - jnp.arange(BLOCK)[:, None]
    tab_b = jnp.transpose(rb[:, A_HEADS:][_t5_bucket(relb)], (2, 0, 1))
    tab_b = jnp.where((jnp.abs(relb) <= WINDOW)[None], tab_b, NEG_INF)
    kmask = jnp.where(jnp.arange(TILE) >= META_START, 0.0, NEG_INF).astype(F32).reshape(1, TILE)
    slot = jnp.arange(lp)
    inv = ROPE_THETA ** (-jnp.arange(0, C_ROPE_DIM, 2, dtype=F32) / C_ROPE_DIM)
    ang = (slot - META_START).astype(F32)[:, None] * inv[None, :]
    ang = jnp.concatenate([ang, ang], axis=-1)
    ones = jnp.ones((lp, C_NOPE_DIM), F32)
    zeros_n = jnp.zeros((lp, C_NOPE_DIM), F32)
    zeros_p = jnp.zeros((lp, LANES - C_NOPE_DIM - C_ROPE_DIM), F32)
    cos_t = jnp.concatenate([ones, jnp.cos(ang), zeros_p], axis=-1)
    sin_t = jnp.concatenate([zeros_n, jnp.sin(ang), zeros_p], axis=-1)
    return tab_a, far_a, tab_b, kmask, cos_t, sin_t


def _layer_weights(w_in, w_uq, w_ukv, w_out):
    d = w_in.shape[0]
    sizes = (512, 512, 512, 256, 128, 128, 256, 128, 32)
    offs = [0]
    for s in sizes:
        offs.append(offs[-1] + s)
    aq, ak, av, bq, bk, bv, cq, ckv, kr = (w_in[:, offs[i]:offs[i + 1]] for i in range(9))
    zeros64 = jnp.zeros((d, B_HEAD_DIM), F32)
    bq_pad = []
    for h in range(B_HEADS):
        qh = bq[:, h * B_HEAD_DIM:(h + 1) * B_HEAD_DIM] * (B_HEAD_DIM ** -0.5)
        bq_pad += [qh, zeros64] if h // (B_HEADS // B_KV_HEADS) == 0 else [zeros64, qh]
    bvs = jnp.concatenate([bv[:, B_HEAD_DIM:], bv[:, :B_HEAD_DIM]], axis=1)
    pad_lo = jnp.zeros((d, C_NOPE_DIM), F32)
    pad_hi = jnp.zeros((d, LANES - C_NOPE_DIM - C_ROPE_DIM), F32)
    w_big = jnp.concatenate(
        [aq * (A_QK_DIM ** -0.5), ak, av] + bq_pad + [bk, bv, bvs, cq, ckv,
                                                      pad_lo, kr, pad_hi, pad_lo, _rot_cols(kr), pad_hi],
        axis=1).astype(BF16)
    uq = w_uq.reshape(C_Q_RANK, C_HEADS, C_NOPE_DIM + C_ROPE_DIM)
    uq_nope, uq_rope = uq[..., :C_NOPE_DIM], uq[..., C_NOPE_DIM:]
    zq = jnp.zeros((C_Q_RANK, C_HEADS, LANES - C_NOPE_DIM - C_ROPE_DIM), F32)
    wuqm = jnp.concatenate([uq_nope, uq_rope, zq], axis=-1).reshape(C_Q_RANK, C_HEADS * LANES).astype(BF16)
    wuqr = jnp.concatenate([jnp.zeros_like(uq_nope), _rot_cols(uq_rope), zq], axis=-1
                           ).reshape(C_Q_RANK, C_HEADS * LANES).astype(BF16)
    ukv = w_ukv.reshape(C_KV_RANK, C_HEADS, C_NOPE_DIM + C_V_DIM)
    wukvk = jnp.concatenate([ukv[..., :C_NOPE_DIM], jnp.zeros((C_KV_RANK, C_HEADS, LANES - C_NOPE_DIM), F32)],
                            axis=-1).reshape(C_KV_RANK, C_HEADS * LANES).astype(BF16)
    wukvv = ukv[..., C_NOPE_DIM:].reshape(C_KV_RANK, C_HEADS * C_V_DIM).astype(BF16)
    n_a = A_HEADS * A_V_DIM
    n_b = B_HEADS * B_HEAD_DIM
    wo = w_out.astype(BF16)
    return w_big, wuqm, wuqr, wukvk, wukvv, wo[:n_a], wo[n_a:n_a + n_b], wo[n_a + n_b:]


def _trunk(x, meta, rel_bias, g_ffn1, w_ffn1_gu, w_ffn1_down, g_mix, w_in, diff_lambda, g_subln, sinks, g_cq,
           g_ckv, w_uq, w_ukv, w_out, g_ffn2, w_ffn2_gu, w_ffn2_down, g_final):
    nb, seq, d = x.shape
    lp = BLOCK + seq
    assert lp % TILE == 0 and d == D_MODEL
    lead = jnp.concatenate([jnp.zeros((META_START, d), x.dtype), meta.astype(x.dtype)], axis=0)
    h = jnp.concatenate([jnp.broadcast_to(lead[None], (nb, BLOCK, d)), x], axis=1)
    tab_a, far_a, tab_b, kmask, cos_t, sin_t = _bias_tables(rel_bias, lp)
    pad = ((0, 0), (BLOCK, BLOCK), (0, 0))
    for l in range(DEPTH):
        w_big, wuqm, wuqr, wukvk, wukvv, wo_a, wo_b, wo_c = _layer_weights(w_in[l], w_uq[l], w_ukv[l], w_out[l])
        h = _ffn(h, g_ffn1[l], w_ffn1_gu[l].astype(BF16), w_ffn1_down[l].astype(BF16))
        aq, ak, av, bq, bk, bv, bvs, qc, kc, vc = _inproj(h, g_mix[l], w_big, g_cq[l], g_ckv[l], wuqm, wuqr,
                                                           wukvk, wukvv, cos_t, sin_t)
        lam_init = 0.8 - 0.6 * math.exp(-0.3 * l)
        oa = _attn_a(aq, ak, av, tab_a, far_a, kmask, diff_lambda[l].astype(F32), g_subln[l], lam_init)
        ob = _attn_b(bq, jnp.pad(bk, pad), jnp.pad(bv, pad), jnp.pad(bvs, pad), tab_b, sinks[l].astype(F32))
        oc = _attn_c(qc, kc, vc, kmask)
        h = _outproj(h, oa, ob, oc, wo_a, wo_b, wo_c)
        h = _ffn(h, g_ffn2[l], w_ffn2_gu[l].astype(BF16), w_ffn2_down[l].astype(BF16),
                 g_final=g_final if l == DEPTH - 1 else None)
    return h[:, BLOCK:]


def kernel(x_prompt, x_sample, meta, rel_bias, g_ffn1, w_ffn1_gu, w_ffn1_down, g_mix, w_in, diff_lambda, g_subln,
           sinks, g_cq, g_ckv, w_uq, w_ukv, w_out, g_ffn2, w_ffn2_gu, w_ffn2_down, g_final):
    n_prompt = x_prompt.shape[0]
    x = jnp.concatenate([x_prompt, x_sample], axis=0)
    y = _trunk(x, meta, rel_bias, g_ffn1, w_ffn1_gu, w_ffn1_down, g_mix, w_in, diff_lambda, g_subln, sinks, g_cq,
               g_ckv, w_uq, w_ukv, w_out, g_ffn2, w_ffn2_gu, w_ffn2_down, g_final)
    return (y[:n_prompt], y[n_prompt:])
```

```python
import functools
import math

import jax
import jax.numpy as jnp
from jax import lax
from jax.experimental import pallas as pl
from jax.experimental.pallas import tpu as pltpu

D_MODEL = 1024
DEPTH = 2
N_META = 16
BLOCK = 128
META_START = BLOCK - N_META
A_HEADS = 4
A_QK_DIM = 64
A_V_DIM = 2 * A_QK_DIM
B_HEADS = 4
B_KV_HEADS = 2
B_HEAD_DIM = 64
WINDOW = 128
C_HEADS = 4
C_Q_RANK = 256
C_KV_RANK = 128
C_NOPE_DIM = 64
C_ROPE_DIM = 32
C_V_DIM = 64
ROPE_THETA = 10000.0
N_BUCKETS = 32
MAX_DISTANCE = 128
D_FF = 2816
EPS = 1e-6

LANES = 128
TILE = 5 * BLOCK
FF_CHUNK = D_FF // 2
ROW_BLOCK = TILE // 2
PIPE_UNROLL = 4
VMEM_LIMIT = 56 * 1024 * 1024

F32 = jnp.float32
BF16 = jnp.bfloat16
NEG_INF = float("-inf")

_W_AQ, _W_AK, _W_AV, _W_BQ = 0, 512, 1024, 1536
_W_BK, _W_BV, _W_BVS = 2048, 2176, 2304
_W_CQ, _W_CKV, _W_KR, _W_KRR = 2432, 2688, 2816, 2944
_W_TOTAL = 3072


def _params(n_axes):
    return pltpu.CompilerParams(dimension_semantics=("arbitrary",) * n_axes, vmem_limit_bytes=VMEM_LIMIT)


def _resident(shape):
    return pl.BlockSpec(shape, lambda *_: (0,) * len(shape), pipeline_mode=pl.Buffered(1))


def _rms(x, g):
    return x * lax.rsqrt(jnp.mean(x * x, axis=-1, keepdims=True) + EPS) * g


def _dot(a, b):
    return jnp.dot(a, b, preferred_element_type=F32)


def _dot_nt(a, b):
    return lax.dot_general(a, b, (((1,), (1,)), ((), ())), preferred_element_type=F32)


def _ffn_kernel(x_ref, g_ref, wgu_ref, wd_ref, *rest, final_norm):
    if final_norm:
        gf_ref, o_ref = rest
    else:
        (o_ref,) = rest
    x = x_ref[...]
    xn = _rms(x, g_ref[...]).astype(BF16)
    acc = None
    for lo in range(0, D_FF, FF_CHUNK):
        g = _dot(xn, wgu_ref[:, lo:lo + FF_CHUNK])
        u = _dot(xn, wgu_ref[:, D_FF + lo:D_FF + lo + FF_CHUNK])
        a = (g * jax.nn.sigmoid(g) * u).astype(BF16)
        d = _dot(a, wd_ref[lo:lo + FF_CHUNK, :])
        acc = d if acc is None else acc + d
    y = x + 0.5 * acc
    if final_norm:
        y = _rms(y, gf_ref[...])
    o_ref[...] = y


def _ffn(h, g, wgu, wd, g_final=None):
    nb, lp, d = h.shape
    tile = pl.BlockSpec((None, TILE, d), lambda b, t: (b, t, 0))
    in_specs = [tile, _resident((1, d)), _resident(wgu.shape), _resident(wd.shape)]
    args = [h, g.reshape(1, d), wgu, wd]
    if g_final is not None:
        in_specs.append(_resident((1, d)))
        args.append(g_final.reshape(1, d))
    return pl.pallas_call(
        functools.partial(_ffn_kernel, final_norm=g_final is not None),
        grid=(nb, lp // TILE),
        in_specs=in_specs,
        out_specs=tile,
        out_shape=jax.ShapeDtypeStruct(h.shape, F32),
        compiler_params=_params(2),
        name="ffn",
    )(*args)


def _inproj_kernel(h_ref, g_ref, w_ref, gcq_ref, gckv_ref, wuqm_ref, wuqr_ref, wukvk_ref, wukvv_ref,
                   cos_ref, sin_ref,
                   aq_ref, ak_ref, av_ref, bq_ref, bk_ref, bv_ref, bvs_ref, qc_ref, kc_ref, vc_ref, *, c_scale):
    xn = _rms(h_ref[...], g_ref[...]).astype(BF16)

    def proj(lo, n):
        return _dot(xn, w_ref[:, lo:lo + n])

    aq_ref[...] = proj(_W_AQ, 512).astype(BF16)
    ak_ref[...] = proj(_W_AK, 512).astype(BF16)
    av_ref[...] = proj(_W_AV, 512).astype(BF16)
    bq_ref[...] = proj(_W_BQ, 512).astype(BF16)
    bk_ref[...] = proj(_W_BK, 128).astype(BF16)
    bv_ref[...] = proj(_W_BV, 128).astype(BF16)
    bvs_ref[...] = proj(_W_BVS, 128).astype(BF16)

    cos = cos_ref[...]
    sin = sin_ref[...]
    cos4 = jnp.concatenate([cos] * C_HEADS, axis=1)
    sin4 = jnp.concatenate([sin] * C_HEADS, axis=1)
    cqn = _rms(proj(_W_CQ, C_Q_RANK), gcq_ref[...]).astype(BF16)
    q = _dot(cqn, wuqm_ref[...]) * cos4 + _dot(cqn, wuqr_ref[...]) * sin4
    qc_ref[...] = (q * c_scale).astype(BF16)

    ckvn = _rms(proj(_W_CKV, C_KV_RANK), gckv_ref[...]).astype(BF16)
    k_rope = proj(_W_KR, LANES) * cos + proj(_W_KRR, LANES) * sin
    kc_ref[...] = (_dot(ckvn, wukvk_ref[...]) + jnp.concatenate([k_rope] * C_HEADS, axis=1)).astype(BF16)
    vc_ref[...] = _dot(ckvn, wukvv_ref[...]).astype(BF16)


def _inproj(h, g, w_big, gcq, gckv, wuqm, wuqr, wukvk, wukvv, cos_t, sin_t):
    nb, lp, d = h.shape

    def tile(n):
        return pl.BlockSpec((None, TILE, n), lambda b, t: (b, t, 0))

    pos = pl.BlockSpec((TILE, LANES), lambda b, t: (t, 0))
    widths = (512, 512, 512, 512, 128, 128, 128, 512, 512, 256)
    return pl.pallas_call(
        functools.partial(_inproj_kernel, c_scale=(C_NOPE_DIM + C_ROPE_DIM) ** -0.5),
        grid=(nb, lp // TILE),
        in_specs=[tile(d), _resident((1, d)), _resident(w_big.shape), _resident((1, C_Q_RANK)),
                  _resident((1, C_KV_RANK)), _resident(wuqm.shape), _resident(wuqr.shape),
                  _resident(wukvk.shape), _resident(wukvv.shape), pos, pos],
        out_specs=[tile(n) for n in widths],
        out_shape=[jax.ShapeDtypeStruct((nb, lp, n), BF16) for n in widths],
        compiler_params=_params(2),
        name="inproj",
    )(h, g.reshape(1, d), w_big, gcq.reshape(1, -1), gckv.reshape(1, -1), wuqm, wuqr, wukvk, wukvv, cos_t, sin_t)


def _flash_scratch():
    return ([pltpu.VMEM((ROW_BLOCK, TILE), F32)] * 2 + [pltpu.VMEM((ROW_BLOCK, LANES), F32)] * 2
            + [pltpu.VMEM((ROW_BLOCK, TILE), BF16)] * 2 + [pltpu.VMEM((ROW_BLOCK, LANES), F32)] * 2
            + [pltpu.VMEM((2 * TILE, LANES), F32), pltpu.VMEM((2 * TILE, 2 * LANES), F32)])


def _flash_pipeline(n_tiles, scores, values, stat_row, scratch):
    s0, s1, x0, x1, p0, p1, a0, a1, m_ref, acc_ref = scratch
    s_bufs, x_bufs, p_bufs, a_bufs = (s0, s1), (x0, x1), (p0, p1), (a0, a1)
    assert n_tiles % PIPE_UNROLL == 0 and PIPE_UNROLL % 2 == 0
    m_ref[...] = jnp.full(m_ref.shape, NEG_INF, F32)
    acc_ref[...] = jnp.zeros(acc_ref.shape, F32)
    p1[...] = jnp.zeros(p1.shape, BF16)
    a1[...] = jnp.ones(a1.shape, F32)

    def issue(t, par):
        s = scores(t, par)
        s_bufs[par][...] = s
        x_bufs[par][...] = jnp.broadcast_to(jnp.max(s, axis=-1, keepdims=True), (ROW_BLOCK, LANES))

    def accumulate(t, par):
        rows = pl.ds(stat_row(t, par), ROW_BLOCK)
        alpha = a_bufs[par][...]
        acc_ref[rows, :] = (jnp.concatenate([alpha, alpha], axis=1) * acc_ref[rows, :]
                            + _dot(p_bufs[par][...], values(t, par)))

    def step(t, par):
        issue(jnp.minimum(t + 1, n_tiles - 1), 1 - par)
        rows = pl.ds(stat_row(t, par), ROW_BLOCK)
        m_old = m_ref[rows, :]
        m_new = jnp.maximum(m_old, x_bufs[par][...])
        p_bufs[par][...] = jnp.exp((s_bufs[par][...] - jnp.concatenate([m_new] * (TILE // LANES), axis=1)
                                    ).astype(BF16))
        a_bufs[par][...] = jnp.exp(m_old - m_new)
        m_ref[rows, :] = m_new
        accumulate(jnp.maximum(t - 1, 0), 1 - par)

    issue(0, 0)

    def body(u, carry):
        for k in range(PIPE_UNROLL):
            step(PIPE_UNROLL * u + k, k % 2)
        return carry

    lax.fori_loop(0, n_tiles // PIPE_UNROLL, body, 0)
    accumulate(n_tiles - 1, 1)


(_TAB_LEFT, _TAB_DIAG, _TAB_RIGHT, _TAB_FAR_LEFT, _TAB_FAR_RIGHT,
 _TAB_LEFT_LEAD, _TAB_DIAG_LEAD, _TAB_FAR_LEAD) = range(8)


def _ones_right(v):
    return jnp.concatenate([v, jnp.ones(v.shape, v.dtype)], axis=1)


def _attn_a_kernel(q_ref, k_ref, v_ref, tab_ref, dl_ref, gs_ref, o_ref, qq_ref, *scratch, lam_init, n_chunks):
    i = pl.program_id(2)
    q = q_ref[...]
    lane = lax.broadcasted_iota(jnp.int32, q.shape, 1)
    zero = jnp.zeros_like(q)
    qq_ref[0:TILE, :] = jnp.where(lane < A_QK_DIM, q, zero)
    qq_ref[TILE:2 * TILE, :] = jnp.where(lane >= A_QK_DIM, q, zero)
    blocks = 2 * TILE // ROW_BLOCK

    def q_row(t):
        return pl.multiple_of((t % blocks) * ROW_BLOCK, ROW_BLOCK)

    def k_row(t):
        return pl.multiple_of((t // blocks) * TILE, TILE)

    def scores(t, par):
        j = t // blocks
        lead = jnp.where(i == 0, _TAB_DIAG_LEAD, jnp.where(i == 1, _TAB_LEFT_LEAD, _TAB_FAR_LEAD))
        rest = jnp.where(j < i - 1, _TAB_FAR_LEFT, jnp.where(j > i + 1, _TAB_FAR_RIGHT, j - i + 1))
        table = jnp.where(j == 0, lead, rest)
        row = pl.multiple_of((t % (TILE // ROW_BLOCK)) * ROW_BLOCK, ROW_BLOCK)
        return (_dot_nt(qq_ref[pl.ds(q_row(t), ROW_BLOCK), :], k_ref[pl.ds(k_row(t), TILE), :])
                + tab_ref[table, pl.ds(row, ROW_BLOCK), :])

    def values(t, par):
        return _ones_right(v_ref[pl.ds(k_row(t), TILE), :])

    _flash_pipeline(n_chunks * blocks, scores, values, lambda t, par: q_row(t), scratch)

    acc_ref = scratch[-1]
    o = acc_ref[:, 0:A_V_DIM] / acc_ref[:, A_V_DIM:2 * A_V_DIM]
    dl = dl_ref[...]
    lam = (jnp.exp(jnp.sum(dl[0:1] * dl[1:2], axis=-1, keepdims=True))
           - jnp.exp(jnp.sum(dl[2:3] * dl[3:4], axis=-1, keepdims=True)) + lam_init)
    w = o[0:TILE] - lam * o[TILE:2 * TILE]
    o_ref[...] = (_rms(w, gs_ref[...]) * (1.0 - lam_init)).astype(BF16)


def _attn_a(aq, ak, av, tab, dl, gs, lam_init):
    nb, lp, _ = aq.shape
    n_chunks = lp // TILE
    qspec = pl.BlockSpec((None, TILE, LANES), lambda b, h, t: (b, t, h))
    kvspec = pl.BlockSpec((None, lp, LANES), lambda b, h, t: (b, 0, h))
    return pl.pallas_call(
        functools.partial(_attn_a_kernel, lam_init=lam_init, n_chunks=n_chunks),
        grid=(nb, A_HEADS, n_chunks),
        in_specs=[qspec, kvspec, kvspec,
                  pl.BlockSpec((None,) + tab.shape[1:], lambda b, h, t: (h, 0, 0, 0)),
                  _resident((4, A_QK_DIM)), _resident((1, A_V_DIM))],
        out_specs=qspec,
        out_shape=jax.ShapeDtypeStruct((nb, lp, A_HEADS * A_V_DIM), BF16),
        scratch_shapes=[pltpu.VMEM((2 * TILE, LANES), BF16)] + _flash_scratch(),
        compiler_params=_params(3),
        name="attn_a",
    )(aq, ak, av, tab, dl, gs.reshape(1, A_V_DIM))


def _attn_c_kernel(q_ref, k_ref, v_ref, kmask_ref, o_ref, *scratch, n_chunks):
    blocks = 2 * TILE // ROW_BLOCK

    def q_row(t):
        return pl.multiple_of(((t % blocks) // 2) * ROW_BLOCK, ROW_BLOCK)

    def k_row(t):
        return pl.multiple_of((t // blocks) * TILE, TILE)

    def scores(t, par):
        cols = slice(par * LANES, (par + 1) * LANES)
        return (_dot_nt(q_ref[pl.ds(q_row(t), ROW_BLOCK), cols], k_ref[pl.ds(k_row(t), TILE), cols])
                + kmask_ref[jnp.where(t < blocks, 1, 0)])

    def values(t, par):
        return _ones_right(v_ref[pl.ds(k_row(t), TILE), :])

    def stat_row(t, par):
        return pl.multiple_of(par * TILE + q_row(t), ROW_BLOCK)

    _flash_pipeline(n_chunks * blocks, scores, values, stat_row, scratch)

    acc_ref = scratch[-1]
    o = acc_ref[:, 0:LANES] / acc_ref[:, LANES:2 * LANES]
    lane = lax.broadcasted_iota(jnp.int32, (TILE, LANES), 1)
    o_ref[...] = jnp.where(lane < C_V_DIM, o[0:TILE], o[TILE:2 * TILE]).astype(BF16)


def _attn_c(qc, kc, vc, kmask):
    nb, lp, _ = qc.shape
    n_chunks = lp // TILE
    return pl.pallas_call(
        functools.partial(_attn_c_kernel, n_chunks=n_chunks),
        grid=(nb, C_HEADS // 2, n_chunks),
        in_specs=[pl.BlockSpec((None, TILE, 2 * LANES), lambda b, p, t: (b, t, p)),
                  pl.BlockSpec((None, lp, 2 * LANES), lambda b, p, t: (b, 0, p)),
                  pl.BlockSpec((None, lp, LANES), lambda b, p, t: (b, 0, p)),
                  _resident(kmask.shape)],
        out_specs=pl.BlockSpec((None, TILE, LANES), lambda b, p, t: (b, t, p)),
        out_shape=jax.ShapeDtypeStruct((nb, lp, C_HEADS * C_V_DIM), BF16),
        scratch_shapes=_flash_scratch(),
        compiler_params=_params(3),
        name="attn_c",
    )(qc, kc, vc, kmask)


def _attn_b_kernel(sink_ref, q_ref, k_ref, v_ref, vs_ref, tab_ref, o_ref, *, lp):
    t = pl.program_id(1)
    lane = lax.broadcasted_iota(jnp.int32, (BLOCK, LANES), 1)
    kk = lax.broadcasted_iota(jnp.int32, (1, 3 * BLOCK), 1)
    for blk in range(TILE // BLOCK):
        gblk = t * (TILE // BLOCK) + blk
        row0 = pl.multiple_of(gblk * BLOCK, BLOCK)
        kw = k_ref[pl.ds(row0, 3 * BLOCK), :]
        vw = v_ref[pl.ds(row0, 3 * BLOCK), :]
        vsw = vs_ref[pl.ds(row0, 3 * BLOCK), :]
        kslot = (gblk - 1) * BLOCK + kk
        row_mask = jnp.where((kslot >= META_START) & (kslot < lp), 0.0, NEG_INF)
        rows = slice(blk * BLOCK, (blk + 1) * BLOCK)
        outs = []
        for h in range(B_HEADS):
            qh = q_ref[rows, h * LANES:(h + 1) * LANES]
            s = _dot_nt(qh, kw) + tab_ref[h] + row_mask
            sink = sink_ref[h]
            m = jnp.maximum(jnp.max(s, axis=-1, keepdims=True), sink)
            p = jnp.exp(s - m)
            denom = jnp.sum(p, axis=-1, keepdims=True) + jnp.exp(sink - m)
            outs.append(_dot(p.astype(BF16), vw if h in (0, 3) else vsw) / denom)
        o_ref[rows, 0:LANES] = jnp.where(lane < B_HEAD_DIM, outs[0], outs[1]).astype(BF16)
        o_ref[rows, LANES:2 * LANES] = jnp.where(lane < B_HEAD_DIM, outs[2], outs[3]).astype(BF16)


def _attn_b(bq, bk_pad, bv_pad, bvs_pad, tab, sinks):
    nb, lp, _ = bq.shape
    kvspec = pl.BlockSpec((None, lp + 2 * BLOCK, LANES), lambda b, t: (b, 0, 0))
    return pl.pallas_call(
        functools.partial(_attn_b_kernel, lp=lp),
        grid=(nb, lp // TILE),
        in_specs=[pl.BlockSpec(memory_space=pltpu.SMEM),
                  pl.BlockSpec((None, TILE, B_HEADS * LANES), lambda b, t: (b, t, 0)),
                  kvspec, kvspec, kvspec, _resident(tab.shape)],
        out_specs=pl.BlockSpec((None, TILE, B_HEADS * B_HEAD_DIM), lambda b, t: (b, t, 0)),
        out_shape=jax.ShapeDtypeStruct((nb, lp, B_HEADS * B_HEAD_DIM), BF16),
        compiler_params=_params(2),
        name="attn_b",
    )(sinks, bq, bk_pad, bv_pad, bvs_pad, tab)


def _outproj_kernel(h_ref, oa_ref, ob_ref, oc_ref, wa_ref, wb_ref, wc_ref, o_ref):
    o_ref[...] = (h_ref[...] + _dot(oa_ref[...], wa_ref[...]) + _dot(ob_ref[...], wb_ref[...])
                  + _dot(oc_ref[...], wc_ref[...]))


def _outproj(h, oa, ob, oc, wa, wb, wc):
    nb, lp, d = h.shape

    def tile(n):
        return pl.BlockSpec((None, TILE, n), lambda b, t: (b, t, 0))

    return pl.pallas_call(
        _outproj_kernel,
        grid=(nb, lp // TILE),
        in_specs=[tile(d), tile(oa.shape[-1]), tile(ob.shape[-1]), tile(oc.shape[-1]),
                  _resident(wa.shape), _resident(wb.shape), _resident(wc.shape)],
        out_specs=tile(d),
        out_shape=jax.ShapeDtypeStruct(h.shape, F32),
        compiler_params=_params(2),
        name="outproj",
    )(h, oa, ob, oc, wa, wb, wc)


def _t5_bucket(rel):
    half = N_BUCKETS // 2
    max_exact = half // 2
    ret = jnp.where(rel > 0, half, 0)
    n = jnp.abs(rel)
    nf = jnp.maximum(n, 1).astype(F32)
    large = max_exact + (jnp.log(nf / max_exact) / math.log(MAX_DISTANCE / max_exact)
                         * (half - max_exact)).astype(jnp.int32)
    large = jnp.minimum(large, half - 1)
    return ret + jnp.where(n < max_exact, n, large)


def _rot_cols(w):
    half = w.shape[-1] // 2
    return jnp.concatenate([-w[..., half:], w[..., :half]], axis=-1)


def _toeplitz(strip, n_rows, n_cols):
    nh, length = strip.shape
    rows = jnp.broadcast_to(strip[:, None, :], (nh, n_rows, length))
    skew = jnp.pad(rows, ((0, 0), (0, 0), (0, 1))).reshape(nh, -1)[:, :n_rows * length].reshape(nh, n_rows, length)
    return skew[:, :, n_rows - 1:n_rows - 1 + n_cols]


def _bias_tables(rel_bias, lp):
    rb = rel_bias.astype(F32)
    strip_a = rb[:, :A_HEADS][_t5_bucket(jnp.arange(-(2 * TILE - 1), 2 * TILE))].T
    wide = _toeplitz(strip_a, TILE, 3 * TILE)
    left, diag, right = (wide[:, :, d * TILE:(d + 1) * TILE] for d in range(3))
    far_a = rb[:, :A_HEADS][_t5_bucket(jnp.array([-2 * TILE, 2 * TILE]))].T
    lead = jnp.where(jnp.arange(TILE) >= META_START, 0.0, NEG_INF).astype(F32).reshape(1, 1, TILE)
    far_left = jnp.broadcast_to(far_a[:, 0][:, None, None], left.shape)
    far_right = jnp.broadcast_to(far_a[:, 1][:, None, None], left.shape)
    tab_a = jnp.stack([left, diag, right, far_left, far_right, left + lead, diag + lead, far_left + lead], axis=1)
    kmask = jnp.concatenate([jnp.zeros_like(lead), lead], axis=0)
    relb = (jnp.arange(3 * BLOCK) - BLOCK)[None, :] - jnp.arange(BLOCK)[:, None]
    strip_b = rb[:, A_HEADS:][_t5_bucket(jnp.arange(-(2 * BLOCK - 1), 2 * BLOCK))].T
    tab_b = jnp.where((jnp.abs(relb) <= WINDOW)[None], _toeplitz(strip_b, BLOCK, 3 * BLOCK), NEG_INF)
    slot = jnp.arange(lp)
    inv = ROPE_THETA ** (-jnp.arange(0, C_ROPE_DIM, 2, dtype=F32) / C_ROPE_DIM)
    ang = (slot - META_START).astype(F32)[:, None] * inv[None, :]
    ang = jnp.concatenate([ang, ang], axis=-1)
    ones = jnp.ones((lp, C_NOPE_DIM), F32)
    zeros_n = jnp.zeros((lp, C_NOPE_DIM), F32)
    zeros_p = jnp.zeros((lp, LANES - C_NOPE_DIM - C_ROPE_DIM), F32)
    cos_t = jnp.concatenate([ones, jnp.cos(ang), zeros_p], axis=-1)
    sin_t = jnp.concatenate([zeros_n, jnp.sin(ang), zeros_p], axis=-1)
    return tab_a, tab_b, kmask, cos_t, sin_t


def _layer_weights(w_in, w_uq, w_ukv, w_out):
    d = w_in.shape[0]
    sizes = (512, 512, 512, 256, 128, 128, 256, 128, 32)
    offs = [0]
    for s in sizes:
        offs.append(offs[-1] + s)
    aq, ak, av, bq, bk, bv, cq, ckv, kr = (w_in[:, offs[i]:offs[i + 1]] for i in range(9))
    zeros64 = jnp.zeros((d, B_HEAD_DIM), F32)
    bq_pad = []
    for h in range(B_HEADS):
        qh = bq[:, h * B_HEAD_DIM:(h + 1) * B_HEAD_DIM] * (B_HEAD_DIM ** -0.5)
        bq_pad += [qh, zeros64] if h // (B_HEADS // B_KV_HEADS) == 0 else [zeros64, qh]
    bvs = jnp.concatenate([bv[:, B_HEAD_DIM:], bv[:, :B_HEAD_DIM]], axis=1)
    pad_lo = jnp.zeros((d, C_NOPE_DIM), F32)
    pad_hi = jnp.zeros((d, LANES - C_NOPE_DIM - C_ROPE_DIM), F32)
    w_big = jnp.concatenate(
        [aq * (A_QK_DIM ** -0.5), ak, av] + bq_pad + [bk, bv, bvs, cq, ckv,
                                                      pad_lo, kr, pad_hi, pad_lo, _rot_cols(kr), pad_hi],
        axis=1).astype(BF16)
    uq = w_uq.reshape(C_Q_RANK, C_HEADS, C_NOPE_DIM + C_ROPE_DIM)
    uq_nope, uq_rope = uq[..., :C_NOPE_DIM], uq[..., C_NOPE_DIM:]
    zq = jnp.zeros((C_Q_RANK, C_HEADS, LANES - C_NOPE_DIM - C_ROPE_DIM), F32)
    wuqm = jnp.concatenate([uq_nope, uq_rope, zq], axis=-1).reshape(C_Q_RANK, C_HEADS * LANES).astype(BF16)
    wuqr = jnp.concatenate([jnp.zeros_like(uq_nope), _rot_cols(uq_rope), zq], axis=-1
                           ).reshape(C_Q_RANK, C_HEADS * LANES).astype(BF16)
    ukv = w_ukv.reshape(C_KV_RANK, C_HEADS, C_NOPE_DIM + C_V_DIM)
    wukvk = jnp.concatenate([ukv[..., :C_NOPE_DIM], jnp.zeros((C_KV_RANK, C_HEADS, LANES - C_NOPE_DIM), F32)],
                            axis=-1).reshape(C_KV_RANK, C_HEADS * LANES).astype(BF16)
    wukvv = ukv[..., C_NOPE_DIM:].reshape(C_KV_RANK, C_HEADS * C_V_DIM).astype(BF16)
    n_a = A_HEADS * A_V_DIM
    n_b = B_HEADS * B_HEAD_DIM
    wo = w_out.astype(BF16)
    return w_big, wuqm, wuqr, wukvk, wukvv, wo[:n_a], wo[n_a:n_a + n_b], wo[n_a + n_b:]


def _trunk(x, meta, rel_bias, g_ffn1, w_ffn1_gu, w_ffn1_down, g_mix, w_in, diff_lambda, g_subln, sinks, g_cq,
           g_ckv, w_uq, w_ukv, w_out, g_ffn2, w_ffn2_gu, w_ffn2_down, g_final):
    nb, seq, d = x.shape
    lp = BLOCK + seq
    assert lp % TILE == 0 and d == D_MODEL
    lead = jnp.concatenate([jnp.zeros((META_START, d), x.dtype), meta.astype(x.dtype)], axis=0)
    h = jnp.concatenate([jnp.broadcast_to(lead[None], (nb, BLOCK, d)), x], axis=1)
    tab_a, tab_b, kmask, cos_t, sin_t = _bias_tables(rel_bias, lp)
    pad = ((0, 0), (BLOCK, BLOCK), (0, 0))
    for l in range(DEPTH):
        w_big, wuqm, wuqr, wukvk, wukvv, wo_a, wo_b, wo_c = _layer_weights(w_in[l], w_uq[l], w_ukv[l], w_out[l])
        h = _ffn(h, g_ffn1[l], w_ffn1_gu[l].astype(BF16), w_ffn1_down[l].astype(BF16))
        aq, ak, av, bq, bk, bv, bvs, qc, kc, vc = _inproj(h, g_mix[l], w_big, g_cq[l], g_ckv[l], wuqm, wuqr,
                                                           wukvk, wukvv, cos_t, sin_t)
        lam_init = 0.8 - 0.6 * math.exp(-0.3 * l)
        oa = _attn_a(aq, ak, av, tab_a, diff_lambda[l].astype(F32), g_subln[l], lam_init)
        ob = _attn_b(bq, jnp.pad(bk, pad), jnp.pad(bv, pad), jnp.pad(bvs, pad), tab_b, sinks[l].astype(F32))
        oc = _attn_c(qc, kc, vc, kmask)
        h = _outproj(h, oa, ob, oc, wo_a, wo_b, wo_c)
        h = _ffn(h, g_ffn2[l], w_ffn2_gu[l].astype(BF16), w_ffn2_down[l].astype(BF16),
                 g_final=g_final if l == DEPTH - 1 else None)
    return h[:, BLOCK:]


def kernel(x_prompt, x_sample, meta, rel_bias, g_ffn1, w_ffn1_gu, w_ffn1_down, g_mix, w_in, diff_lambda, g_subln,
           sinks, g_cq, g_ckv, w_uq, w_ukv, w_out, g_ffn2, w_ffn2_gu, w_ffn2_down, g_final):
    n_prompt = x_prompt.shape[0]
    x = jnp.concatenate([x_prompt, x_sample], axis=0)
    y = _trunk(x, meta, rel_bias, g_ffn1, w_ffn1_gu, w_ffn1_down, g_mix, w_in, diff_lambda, g_subln, sinks, g_cq,
               g_ckv, w_uq, w_ukv, w_out, g_ffn2, w_ffn2_gu, w_ffn2_down, g_final)
    return (y[:n_prompt], y[n_prompt:])
```

```python
import functools
import math

import jax
import jax.numpy as jnp
from jax import lax
from jax.experimental import pallas as pl
from jax.experimental.pallas import tpu as pltpu

D_MODEL = 1024
DEPTH = 2
N_META = 16
BLOCK = 128
META_START = BLOCK - N_META
A_HEADS = 4
A_QK_DIM = 64
A_V_DIM = 2 * A_QK_DIM
B_HEADS = 4
B_KV_HEADS = 2
B_HEAD_DIM = 64
WINDOW = 128
C_HEADS = 4
C_Q_RANK = 256
C_KV_RANK = 128
C_NOPE_DIM = 64
C_ROPE_DIM = 32
C_V_DIM = 64
ROPE_THETA = 10000.0
N_BUCKETS = 32
MAX_DISTANCE = 128
D_FF = 2816
EPS = 1e-6

LANES = 128
TILE = 5 * BLOCK
FF_CHUNK = D_FF // 2
ROW_BLOCK = TILE // 2
PIPE_UNROLL = 26
VMEM_LIMIT = 56 * 1024 * 1024

F32 = jnp.float32
BF16 = jnp.bfloat16
NEG_INF = float("-inf")

_W_AQ, _W_AK, _W_AV, _W_BQ = 0, 512, 1024, 1536
_W_BK, _W_BV, _W_BVS = 2048, 2176, 2304
_W_CQ, _W_CKV, _W_KR, _W_KRR = 2432, 2688, 2816, 2944
_W_TOTAL = 3072


def _params(n_axes):
    return pltpu.CompilerParams(dimension_semantics=("arbitrary",) * n_axes, vmem_limit_bytes=VMEM_LIMIT)


def _resident(shape):
    return pl.BlockSpec(shape, lambda *_: (0,) * len(shape), pipeline_mode=pl.Buffered(1))


def _rms(x, g):
    return x * lax.rsqrt(jnp.mean(x * x, axis=-1, keepdims=True) + EPS) * g


def _dot(a, b):
    return jnp.dot(a, b, preferred_element_type=F32)


def _dot_nt(a, b):
    return lax.dot_general(a, b, (((1,), (1,)), ((), ())), preferred_element_type=F32)


def _ffn_kernel(x_ref, g_ref, wgu_ref, wd_ref, *rest, final_norm):
    if final_norm:
        gf_ref, o_ref = rest
    else:
        (o_ref,) = rest
    x = x_ref[...]
    xn = _rms(x, g_ref[...]).astype(BF16)
    acc = None
    for lo in range(0, D_FF, FF_CHUNK):
        g = _dot(xn, wgu_ref[:, lo:lo + FF_CHUNK])
        u = _dot(xn, wgu_ref[:, D_FF + lo:D_FF + lo + FF_CHUNK])
        a = (g * jax.nn.sigmoid(g) * u).astype(BF16)
        d = _dot(a, wd_ref[lo:lo + FF_CHUNK, :])
        acc = d if acc is None else acc + d
    y = x + 0.5 * acc
    if final_norm:
        y = _rms(y, gf_ref[...])
    o_ref[...] = y


def _ffn(h, g, wgu, wd, g_final=None):
    nb, lp, d = h.shape
    tile = pl.BlockSpec((None, TILE, d), lambda b, t: (b, t, 0))
    in_specs = [tile, _resident((1, d)), _resident(wgu.shape), _resident(wd.shape)]
    args = [h, g.reshape(1, d), wgu, wd]
    if g_final is not None:
        in_specs.append(_resident((1, d)))
        args.append(g_final.reshape(1, d))
    return pl.pallas_call(
        functools.partial(_ffn_kernel, final_norm=g_final is not None),
        grid=(nb, lp // TILE),
        in_specs=in_specs,
        out_specs=tile,
        out_shape=jax.ShapeDtypeStruct(h.shape, F32),
        compiler_params=_params(2),
        name="ffn",
    )(*args)


def _inproj_kernel(h_ref, g_ref, w_ref, gcq_ref, gckv_ref, wuqm_ref, wuqr_ref, wukvk_ref, wukvv_ref,
                   cos_ref, sin_ref,
                   aq_ref, ak_ref, av_ref, bq_ref, bk_ref, bv_ref, bvs_ref, qc_ref, kc_ref, vc_ref, *, c_scale):
    xn = _rms(h_ref[...], g_ref[...]).astype(BF16)

    def proj(lo, n):
        return _dot(xn, w_ref[:, lo:lo + n])

    aq_ref[...] = proj(_W_AQ, 512).astype(BF16)
    ak_ref[...] = proj(_W_AK, 512).astype(BF16)
    av_ref[...] = proj(_W_AV, 512).astype(BF16)
    bq_ref[...] = proj(_W_BQ, 512).astype(BF16)
    bk_ref[...] = proj(_W_BK, 128).astype(BF16)
    bv_ref[...] = proj(_W_BV, 128).astype(BF16)
    bvs_ref[...] = proj(_W_BVS, 128).astype(BF16)

    cos = cos_ref[...]
    sin = sin_ref[...]
    cos4 = jnp.concatenate([cos] * C_HEADS, axis=1)
    sin4 = jnp.concatenate([sin] * C_HEADS, axis=1)
    cqn = _rms(proj(_W_CQ, C_Q_RANK), gcq_ref[...]).astype(BF16)
    q = _dot(cqn, wuqm_ref[...]) * cos4 + _dot(cqn, wuqr_ref[...]) * sin4
    qc_ref[...] = (q * c_scale).astype(BF16)

    ckvn = _rms(proj(_W_CKV, C_KV_RANK), gckv_ref[...]).astype(BF16)
    k_rope = proj(_W_KR, LANES) * cos + proj(_W_KRR, LANES) * sin
    kc_ref[...] = (_dot(ckvn, wukvk_ref[...]) + jnp.concatenate([k_rope] * C_HEADS, axis=1)).astype(BF16)
    vc_ref[...] = _dot(ckvn, wukvv_ref[...]).astype(BF16)


def _inproj(h, g, w_big, gcq, gckv, wuqm, wuqr, wukvk, wukvv, cos_t, sin_t):
    nb, lp, d = h.shape

    def tile(n):
        return pl.BlockSpec((None, TILE, n), lambda b, t: (b, t, 0))

    pos = pl.BlockSpec((TILE, LANES), lambda b, t: (t, 0))
    widths = (512, 512, 512, 512, 128, 128, 128, 512, 512, 256)
    return pl.pallas_call(
        functools.partial(_inproj_kernel, c_scale=(C_NOPE_DIM + C_ROPE_DIM) ** -0.5),
        grid=(nb, lp // TILE),
        in_specs=[tile(d), _resident((1, d)), _resident(w_big.shape), _resident((1, C_Q_RANK)),
                  _resident((1, C_KV_RANK)), _resident(wuqm.shape), _resident(wuqr.shape),
                  _resident(wukvk.shape), _resident(wukvv.shape), pos, pos],
        out_specs=[tile(n) for n in widths],
        out_shape=[jax.ShapeDtypeStruct((nb, lp, n), BF16) for n in widths],
        compiler_params=_params(2),
        name="inproj",
    )(h, g.reshape(1, d), w_big, gcq.reshape(1, -1), gckv.reshape(1, -1), wuqm, wuqr, wukvk, wukvv, cos_t, sin_t)


def _static(x):
    return isinstance(x, (int, bool))


def _aligned(x, m):
    return x if _static(x) else pl.multiple_of(x, m)


def _imin(a, b):
    return min(a, b) if _static(a) else jnp.minimum(a, b)


def _imax(a, b):
    return max(a, b) if _static(a) else jnp.maximum(a, b)


def _iselect(c, a, b):
    return (a if c else b) if _static(c) else jnp.where(c, a, b)


def _flash_scratch():
    return ([pltpu.VMEM((ROW_BLOCK, TILE), F32)] * 2 + [pltpu.VMEM((ROW_BLOCK, TILE), BF16)] * 2
            + [pltpu.VMEM((ROW_BLOCK, LANES), F32)] * 2
            + [pltpu.VMEM((2 * TILE, LANES), F32), pltpu.VMEM((2 * TILE, 2 * LANES), F32)])


def _flash_pipeline(n_tiles, scores, values, stat_row, scratch):
    s0, s1, p0, p1, a0, a1, m_ref, acc_ref = scratch
    s_bufs, p_bufs, a_bufs = (s0, s1), (p0, p1), (a0, a1)
    unroll = max(k for k in range(2, PIPE_UNROLL + 1, 2) if n_tiles % k == 0)
    m_ref[...] = jnp.full(m_ref.shape, NEG_INF, F32)
    acc_ref[...] = jnp.zeros(acc_ref.shape, F32)
    p1[...] = jnp.zeros(p1.shape, BF16)
    a1[...] = jnp.ones(a1.shape, F32)

    def issue(t, par):
        s_bufs[par][...] = scores(t, par)

    def accumulate(t, par):
        rows = pl.ds(stat_row(t, par), ROW_BLOCK)
        alpha = a_bufs[par][...]
        acc_ref[rows, :] = (jnp.concatenate([alpha, alpha], axis=1) * acc_ref[rows, :]
                            + _dot(p_bufs[par][...], values(t, par)))

    def step(t, par):
        issue(_imin(t + 1, n_tiles - 1), 1 - par)
        rows = pl.ds(stat_row(t, par), ROW_BLOCK)
        m_old = m_ref[rows, :]
        m_new = jnp.maximum(m_old, jnp.max(s_bufs[par][...], axis=-1, keepdims=True))
        p_bufs[par][...] = jnp.exp((s_bufs[par][...] - jnp.concatenate([m_new] * (TILE // LANES), axis=1)
                                    ).astype(BF16))
        a_bufs[par][...] = jnp.exp(m_old - m_new)
        m_ref[rows, :] = m_new
        accumulate(_imax(t - 1, 0), 1 - par)

    issue(0, 0)

    def body(u, carry):
        for k in range(unroll):
            step(unroll * u + k, k % 2)
        return carry

    if unroll == n_tiles:
        for k in range(n_tiles):
            step(k, k % 2)
    else:
        lax.fori_loop(0, n_tiles // unroll, body, 0)
    accumulate(n_tiles - 1, 1)


(_TAB_LEFT, _TAB_DIAG, _TAB_RIGHT, _TAB_FAR_LEFT, _TAB_FAR_RIGHT,
 _TAB_LEFT_LEAD, _TAB_DIAG_LEAD, _TAB_FAR_LEAD) = range(8)


def _ones_right(v):
    return jnp.concatenate([v, jnp.ones(v.shape, v.dtype)], axis=1)


def _attn_a_kernel(q_ref, k_ref, v_ref, tab_ref, dl_ref, gs_ref, o_ref, qq_ref, *scratch, lam_init, n_chunks):
    i = pl.program_id(2)
    q = q_ref[...]
    lane = lax.broadcasted_iota(jnp.int32, q.shape, 1)
    zero = jnp.zeros_like(q)
    qq_ref[0:TILE, :] = jnp.where(lane < A_QK_DIM, q, zero)
    qq_ref[TILE:2 * TILE, :] = jnp.where(lane >= A_QK_DIM, q, zero)
    blocks = 2 * TILE // ROW_BLOCK

    def q_row(t):
        return _aligned((t % blocks) * ROW_BLOCK, ROW_BLOCK)

    def k_row(t):
        return _aligned((t // blocks) * TILE, TILE)

    def scores(t, par):
        j = t // blocks
        lead = jnp.where(i == 0, _TAB_DIAG_LEAD, jnp.where(i == 1, _TAB_LEFT_LEAD, _TAB_FAR_LEAD))
        rest = jnp.where(j < i - 1, _TAB_FAR_LEFT, jnp.where(j > i + 1, _TAB_FAR_RIGHT, j - i + 1))
        table = jnp.where(j == 0, lead, rest)
        row = _aligned((t % (TILE // ROW_BLOCK)) * ROW_BLOCK, ROW_BLOCK)
        return (_dot_nt(qq_ref[pl.ds(q_row(t), ROW_BLOCK), :], k_ref[pl.ds(k_row(t), TILE), :])
                + tab_ref[table, pl.ds(row, ROW_BLOCK), :])

    def values(t, par):
        return _ones_right(v_ref[pl.ds(k_row(t), TILE), :])

    _flash_pipeline(n_chunks * blocks, scores, values, lambda t, par: q_row(t), scratch)

    acc_ref = scratch[-1]
    o = acc_ref[:, 0:A_V_DIM] / acc_ref[:, A_V_DIM:2 * A_V_DIM]
    dl = dl_ref[...]
    lam = (jnp.exp(jnp.sum(dl[0:1] * dl[1:2], axis=-1, keepdims=True))
           - jnp.exp(jnp.sum(dl[2:3] * dl[3:4], axis=-1, keepdims=True)) + lam_init)
    w = o[0:TILE] - lam * o[TILE:2 * TILE]
    o_ref[...] = (_rms(w, gs_ref[...]) * (1.0 - lam_init)).astype(BF16)


def _attn_a(aq, ak, av, tab, dl, gs, lam_init):
    nb, lp, _ = aq.shape
    n_chunks = lp // TILE
    qspec = pl.BlockSpec((None, TILE, LANES), lambda b, h, t: (b, t, h))
    kvspec = pl.BlockSpec((None, lp, LANES), lambda b, h, t: (b, 0, h))
    return pl.pallas_call(
        functools.partial(_attn_a_kernel, lam_init=lam_init, n_chunks=n_chunks),
        grid=(nb, A_HEADS, n_chunks),
        in_specs=[qspec, kvspec, kvspec,
                  pl.BlockSpec((None,) + tab.shape[1:], lambda b, h, t: (h, 0, 0, 0)),
                  _resident((4, A_QK_DIM)), _resident((1, A_V_DIM))],
        out_specs=qspec,
        out_shape=jax.ShapeDtypeStruct((nb, lp, A_HEADS * A_V_DIM), BF16),
        scratch_shapes=[pltpu.VMEM((2 * TILE, LANES), BF16)] + _flash_scratch(),
        compiler_params=_params(3),
        name="attn_a",
    )(aq, ak, av, tab, dl, gs.reshape(1, A_V_DIM))


def _attn_c_kernel(q_ref, k_ref, v_ref, kmask_ref, o_ref, *scratch, n_chunks):
    blocks = 2 * TILE // ROW_BLOCK

    def q_row(t):
        return _aligned(((t % blocks) // 2) * ROW_BLOCK, ROW_BLOCK)

    def k_row(t):
        return _aligned((t // blocks) * TILE, TILE)

    def scores(t, par):
        cols = slice(par * LANES, (par + 1) * LANES)
        return (_dot_nt(q_ref[pl.ds(q_row(t), ROW_BLOCK), cols], k_ref[pl.ds(k_row(t), TILE), cols])
                + kmask_ref[_iselect(t < blocks, 1, 0)])

    def values(t, par):
        return _ones_right(v_ref[pl.ds(k_row(t), TILE), :])

    def stat_row(t, par):
        return _aligned(par * TILE + q_row(t), ROW_BLOCK)

    _flash_pipeline(n_chunks * blocks, scores, values, stat_row, scratch)

    acc_ref = scratch[-1]
    o = acc_ref[:, 0:LANES] / acc_ref[:, LANES:2 * LANES]
    lane = lax.broadcasted_iota(jnp.int32, (TILE, LANES), 1)
    o_ref[...] = jnp.where(lane < C_V_DIM, o[0:TILE], o[TILE:2 * TILE]).astype(BF16)


def _attn_c(qc, kc, vc, kmask):
    nb, lp, _ = qc.shape
    n_chunks = lp // TILE
    return pl.pallas_call(
        functools.partial(_attn_c_kernel, n_chunks=n_chunks),
        grid=(nb, C_HEADS // 2, n_chunks),
        in_specs=[pl.BlockSpec((None, TILE, 2 * LANES), lambda b, p, t: (b, t, p)),
                  pl.BlockSpec((None, lp, 2 * LANES), lambda b, p, t: (b, 0, p)),
                  pl.BlockSpec((None, lp, LANES), lambda b, p, t: (b, 0, p)),
                  _resident(kmask.shape)],
        out_specs=pl.BlockSpec((None, TILE, LANES), lambda b, p, t: (b, t, p)),
        out_shape=jax.ShapeDtypeStruct((nb, lp, C_HEADS * C_V_DIM), BF16),
        scratch_shapes=_flash_scratch(),
        compiler_params=_params(3),
        name="attn_c",
    )(qc, kc, vc, kmask)


def _attn_b_kernel(sink_ref, q_ref, k_ref, v_ref, vs_ref, tab_ref, o_ref, *, lp):
    t = pl.program_id(1)
    lane = lax.broadcasted_iota(jnp.int32, (BLOCK, LANES), 1)
    kk = lax.broadcasted_iota(jnp.int32, (1, 3 * BLOCK), 1)
    for blk in range(TILE // BLOCK):
        gblk = t * (TILE // BLOCK) + blk
        row0 = pl.multiple_of(gblk * BLOCK, BLOCK)
        kw = k_ref[pl.ds(row0, 3 * BLOCK), :]
        vw = v_ref[pl.ds(row0, 3 * BLOCK), :]
        vsw = vs_ref[pl.ds(row0, 3 * BLOCK), :]
        kslot = (gblk - 1) * BLOCK + kk
        row_mask = jnp.where((kslot >= META_START) & (kslot < lp), 0.0, NEG_INF)
        rows = slice(blk * BLOCK, (blk + 1) * BLOCK)
        outs = []
        for h in range(B_HEADS):
            qh = q_ref[rows, h * LANES:(h + 1) * LANES]
            s = _dot_nt(qh, kw) + tab_ref[h] + row_mask
            sink = sink_ref[h]
            m = jnp.maximum(jnp.max(s, axis=-1, keepdims=True), sink)
            p = jnp.exp(s - m)
            denom = jnp.sum(p, axis=-1, keepdims=True) + jnp.exp(sink - m)
            outs.append(_dot(p.astype(BF16), vw if h in (0, 3) else vsw) / denom)
        o_ref[rows, 0:LANES] = jnp.where(lane < B_HEAD_DIM, outs[0], outs[1]).astype(BF16)
        o_ref[rows, LANES:2 * LANES] = jnp.where(lane < B_HEAD_DIM, outs[2], outs[3]).astype(BF16)


def _attn_b(bq, bk_pad, bv_pad, bvs_pad, tab, sinks):
    nb, lp, _ = bq.shape
    kvspec = pl.BlockSpec((None, lp + 2 * BLOCK, LANES), lambda b, t: (b, 0, 0))
    return pl.pallas_call(
        functools.partial(_attn_b_kernel, lp=lp),
        grid=(nb, lp // TILE),
        in_specs=[pl.BlockSpec(memory_space=pltpu.SMEM),
                  pl.BlockSpec((None, TILE, B_HEADS * LANES), lambda b, t: (b, t, 0)),
                  kvspec, kvspec, kvspec, _resident(tab.shape)],
        out_specs=pl.BlockSpec((None, TILE, B_HEADS * B_HEAD_DIM), lambda b, t: (b, t, 0)),
        out_shape=jax.ShapeDtypeStruct((nb, lp, B_HEADS * B_HEAD_DIM), BF16),
        compiler_params=_params(2),
        name="attn_b",
    )(sinks, bq, bk_pad, bv_pad, bvs_pad, tab)


def _outproj_kernel(h_ref, oa_ref, ob_ref, oc_ref, wa_ref, wb_ref, wc_ref, o_ref):
    o_ref[...] = (h_ref[...] + _dot(oa_ref[...], wa_ref[...]) + _dot(ob_ref[...], wb_ref[...])
                  + _dot(oc_ref[...], wc_ref[...]))


def _outproj(h, oa, ob, oc, wa, wb, wc):
    nb, lp, d = h.shape

    def tile(n):
        return pl.BlockSpec((None, TILE, n), lambda b, t: (b, t, 0))

    return pl.pallas_call(
        _outproj_kernel,
        grid=(nb, lp // TILE),
        in_specs=[tile(d), tile(oa.shape[-1]), tile(ob.shape[-1]), tile(oc.shape[-1]),
                  _resident(wa.shape), _resident(wb.shape), _resident(wc.shape)],
        out_specs=tile(d),
        out_shape=jax.ShapeDtypeStruct(h.shape, F32),
        compiler_params=_params(2),
        name="outproj",
    )(h, oa, ob, oc, wa, wb, wc)


def _t5_bucket(rel):
    half = N_BUCKETS // 2
    max_exact = half // 2
    ret = jnp.where(rel > 0, half, 0)
    n = jnp.abs(rel)
    nf = jnp.maximum(n, 1).astype(F32)
    large = max_exact + (jnp.log(nf / max_exact) / math.log(MAX_DISTANCE / max_exact)
                         * (half - max_exact)).astype(jnp.int32)
    large = jnp.minimum(large, half - 1)
    return ret + jnp.where(n < max_exact, n, large)


def _rot_cols(w):
    half = w.shape[-1] // 2
    return jnp.concatenate([-w[..., half:], w[..., :half]], axis=-1)


def _toeplitz(strip, n_rows, n_cols):
    nh, length = strip.shape
    rows = jnp.broadcast_to(strip[:, None, :], (nh, n_rows, length))
    skew = jnp.pad(rows, ((0, 0), (0, 0), (0, 1))).reshape(nh, -1)[:, :n_rows * length].reshape(nh, n_rows, length)
    return skew[:, :, n_rows - 1:n_rows - 1 + n_cols]


def _bias_tables(rel_bias, lp):
    rb = rel_bias.astype(F32)
    strip_a = rb[:, :A_HEADS][_t5_bucket(jnp.arange(-(2 * TILE - 1), 2 * TILE))].T
    wide = _toeplitz(strip_a, TILE, 3 * TILE)
    left, diag, right = (wide[:, :, d * TILE:(d + 1) * TILE] for d in range(3))
    far_a = rb[:, :A_HEADS][_t5_bucket(jnp.array([-2 * TILE, 2 * TILE]))].T
    lead = jnp.where(jnp.arange(TILE) >= META_START, 0.0, NEG_INF).astype(F32).reshape(1, 1, TILE)
    far_left = jnp.broadcast_to(far_a[:, 0][:, None, None], left.shape)
    far_right = jnp.broadcast_to(far_a[:, 1][:, None, None], left.shape)
    tab_a = jnp.stack([left, diag, right, far_left, far_right, left + lead, diag + lead, far_left + lead], axis=1)
    kmask = jnp.concatenate([jnp.zeros_like(lead), lead], axis=0)
    relb = (jnp.arange(3 * BLOCK) - BLOCK)[None, :] - jnp.arange(BLOCK)[:, None]
    strip_b = rb[:, A_HEADS:][_t5_bucket(jnp.arange(-(2 * BLOCK - 1), 2 * BLOCK))].T
    tab_b = jnp.where((jnp.abs(relb) <= WINDOW)[None], _toeplitz(strip_b, BLOCK, 3 * BLOCK), NEG_INF)
    slot = jnp.arange(lp)
    inv = ROPE_THETA ** (-jnp.arange(0, C_ROPE_DIM, 2, dtype=F32) / C_ROPE_DIM)
    ang = (slot - META_START).astype(F32)[:, None] * inv[None, :]
    ang = jnp.concatenate([ang, ang], axis=-1)
    ones = jnp.ones((lp, C_NOPE_DIM), F32)
    zeros_n = jnp.zeros((lp, C_NOPE_DIM), F32)
    zeros_p = jnp.zeros((lp, LANES - C_NOPE_DIM - C_ROPE_DIM), F32)
    cos_t = jnp.concatenate([ones, jnp.cos(ang), zeros_p], axis=-1)
    sin_t = jnp.concatenate([zeros_n, jnp.sin(ang), zeros_p], axis=-1)
    return tab_a, tab_b, kmask, cos_t, sin_t


def _layer_weights(w_in, w_uq, w_ukv, w_out):
    d = w_in.shape[0]
    sizes = (512, 512, 512, 256, 128, 128, 256, 128, 32)
    offs = [0]
    for s in sizes:
        offs.append(offs[-1] + s)
    aq, ak, av, bq, bk, bv, cq, ckv, kr = (w_in[:, offs[i]:offs[i + 1]] for i in range(9))
    zeros64 = jnp.zeros((d, B_HEAD_DIM), F32)
    bq_pad = []
    for h in range(B_HEADS):
        qh = bq[:, h * B_HEAD_DIM:(h + 1) * B_HEAD_DIM] * (B_HEAD_DIM ** -0.5)
        bq_pad += [qh, zeros64] if h // (B_HEADS // B_KV_HEADS) == 0 else [zeros64, qh]
    bvs = jnp.concatenate([bv[:, B_HEAD_DIM:], bv[:, :B_HEAD_DIM]], axis=1)
    pad_lo = jnp.zeros((d, C_NOPE_DIM), F32)
    pad_hi = jnp.zeros((d, LANES - C_NOPE_DIM - C_ROPE_DIM), F32)
    w_big = jnp.concatenate(
        [aq * (A_QK_DIM ** -0.5), ak, av] + bq_pad + [bk, bv, bvs, cq, ckv,
                                                      pad_lo, kr, pad_hi, pad_lo, _rot_cols(kr), pad_hi],
        axis=1).astype(BF16)
    uq = w_uq.reshape(C_Q_RANK, C_HEADS, C_NOPE_DIM + C_ROPE_DIM)
    uq_nope, uq_rope = uq[..., :C_NOPE_DIM], uq[..., C_NOPE_DIM:]
    zq = jnp.zeros((C_Q_RANK, C_HEADS, LANES - C_NOPE_DIM - C_ROPE_DIM), F32)
    wuqm = jnp.concatenate([uq_nope, uq_rope, zq], axis=-1).reshape(C_Q_RANK, C_HEADS * LANES).astype(BF16)
    wuqr = jnp.concatenate([jnp.zeros_like(uq_nope), _rot_cols(uq_rope), zq], axis=-1
                           ).reshape(C_Q_RANK, C_HEADS * LANES).astype(BF16)
    ukv = w_ukv.reshape(C_KV_RANK, C_HEADS, C_NOPE_DIM + C_V_DIM)
    wukvk = jnp.concatenate([ukv[..., :C_NOPE_DIM], jnp.zeros((C_KV_RANK, C_HEADS, LANES - C_NOPE_DIM), F32)],
                            axis=-1).reshape(C_KV_RANK, C_HEADS * LANES).astype(BF16)
    wukvv = ukv[..., C_NOPE_DIM:].reshape(C_KV_RANK, C_HEADS * C_V_DIM).astype(BF16)
    n_a = A_HEADS * A_V_DIM
    n_b = B_HEADS * B_HEAD_DIM
    wo = w_out.astype(BF16)
    return w_big, wuqm, wuqr, wukvk, wukvv, wo[:n_a], wo[n_a:n_a + n_b], wo[n_a + n_b:]


def _trunk(x, meta, rel_bias, g_ffn1, w_ffn1_gu, w_ffn1_down, g_mix, w_in, diff_lambda, g_subln, sinks, g_cq,
           g_ckv, w_uq, w_ukv, w_out, g_ffn2, w_ffn2_gu, w_ffn2_down, g_final):
    nb, seq, d = x.shape
    lp = BLOCK + seq
    assert lp % TILE == 0 and d == D_MODEL
    lead = jnp.concatenate([jnp.zeros((META_START, d), x.dtype), meta.astype(x.dtype)], axis=0)
    h = jnp.concatenate([jnp.broadcast_to(lead[None], (nb, BLOCK, d)), x], axis=1)
    tab_a, tab_b, kmask, cos_t, sin_t = _bias_tables(rel_bias, lp)
    pad = ((0, 0), (BLOCK, BLOCK), (0, 0))
    for l in range(DEPTH):
        w_big, wuqm, wuqr, wukvk, wukvv, wo_a, wo_b, wo_c = _layer_weights(w_in[l], w_uq[l], w_ukv[l], w_out[l])
        h = _ffn(h, g_ffn1[l], w_ffn1_gu[l].astype(BF16), w_ffn1_down[l].astype(BF16))
        aq, ak, av, bq, bk, bv, bvs, qc, kc, vc = _inproj(h, g_mix[l], w_big, g_cq[l], g_ckv[l], wuqm, wuqr,
                                                           wukvk, wukvv, cos_t, sin_t)
        lam_init = 0.8 - 0.6 * math.exp(-0.3 * l)
        oa = _attn_a(aq, ak, av, tab_a, diff_lambda[l].astype(F32), g_subln[l], lam_init)
        ob = _attn_b(bq, jnp.pad(bk, pad), jnp.pad(bv, pad), jnp.pad(bvs, pad), tab_b, sinks[l].astype(F32))
        oc = _attn_c(qc, kc, vc, kmask)
        h = _outproj(h, oa, ob, oc, wo_a, wo_b, wo_c)
        h = _ffn(h, g_ffn2[l], w_ffn2_gu[l].astype(BF16), w_ffn2_down[l].astype(BF16),
                 g_final=g_final if l == DEPTH - 1 else None)
    return h[:, BLOCK:]


def kernel(x_prompt, x_sample, meta, rel_bias, g_ffn1, w_ffn1_gu, w_ffn1_down, g_mix, w_in, diff_lambda, g_subln,
           sinks, g_cq, g_ckv, w_uq, w_ukv, w_out, g_ffn2, w_ffn2_gu, w_ffn2_down, g_final):
    n_prompt = x_prompt.shape[0]
    x = jnp.concatenate([x_prompt, x_sample], axis=0)
    y = _trunk(x, meta, rel_bias, g_ffn1, w_ffn1_gu, w_ffn1_down, g_mix, w_in, diff_lambda, g_subln, sinks, g_cq,
               g_ckv, w_uq, w_ukv, w_out, g_ffn2, w_ffn2_gu, w_ffn2_down, g_final)
    return (y[:n_prompt], y[n_prompt:])
```

```python
import functools
import math

import jax
import jax.numpy as jnp
from jax import lax
from jax.experimental import pallas as pl
from jax.experimental.pallas import tpu as pltpu

D_MODEL = 1024
DEPTH = 2
N_META = 16
BLOCK = 128
META_START = BLOCK - N_META
A_HEADS = 4
A_QK_DIM = 64
A_V_DIM = 2 * A_QK_DIM
B_HEADS = 4
B_KV_HEADS = 2
B_HEAD_DIM = 64
WINDOW = 128
C_HEADS = 4
C_Q_RANK = 256
C_KV_RANK = 128
C_NOPE_DIM = 64
C_ROPE_DIM = 32
C_V_DIM = 64
ROPE_THETA = 10000.0
N_BUCKETS = 32
MAX_DISTANCE = 128
D_FF = 2816
EPS = 1e-6

LANES = 128
TILE = 5 * BLOCK
FF_CHUNK = D_FF // 2
ROW_BLOCK = TILE // 2
PIPE_UNROLL = 4
VMEM_LIMIT = 56 * 1024 * 1024

F32 = jnp.float32
BF16 = jnp.bfloat16
NEG_INF = float("-inf")

_W_AQ, _W_AK, _W_AV, _W_BQ = 0, 512, 1024, 1536
_W_BK, _W_BV, _W_BVS = 2048, 2176, 2304
_W_CQ, _W_CKV, _W_KR, _W_KRR = 2432, 2688, 2816, 2944
_W_TOTAL = 3072


def _params(n_axes):
    return pltpu.CompilerParams(dimension_semantics=("arbitrary",) * n_axes, vmem_limit_bytes=VMEM_LIMIT)


def _resident(shape):
    return pl.BlockSpec(shape, lambda *_: (0,) * len(shape), pipeline_mode=pl.Buffered(1))


def _rms(x, g):
    return x * lax.rsqrt(jnp.mean(x * x, axis=-1, keepdims=True) + EPS) * g


def _dot(a, b):
    return jnp.dot(a, b, preferred_element_type=F32)


def _dot_nt(a, b):
    return lax.dot_general(a, b, (((1,), (1,)), ((), ())), preferred_element_type=F32)


def _ffn_kernel(x_ref, *refs, mixed, final_norm):
    refs = list(refs)
    x = x_ref[...]
    if mixed:
        oa_ref, ob_ref, oc_ref, wa_ref, wb_ref, wc_ref = refs[:6]
        del refs[:6]
        x = x + _dot(oa_ref[...], wa_ref[...]) + _dot(ob_ref[...], wb_ref[...]) + _dot(oc_ref[...], wc_ref[...])
    g_ref, wgu_ref, wd_ref = refs[:3]
    gf_ref = refs[3] if final_norm else None
    o_ref = refs[-1]
    xn = _rms(x, g_ref[...]).astype(BF16)
    acc = None
    for lo in range(0, D_FF, FF_CHUNK):
        g = _dot(xn, wgu_ref[:, lo:lo + FF_CHUNK])
        u = _dot(xn, wgu_ref[:, D_FF + lo:D_FF + lo + FF_CHUNK])
        a = (g * jax.nn.sigmoid(g) * u).astype(BF16)
        d = _dot(a, wd_ref[lo:lo + FF_CHUNK, :])
        acc = d if acc is None else acc + d
    y = x + 0.5 * acc
    if final_norm:
        y = _rms(y, gf_ref[...])
    o_ref[...] = y


def _ffn(h, g, wgu, wd, mix=None, g_final=None):
    nb, lp, d = h.shape

    def rows(n):
        return pl.BlockSpec((None, TILE, n), lambda b, t: (b, t, 0))

    tile = rows(d)
    in_specs = [tile]
    args = [h]
    if mix is not None:
        in_specs += [rows(o.shape[-1]) for o in mix[:3]] + [_resident(w.shape) for w in mix[3:]]
        args += list(mix)
    in_specs += [_resident((1, d)), _resident(wgu.shape), _resident(wd.shape)]
    args += [g.reshape(1, d), wgu, wd]
    if g_final is not None:
        in_specs.append(_resident((1, d)))
        args.append(g_final.reshape(1, d))
    return pl.pallas_call(
        functools.partial(_ffn_kernel, mixed=mix is not None, final_norm=g_final is not None),
        grid=(nb, lp // TILE),
        in_specs=in_specs,
        out_specs=tile,
        out_shape=jax.ShapeDtypeStruct(h.shape, F32),
        compiler_params=_params(2),
        name="ffn",
    )(*args)


def _inproj_kernel(h_ref, g_ref, w_ref, gcq_ref, gckv_ref, wuqm_ref, wuqr_ref, wukvk_ref, wukvv_ref,
                   cos_ref, sin_ref,
                   aq_ref, ak_ref, av_ref, bq_ref, bk_ref, bv_ref, bvs_ref, qc_ref, kc_ref, vc_ref, *, c_scale):
    xn = _rms(h_ref[...], g_ref[...]).astype(BF16)

    def proj(lo, n):
        return _dot(xn, w_ref[:, lo:lo + n])

    aq_ref[...] = proj(_W_AQ, 512).astype(BF16)
    ak_ref[...] = proj(_W_AK, 512).astype(BF16)
    av_ref[...] = proj(_W_AV, 512).astype(BF16)
    bq_ref[...] = proj(_W_BQ, 512).astype(BF16)
    bk_ref[...] = proj(_W_BK, 128).astype(BF16)
    bv_ref[...] = proj(_W_BV, 128).astype(BF16)
    bvs_ref[...] = proj(_W_BVS, 128).astype(BF16)

    cos = cos_ref[...]
    sin = sin_ref[...]
    cos4 = jnp.concatenate([cos] * C_HEADS, axis=1)
    sin4 = jnp.concatenate([sin] * C_HEADS, axis=1)
    cqn = _rms(proj(_W_CQ, C_Q_RANK), gcq_ref[...]).astype(BF16)
    q = _dot(cqn, wuqm_ref[...]) * cos4 + _dot(cqn, wuqr_ref[...]) * sin4
    qc_ref[...] = (q * c_scale).astype(BF16)

    ckvn = _rms(proj(_W_CKV, C_KV_RANK), gckv_ref[...]).astype(BF16)
    k_rope = proj(_W_KR, LANES) * cos + proj(_W_KRR, LANES) * sin
    kc_ref[...] = (_dot(ckvn, wukvk_ref[...]) + jnp.concatenate([k_rope] * C_HEADS, axis=1)).astype(BF16)
    vc_ref[...] = _dot(ckvn, wukvv_ref[...]).astype(BF16)


def _inproj(h, g, w_big, gcq, gckv, wuqm, wuqr, wukvk, wukvv, cos_t, sin_t):
    nb, lp, d = h.shape

    def tile(n):
        return pl.BlockSpec((None, TILE, n), lambda b, t: (b, t, 0))

    pos = pl.BlockSpec((TILE, LANES), lambda b, t: (t, 0))
    widths = (512, 512, 512, 512, 128, 128, 128, 512, 512, 256)
    return pl.pallas_call(
        functools.partial(_inproj_kernel, c_scale=(C_NOPE_DIM + C_ROPE_DIM) ** -0.5),
        grid=(nb, lp // TILE),
        in_specs=[tile(d), _resident((1, d)), _resident(w_big.shape), _resident((1, C_Q_RANK)),
                  _resident((1, C_KV_RANK)), _resident(wuqm.shape), _resident(wuqr.shape),
                  _resident(wukvk.shape), _resident(wukvv.shape), pos, pos],
        out_specs=[tile(n) for n in widths],
        out_shape=[jax.ShapeDtypeStruct((nb, lp, n), BF16) for n in widths],
        compiler_params=_params(2),
        name="inproj",
    )(h, g.reshape(1, d), w_big, gcq.reshape(1, -1), gckv.reshape(1, -1), wuqm, wuqr, wukvk, wukvv, cos_t, sin_t)


def _static(x):
    return isinstance(x, (int, bool))


def _aligned(x, m):
    return x if _static(x) else pl.multiple_of(x, m)


def _imin(a, b):
    return min(a, b) if _static(a) else jnp.minimum(a, b)


def _imax(a, b):
    return max(a, b) if _static(a) else jnp.maximum(a, b)


def _iselect(c, a, b):
    return (a if c else b) if _static(c) else jnp.where(c, a, b)


def _flash_scratch():
    return ([pltpu.VMEM((ROW_BLOCK, TILE), F32)] * 2 + [pltpu.VMEM((ROW_BLOCK, TILE), BF16)] * 2
            + [pltpu.VMEM((ROW_BLOCK, LANES), F32)] * 2
            + [pltpu.VMEM((2 * TILE, LANES), F32), pltpu.VMEM((2 * TILE, 2 * LANES), F32)])


def _flash_pipeline(n_tiles, scores, values, stat_row, scratch):
    s0, s1, p0, p1, a0, a1, m_ref, acc_ref = scratch
    s_bufs, p_bufs, a_bufs = (s0, s1), (p0, p1), (a0, a1)
    unroll = max(k for k in range(2, PIPE_UNROLL + 1, 2) if n_tiles % k == 0)
    m_ref[...] = jnp.full(m_ref.shape, NEG_INF, F32)
    acc_ref[...] = jnp.zeros(acc_ref.shape, F32)
    p1[...] = jnp.zeros(p1.shape, BF16)
    a1[...] = jnp.ones(a1.shape, F32)

    def issue(t, par):
        s_bufs[par][...] = scores(t, par)

    def accumulate(t, par):
        rows = pl.ds(stat_row(t, par), ROW_BLOCK)
        alpha = a_bufs[par][...]
        acc_ref[rows, :] = (jnp.concatenate([alpha, alpha], axis=1) * acc_ref[rows, :]
                            + _dot(p_bufs[par][...], values(t, par)))

    def step(t, par):
        issue(_imin(t + 1, n_tiles - 1), 1 - par)
        rows = pl.ds(stat_row(t, par), ROW_BLOCK)
        m_old = m_ref[rows, :]
        m_new = jnp.maximum(m_old, jnp.max(s_bufs[par][...], axis=-1, keepdims=True))
        p_bufs[par][...] = jnp.exp((s_bufs[par][...] - jnp.concatenate([m_new] * (TILE // LANES), axis=1)
                                    ).astype(BF16))
        a_bufs[par][...] = jnp.exp(m_old - m_new)
        m_ref[rows, :] = m_new
        accumulate(_imax(t - 1, 0), 1 - par)

    issue(0, 0)

    def body(u, carry):
        for k in range(unroll):
            step(unroll * u + k, k % 2)
        return carry

    if unroll == n_tiles:
        for k in range(n_tiles):
            step(k, k % 2)
    else:
        lax.fori_loop(0, n_tiles // unroll, body, 0)
    accumulate(n_tiles - 1, 1)


(_TAB_LEFT, _TAB_DIAG, _TAB_RIGHT, _TAB_FAR_LEFT, _TAB_FAR_RIGHT,
 _TAB_LEFT_LEAD, _TAB_DIAG_LEAD, _TAB_FAR_LEAD) = range(8)


def _ones_right(v):
    return jnp.concatenate([v, jnp.ones(v.shape, v.dtype)], axis=1)


def _attn_a_kernel(q_ref, k_ref, v_ref, tab_ref, dl_ref, gs_ref, o_ref, qq_ref, *scratch, lam_init, n_chunks):
    i = pl.program_id(2)
    q = q_ref[...]
    lane = lax.broadcasted_iota(jnp.int32, q.shape, 1)
    zero = jnp.zeros_like(q)
    qq_ref[0:TILE, :] = jnp.where(lane < A_QK_DIM, q, zero)
    qq_ref[TILE:2 * TILE, :] = jnp.where(lane >= A_QK_DIM, q, zero)
    blocks = 2 * TILE // ROW_BLOCK

    def q_row(t):
        return _aligned((t % blocks) * ROW_BLOCK, ROW_BLOCK)

    def k_row(t):
        return _aligned((t // blocks) * TILE, TILE)

    def scores(t, par):
        j = t // blocks
        lead = jnp.where(i == 0, _TAB_DIAG_LEAD, jnp.where(i == 1, _TAB_LEFT_LEAD, _TAB_FAR_LEAD))
        rest = jnp.where(j < i - 1, _TAB_FAR_LEFT, jnp.where(j > i + 1, _TAB_FAR_RIGHT, j - i + 1))
        table = jnp.where(j == 0, lead, rest)
        row = _aligned((t % (TILE // ROW_BLOCK)) * ROW_BLOCK, ROW_BLOCK)
        return (_dot_nt(qq_ref[pl.ds(q_row(t), ROW_BLOCK), :], k_ref[pl.ds(k_row(t), TILE), :])
                + tab_ref[table, pl.ds(row, ROW_BLOCK), :])

    def values(t, par):
        return _ones_right(v_ref[pl.ds(k_row(t), TILE), :])

    _flash_pipeline(n_chunks * blocks, scores, values, lambda t, par: q_row(t), scratch)

    acc_ref = scratch[-1]
    o = acc_ref[:, 0:A_V_DIM] / acc_ref[:, A_V_DIM:2 * A_V_DIM]
    dl = dl_ref[...]
    lam = (jnp.exp(jnp.sum(dl[0:1] * dl[1:2], axis=-1, keepdims=True))
           - jnp.exp(jnp.sum(dl[2:3] * dl[3:4], axis=-1, keepdims=True)) + lam_init)
    w = o[0:TILE] - lam * o[TILE:2 * TILE]
    o_ref[...] = (_rms(w, gs_ref[...]) * (1.0 - lam_init)).astype(BF16)


def _attn_a(aq, ak, av, tab, dl, gs, lam_init):
    nb, lp, _ = aq.shape
    n_chunks = lp // TILE
    qspec = pl.BlockSpec((None, TILE, LANES), lambda b, h, t: (b, t, h))
    kvspec = pl.BlockSpec((None, lp, LANES), lambda b, h, t: (b, 0, h))
    return pl.pallas_call(
        functools.partial(_attn_a_kernel, lam_init=lam_init, n_chunks=n_chunks),
        grid=(nb, A_HEADS, n_chunks),
        in_specs=[qspec, kvspec, kvspec,
                  pl.BlockSpec((None,) + tab.shape[1:], lambda b, h, t: (h, 0, 0, 0)),
                  _resident((4, A_QK_DIM)), _resident((1, A_V_DIM))],
        out_specs=qspec,
        out_shape=jax.ShapeDtypeStruct((nb, lp, A_HEADS * A_V_DIM), BF16),
        scratch_shapes=[pltpu.VMEM((2 * TILE, LANES), BF16)] + _flash_scratch(),
        compiler_params=_params(3),
        name="attn_a",
    )(aq, ak, av, tab, dl, gs.reshape(1, A_V_DIM))


def _attn_c_kernel(q_ref, k_ref, v_ref, kmask_ref, o_ref, *scratch, n_chunks):
    blocks = 2 * TILE // ROW_BLOCK

    def q_row(t):
        return _aligned(((t % blocks) // 2) * ROW_BLOCK, ROW_BLOCK)

    def k_row(t):
        return _aligned((t // blocks) * TILE, TILE)

    def scores(t, par):
        cols = slice(par * LANES, (par + 1) * LANES)
        return (_dot_nt(q_ref[pl.ds(q_row(t), ROW_BLOCK), cols], k_ref[pl.ds(k_row(t), TILE), cols])
                + kmask_ref[_iselect(t < blocks, 1, 0)])

    def values(t, par):
        return _ones_right(v_ref[pl.ds(k_row(t), TILE), :])

    def stat_row(t, par):
        return _aligned(par * TILE + q_row(t), ROW_BLOCK)

    _flash_pipeline(n_chunks * blocks, scores, values, stat_row, scratch)

    acc_ref = scratch[-1]
    o = acc_ref[:, 0:LANES] / acc_ref[:, LANES:2 * LANES]
    lane = lax.broadcasted_iota(jnp.int32, (TILE, LANES), 1)
    o_ref[...] = jnp.where(lane < C_V_DIM, o[0:TILE], o[TILE:2 * TILE]).astype(BF16)


def _attn_c(qc, kc, vc, kmask):
    nb, lp, _ = qc.shape
    n_chunks = lp // TILE
    return pl.pallas_call(
        functools.partial(_attn_c_kernel, n_chunks=n_chunks),
        grid=(nb, C_HEADS // 2, n_chunks),
        in_specs=[pl.BlockSpec((None, TILE, 2 * LANES), lambda b, p, t: (b, t, p)),
                  pl.BlockSpec((None, lp, 2 * LANES), lambda b, p, t: (b, 0, p)),
                  pl.BlockSpec((None, lp, LANES), lambda b, p, t: (b, 0, p)),
                  _resident(kmask.shape)],
        out_specs=pl.BlockSpec((None, TILE, LANES), lambda b, p, t: (b, t, p)),
        out_shape=jax.ShapeDtypeStruct((nb, lp, C_HEADS * C_V_DIM), BF16),
        scratch_shapes=_flash_scratch(),
        compiler_params=_params(3),
        name="attn_c",
    )(qc, kc, vc, kmask)


_B_ORDER = (0, 3, 1, 2)


def _attn_b_kernel(sink_ref, q_ref, k_ref, v_ref, vs_ref, tab_ref, o_ref, *, lp):
    t = pl.program_id(1)
    lane = lax.broadcasted_iota(jnp.int32, (BLOCK, LANES), 1)
    kk = lax.broadcasted_iota(jnp.int32, (1, 3 * BLOCK), 1)
    row = lax.broadcasted_iota(jnp.int32, (B_HEADS * BLOCK, 1), 0)
    sink = jnp.full((B_HEADS * BLOCK, 1), sink_ref[_B_ORDER[0]], F32)
    for n in range(1, B_HEADS):
        sink = jnp.where(row >= n * BLOCK, sink_ref[_B_ORDER[n]], sink)
    half = B_HEADS * BLOCK // 2
    for blk in range(TILE // BLOCK):
        gblk = t * (TILE // BLOCK) + blk
        row0 = pl.multiple_of(gblk * BLOCK, BLOCK)
        kw = k_ref[pl.ds(row0, 3 * BLOCK), :]
        vw = v_ref[pl.ds(row0, 3 * BLOCK), :]
        vsw = vs_ref[pl.ds(row0, 3 * BLOCK), :]
        kslot = (gblk - 1) * BLOCK + kk
        row_mask = jnp.where((kslot >= META_START) & (kslot < lp), 0.0, NEG_INF)
        rows = slice(blk * BLOCK, (blk + 1) * BLOCK)
        qs = jnp.concatenate([q_ref[rows, h * LANES:(h + 1) * LANES] for h in _B_ORDER], axis=0)
        s = _dot_nt(qs, kw) + tab_ref[...] + row_mask
        m = jnp.maximum(jnp.max(s, axis=-1, keepdims=True), sink)
        p = jnp.exp(s - m)
        denom = jnp.sum(p, axis=-1, keepdims=True) + jnp.exp(sink - m)
        pb = p.astype(BF16)
        o03 = _dot(pb[0:half], vw) / denom[0:half]
        o12 = _dot(pb[half:], vsw) / denom[half:]
        o_ref[rows, 0:LANES] = jnp.where(lane < B_HEAD_DIM, o03[0:BLOCK], o12[0:BLOCK]).astype(BF16)
        o_ref[rows, LANES:2 * LANES] = jnp.where(lane < B_HEAD_DIM, o12[BLOCK:], o03[BLOCK:]).astype(BF16)


def _attn_b(bq, bk_pad, bv_pad, bvs_pad, tab, sinks):
    nb, lp, _ = bq.shape
    kvspec = pl.BlockSpec((None, lp + 2 * BLOCK, LANES), lambda b, t: (b, 0, 0))
    return pl.pallas_call(
        functools.partial(_attn_b_kernel, lp=lp),
        grid=(nb, lp // TILE),
        in_specs=[pl.BlockSpec(memory_space=pltpu.SMEM),
                  pl.BlockSpec((None, TILE, B_HEADS * LANES), lambda b, t: (b, t, 0)),
                  kvspec, kvspec, kvspec, _resident(tab.shape)],
        out_specs=pl.BlockSpec((None, TILE, B_HEADS * B_HEAD_DIM), lambda b, t: (b, t, 0)),
        out_shape=jax.ShapeDtypeStruct((nb, lp, B_HEADS * B_HEAD_DIM), BF16),
        compiler_params=_params(2),
        name="attn_b",
    )(sinks, bq, bk_pad, bv_pad, bvs_pad, tab)


def _t5_bucket(rel):
    half = N_BUCKETS // 2
    max_exact = half // 2
    ret = jnp.where(rel > 0, half, 0)
    n = jnp.abs(rel)
    nf = jnp.maximum(n, 1).astype(F32)
    large = max_exact + (jnp.log(nf / max_exact) / math.log(MAX_DISTANCE / max_exact)
                         * (half - max_exact)).astype(jnp.int32)
    large = jnp.minimum(large, half - 1)
    return ret + jnp.where(n < max_exact, n, large)


def _rot_cols(w):
    half = w.shape[-1] // 2
    return jnp.concatenate([-w[..., half:], w[..., :half]], axis=-1)


def _bias_of_rel(table, rel):
    bucket = _t5_bucket(rel)[None]
    out = jnp.zeros((table.shape[1],) + rel.shape, F32)
    for b in range(N_BUCKETS):
        out = jnp.where(bucket == b, table[b].reshape((-1,) + (1,) * rel.ndim), out)
    return out


def _bias_tables(rel_bias, lp):
    rb = rel_bias.astype(F32)
    rela = (jnp.arange(3 * TILE) - TILE)[None, :] - jnp.arange(TILE)[:, None]
    wide = _bias_of_rel(rb[:, :A_HEADS], rela)
    left, diag, right = (wide[:, :, d * TILE:(d + 1) * TILE] for d in range(3))
    far_a = rb[:, :A_HEADS][_t5_bucket(jnp.array([-2 * TILE, 2 * TILE]))].T
    lead = jnp.where(jnp.arange(TILE) >= META_START, 0.0, NEG_INF).astype(F32).reshape(1, 1, TILE)
    far_left = jnp.broadcast_to(far_a[:, 0][:, None, None], left.shape)
    far_right = jnp.broadcast_to(far_a[:, 1][:, None, None], left.shape)
    tab_a = jnp.stack([left, diag, right, far_left, far_right, left + lead, diag + lead, far_left + lead], axis=1)
    kmask = jnp.concatenate([jnp.zeros_like(lead), lead], axis=0)
    relb = (jnp.arange(3 * BLOCK) - BLOCK)[None, :] - jnp.arange(BLOCK)[:, None]
    tab_b = jnp.where((jnp.abs(relb) <= WINDOW)[None], _bias_of_rel(rb[:, A_HEADS:], relb), NEG_INF)
    tab_b = jnp.concatenate([tab_b[h] for h in _B_ORDER], axis=0)
    slot = jnp.arange(lp)
    inv = ROPE_THETA ** (-jnp.arange(0, C_ROPE_DIM, 2, dtype=F32) / C_ROPE_DIM)
    ang = (slot - META_START).astype(F32)[:, None] * inv[None, :]
    ang = jnp.concatenate([ang, ang], axis=-1)
    ones = jnp.ones((lp, C_NOPE_DIM), F32)
    zeros_n = jnp.zeros((lp, C_NOPE_DIM), F32)
    zeros_p = jnp.zeros((lp, LANES - C_NOPE_DIM - C_ROPE_DIM), F32)
    cos_t = jnp.concatenate([ones, jnp.cos(ang), zeros_p], axis=-1)
    sin_t = jnp.concatenate([zeros_n, jnp.sin(ang), zeros_p], axis=-1)
    return tab_a, tab_b, kmask, cos_t, sin_t


def _layer_weights(w_in, w_uq, w_ukv, w_out):
    d = w_in.shape[0]
    sizes = (512, 512, 512, 256, 128, 128, 256, 128, 32)
    offs = [0]
    for s in sizes:
        offs.append(offs[-1] + s)
    aq, ak, av, bq, bk, bv, cq, ckv, kr = (w_in[:, offs[i]:offs[i + 1]] for i in range(9))
    zeros64 = jnp.zeros((d, B_HEAD_DIM), F32)
    bq_pad = []
    for h in range(B_HEADS):
        qh = bq[:, h * B_HEAD_DIM:(h + 1) * B_HEAD_DIM] * (B_HEAD_DIM ** -0.5)
        bq_pad += [qh, zeros64] if h // (B_HEADS // B_KV_HEADS) == 0 else [zeros64, qh]
    bvs = jnp.concatenate([bv[:, B_HEAD_DIM:], bv[:, :B_HEAD_DIM]], axis=1)
    pad_lo = jnp.zeros((d, C_NOPE_DIM), F32)
    pad_hi = jnp.zeros((d, LANES - C_NOPE_DIM - C_ROPE_DIM), F32)
    w_big = jnp.concatenate(
        [aq * (A_QK_DIM ** -0.5), ak, av] + bq_pad + [bk, bv, bvs, cq, ckv,
                                                      pad_lo, kr, pad_hi, pad_lo, _rot_cols(kr), pad_hi],
        axis=1).astype(BF16)
    uq = w_uq.reshape(C_Q_RANK, C_HEADS, C_NOPE_DIM + C_ROPE_DIM)
    uq_nope, uq_rope = uq[..., :C_NOPE_DIM], uq[..., C_NOPE_DIM:]
    zq = jnp.zeros((C_Q_RANK, C_HEADS, LANES - C_NOPE_DIM - C_ROPE_DIM), F32)
    wuqm = jnp.concatenate([uq_nope, uq_rope, zq], axis=-1).reshape(C_Q_RANK, C_HEADS * LANES).astype(BF16)
    wuqr = jnp.concatenate([jnp.zeros_like(uq_nope), _rot_cols(uq_rope), zq], axis=-1
                           ).reshape(C_Q_RANK, C_HEADS * LANES).astype(BF16)
    ukv = w_ukv.reshape(C_KV_RANK, C_HEADS, C_NOPE_DIM + C_V_DIM)
    wukvk = jnp.concatenate([ukv[..., :C_NOPE_DIM], jnp.zeros((C_KV_RANK, C_HEADS, LANES - C_NOPE_DIM), F32)],
                            axis=-1).reshape(C_KV_RANK, C_HEADS * LANES).astype(BF16)
    wukvv = ukv[..., C_NOPE_DIM:].reshape(C_KV_RANK, C_HEADS * C_V_DIM).astype(BF16)
    n_a = A_HEADS * A_V_DIM
    n_b = B_HEADS * B_HEAD_DIM
    wo = w_out.astype(BF16)
    return w_big, wuqm, wuqr, wukvk, wukvv, wo[:n_a], wo[n_a:n_a + n_b], wo[n_a + n_b:]


def _trunk(x, meta, rel_bias, g_ffn1, w_ffn1_gu, w_ffn1_down, g_mix, w_in, diff_lambda, g_subln, sinks, g_cq,
           g_ckv, w_uq, w_ukv, w_out, g_ffn2, w_ffn2_gu, w_ffn2_down, g_final):
    nb, seq, d = x.shape
    lp = BLOCK + seq
    assert lp % TILE == 0 and d == D_MODEL
    lead = jnp.concatenate([jnp.zeros((META_START, d), x.dtype), meta.astype(x.dtype)], axis=0)
    h = jnp.concatenate([jnp.broadcast_to(lead[None], (nb, BLOCK, d)), x], axis=1)
    tab_a, tab_b, kmask, cos_t, sin_t = _bias_tables(rel_bias, lp)
    pad = ((0, 0), (BLOCK, BLOCK), (0, 0))
    for l in range(DEPTH):
        w_big, wuqm, wuqr, wukvk, wukvv, wo_a, wo_b, wo_c = _layer_weights(w_in[l], w_uq[l], w_ukv[l], w_out[l])
        h = _ffn(h, g_ffn1[l], w_ffn1_gu[l].astype(BF16), w_ffn1_down[l].astype(BF16))
        aq, ak, av, bq, bk, bv, bvs, qc, kc, vc = _inproj(h, g_mix[l], w_big, g_cq[l], g_ckv[l], wuqm, wuqr,
                                                           wukvk, wukvv, cos_t, sin_t)
        lam_init = 0.8 - 0.6 * math.exp(-0.3 * l)
        oa = _attn_a(aq, ak, av, tab_a, diff_lambda[l].astype(F32), g_subln[l], lam_init)
        ob = _attn_b(bq, jnp.pad(bk, pad), jnp.pad(bv, pad), jnp.pad(bvs, pad), tab_b, sinks[l].astype(F32))
        oc = _attn_c(qc, kc, vc, kmask)
        h = _ffn(h, g_ffn2[l], w_ffn2_gu[l].astype(BF16), w_ffn2_down[l].astype(BF16),
                 mix=(oa, ob, oc, wo_a, wo_b, wo_c), g_final=g_final if l == DEPTH - 1 else None)
    return h[:, BLOCK:]


def kernel(x_prompt, x_sample, meta, rel_bias, g_ffn1, w_ffn1_gu, w_ffn1_down, g_mix, w_in, diff_lambda, g_subln,
           sinks, g_cq, g_ckv, w_uq, w_ukv, w_out, g_ffn2, w_ffn2_gu, w_ffn2_down, g_final):
    n_prompt = x_prompt.shape[0]
    x = jnp.concatenate([x_prompt, x_sample], axis=0)
    y = _trunk(x, meta, rel_bias, g_ffn1, w_ffn1_gu, w_ffn1_down, g_mix, w_in, diff_lambda, g_subln, sinks, g_cq,
               g_ckv, w_uq, w_ukv, w_out, g_ffn2, w_ffn2_gu, w_ffn2_down, g_final)
    return (y[:n_prompt], y[n_prompt:])
```

```python
import functools
import math

import jax
import jax.numpy as jnp
from jax import lax
from jax.experimental import pallas as pl
from jax.experimental.pallas import tpu as pltpu

D_MODEL = 1024
DEPTH = 2
N_META = 16
BLOCK = 128
META_START = BLOCK - N_META
A_HEADS = 4
A_QK_DIM = 64
A_V_DIM = 2 * A_QK_DIM
B_HEADS = 4
B_KV_HEADS = 2
B_HEAD_DIM = 64
WINDOW = 128
C_HEADS = 4
C_Q_RANK = 256
C_KV_RANK = 128
C_NOPE_DIM = 64
C_ROPE_DIM = 32
C_V_DIM = 64
ROPE_THETA = 10000.0
N_BUCKETS = 32
MAX_DISTANCE = 128
D_FF = 2816
EPS = 1e-6

LANES = 128
TILE = 5 * BLOCK
FF_CHUNK = D_FF // 2
ROW_BLOCK = TILE // 2
PIPE_UNROLL = 4
VMEM_LIMIT = 56 * 1024 * 1024

F32 = jnp.float32
BF16 = jnp.bfloat16
NEG_INF = float("-inf")

_W_AQ, _W_AK, _W_AV, _W_BQ = 0, 512, 1024, 1536
_W_BK, _W_BV, _W_BVS = 2048, 2176, 2304
_W_CQ, _W_CKV, _W_KR, _W_KRR = 2432, 2688, 2816, 2944
_W_TOTAL = 3072


def _params(n_axes):
    return pltpu.CompilerParams(dimension_semantics=("arbitrary",) * n_axes, vmem_limit_bytes=VMEM_LIMIT)


def _resident(shape):
    return pl.BlockSpec(shape, lambda *_: (0,) * len(shape), pipeline_mode=pl.Buffered(1))


def _rms(x, g):
    return x * lax.rsqrt(jnp.mean(x * x, axis=-1, keepdims=True) + EPS) * g


def _dot(a, b):
    return jnp.dot(a, b, preferred_element_type=F32)


def _dot_nt(a, b):
    return lax.dot_general(a, b, (((1,), (1,)), ((), ())), preferred_element_type=F32)


def _ffn_kernel(x_ref, *refs, mixed, final_norm):
    refs = list(refs)
    x = x_ref[...]
    if mixed:
        oa_ref, ob_ref, oc_ref, wa_ref, wb_ref, wc_ref = refs[:6]
        del refs[:6]
        x = x + _dot(oa_ref[...], wa_ref[...]) + _dot(ob_ref[...], wb_ref[...]) + _dot(oc_ref[...], wc_ref[...])
    g_ref, wgu_ref, wd_ref = refs[:3]
    gf_ref = refs[3] if final_norm else None
    o_ref = refs[-1]
    xn = _rms(x, g_ref[...]).astype(BF16)
    acc = None
    for lo in range(0, D_FF, FF_CHUNK):
        g = _dot(xn, wgu_ref[:, lo:lo + FF_CHUNK])
        u = _dot(xn, wgu_ref[:, D_FF + lo:D_FF + lo + FF_CHUNK])
        a = (g * jax.nn.sigmoid(g) * u).astype(BF16)
        d = _dot(a, wd_ref[lo:lo + FF_CHUNK, :])
        acc = d if acc is None else acc + d
    y = x + 0.5 * acc
    if final_norm:
        y = _rms(y, gf_ref[...])
    o_ref[...] = y


def _ffn(h, g, wgu, wd, mix=None, g_final=None):
    nb, lp, d = h.shape

    def rows(n):
        return pl.BlockSpec((None, TILE, n), lambda b, t: (b, t, 0))

    tile = rows(d)
    in_specs = [tile]
    args = [h]
    if mix is not None:
        in_specs += [rows(o.shape[-1]) for o in mix[:3]] + [_resident(w.shape) for w in mix[3:]]
        args += list(mix)
    in_specs += [_resident((1, d)), _resident(wgu.shape), _resident(wd.shape)]
    args += [g.reshape(1, d), wgu, wd]
    if g_final is not None:
        in_specs.append(_resident((1, d)))
        args.append(g_final.reshape(1, d))
    return pl.pallas_call(
        functools.partial(_ffn_kernel, mixed=mix is not None, final_norm=g_final is not None),
        grid=(nb, lp // TILE),
        in_specs=in_specs,
        out_specs=tile,
        out_shape=jax.ShapeDtypeStruct(h.shape, F32),
        compiler_params=_params(2),
        name="ffn",
    )(*args)


def _inproj_kernel(h_ref, g_ref, w_ref, gcq_ref, gckv_ref, wuqm_ref, wuqr_ref, wukvk_ref, wukvv_ref,
                   cos_ref, sin_ref,
                   aq_ref, ak_ref, av_ref, bq_ref, bk_ref, bv_ref, bvs_ref, qc_ref, kc_ref, vc_ref, *, c_scale):
    xn = _rms(h_ref[...], g_ref[...]).astype(BF16)

    def proj(lo, n):
        return _dot(xn, w_ref[:, lo:lo + n])

    aq_ref[...] = proj(_W_AQ, 512).astype(BF16)
    ak_ref[...] = proj(_W_AK, 512).astype(BF16)
    av_ref[...] = proj(_W_AV, 512).astype(BF16)
    bq_ref[...] = proj(_W_BQ, 512).astype(BF16)
    bk_ref[...] = proj(_W_BK, 128).astype(BF16)
    bv_ref[...] = proj(_W_BV, 128).astype(BF16)
    bvs_ref[...] = proj(_W_BVS, 128).astype(BF16)

    cos = cos_ref[...]
    sin = sin_ref[...]
    cos4 = jnp.concatenate([cos] * C_HEADS, axis=1)
    sin4 = jnp.concatenate([sin] * C_HEADS, axis=1)
    cqn = _rms(proj(_W_CQ, C_Q_RANK), gcq_ref[...]).astype(BF16)
    q = _dot(cqn, wuqm_ref[...]) * cos4 + _dot(cqn, wuqr_ref[...]) * sin4
    qc_ref[...] = (q * c_scale).astype(BF16)

    ckvn = _rms(proj(_W_CKV, C_KV_RANK), gckv_ref[...]).astype(BF16)
    k_rope = proj(_W_KR, LANES) * cos + proj(_W_KRR, LANES) * sin
    kc_ref[...] = (_dot(ckvn, wukvk_ref[...]) + jnp.concatenate([k_rope] * C_HEADS, axis=1)).astype(BF16)
    vc_ref[...] = _dot(ckvn, wukvv_ref[...]).astype(BF16)


def _inproj(h, g, w_big, gcq, gckv, wuqm, wuqr, wukvk, wukvv, cos_t, sin_t):
    nb, lp, d = h.shape

    def tile(n):
        return pl.BlockSpec((None, TILE, n), lambda b, t: (b, t, 0))

    pos = pl.BlockSpec((TILE, LANES), lambda b, t: (t, 0))
    widths = (512, 512, 512, 512, 128, 128, 128, 512, 512, 256)
    return pl.pallas_call(
        functools.partial(_inproj_kernel, c_scale=(C_NOPE_DIM + C_ROPE_DIM) ** -0.5),
        grid=(nb, lp // TILE),
        in_specs=[tile(d), _resident((1, d)), _resident(w_big.shape), _resident((1, C_Q_RANK)),
                  _resident((1, C_KV_RANK)), _resident(wuqm.shape), _resident(wuqr.shape),
                  _resident(wukvk.shape), _resident(wukvv.shape), pos, pos],
        out_specs=[tile(n) for n in widths],
        out_shape=[jax.ShapeDtypeStruct((nb, lp, n), BF16) for n in widths],
        compiler_params=_params(2),
        name="inproj",
    )(h, g.reshape(1, d), w_big, gcq.reshape(1, -1), gckv.reshape(1, -1), wuqm, wuqr, wukvk, wukvv, cos_t, sin_t)


def _static(x):
    return isinstance(x, (int, bool))


def _aligned(x, m):
    return x if _static(x) else pl.multiple_of(x, m)


def _imin(a, b):
    return min(a, b) if _static(a) else jnp.minimum(a, b)


def _imax(a, b):
    return max(a, b) if _static(a) else jnp.maximum(a, b)


def _iselect(c, a, b):
    return (a if c else b) if _static(c) else jnp.where(c, a, b)


def _flash_scratch():
    return ([pltpu.VMEM((ROW_BLOCK, TILE), F32)] * 2 + [pltpu.VMEM((ROW_BLOCK, LANES), F32)] * 2
            + [pltpu.VMEM((ROW_BLOCK, TILE), BF16)] * 2 + [pltpu.VMEM((ROW_BLOCK, LANES), F32)] * 2
            + [pltpu.VMEM((2 * TILE, LANES), F32), pltpu.VMEM((2 * TILE, 2 * LANES), F32)])


def _flash_pipeline(n_tiles, scores, values, stat_row, scratch):
    s0, s1, x0, x1, p0, p1, a0, a1, m_ref, acc_ref = scratch
    s_bufs, x_bufs, p_bufs, a_bufs = (s0, s1), (x0, x1), (p0, p1), (a0, a1)
    unroll = max(k for k in range(2, PIPE_UNROLL + 1, 2) if n_tiles % k == 0)
    m_ref[...] = jnp.full(m_ref.shape, NEG_INF, F32)
    acc_ref[...] = jnp.zeros(acc_ref.shape, F32)
    p1[...] = jnp.zeros(p1.shape, BF16)
    a1[...] = jnp.ones(a1.shape, F32)

    def issue(t, par):
        s = scores(t, par)
        s_bufs[par][...] = s
        x_bufs[par][...] = jnp.broadcast_to(jnp.max(s, axis=-1, keepdims=True), (ROW_BLOCK, LANES))

    def accumulate(t, par):
        rows = pl.ds(stat_row(t, par), ROW_BLOCK)
        alpha = a_bufs[par][...]
        acc_ref[rows, :] = (jnp.concatenate([alpha, alpha], axis=1) * acc_ref[rows, :]
                            + _dot(p_bufs[par][...], values(t, par)))

    def step(t, par):
        issue(_imin(t + 1, n_tiles - 1), 1 - par)
        rows = pl.ds(stat_row(t, par), ROW_BLOCK)
        m_old = m_ref[rows, :]
        m_new = jnp.maximum(m_old, x_bufs[par][...])
        p_bufs[par][...] = jnp.exp((s_bufs[par][...] - jnp.concatenate([m_new] * (TILE // LANES), axis=1)
                                    ).astype(BF16))
        a_bufs[par][...] = jnp.exp(m_old - m_new)
        m_ref[rows, :] = m_new
        accumulate(_imax(t - 1, 0), 1 - par)

    issue(0, 0)

    def body(u, carry):
        for k in range(unroll):
            step(unroll * u + k, k % 2)
        return carry

    if unroll == n_tiles:
        for k in range(n_tiles):
            step(k, k % 2)
    else:
        lax.fori_loop(0, n_tiles // unroll, body, 0)
    accumulate(n_tiles - 1, 1)


(_TAB_LEFT, _TAB_DIAG, _TAB_RIGHT, _TAB_FAR_LEFT, _TAB_FAR_RIGHT,
 _TAB_LEFT_LEAD, _TAB_DIAG_LEAD, _TAB_FAR_LEAD) = range(8)


def _ones_right(v):
    return jnp.concatenate([v, jnp.ones(v.shape, v.dtype)], axis=1)


def _attn_a_kernel(q_ref, k_ref, v_ref, tab_ref, dl_ref, gs_ref, o_ref, qq_ref, *scratch, lam_init, n_chunks):
    i = pl.program_id(2)
    q = q_ref[...]
    lane = lax.broadcasted_iota(jnp.int32, q.shape, 1)
    zero = jnp.zeros_like(q)
    qq_ref[0:TILE, :] = jnp.where(lane < A_QK_DIM, q, zero)
    qq_ref[TILE:2 * TILE, :] = jnp.where(lane >= A_QK_DIM, q, zero)
    blocks = 2 * TILE // ROW_BLOCK

    def q_row(t):
        return _aligned((t % blocks) * ROW_BLOCK, ROW_BLOCK)

    def k_row(t):
        return _aligned((t // blocks) * TILE, TILE)

    def scores(t, par):
        j = t // blocks
        lead = jnp.where(i == 0, _TAB_DIAG_LEAD, jnp.where(i == 1, _TAB_LEFT_LEAD, _TAB_FAR_LEAD))
        rest = jnp.where(j < i - 1, _TAB_FAR_LEFT, jnp.where(j > i + 1, _TAB_FAR_RIGHT, j - i + 1))
        table = jnp.where(j == 0, lead, rest)
        row = _aligned((t % (TILE // ROW_BLOCK)) * ROW_BLOCK, ROW_BLOCK)
        return (_dot_nt(qq_ref[pl.ds(q_row(t), ROW_BLOCK), :], k_ref[pl.ds(k_row(t), TILE), :])
                + tab_ref[table, pl.ds(row, ROW_BLOCK), :])

    def values(t, par):
        return _ones_right(v_ref[pl.ds(k_row(t), TILE), :])

    _flash_pipeline(n_chunks * blocks, scores, values, lambda t, par: q_row(t), scratch)

    acc_ref = scratch[-1]
    o = acc_ref[:, 0:A_V_DIM] / acc_ref[:, A_V_DIM:2 * A_V_DIM]
    dl = dl_ref[...]
    lam = (jnp.exp(jnp.sum(dl[0:1] * dl[1:2], axis=-1, keepdims=True))
           - jnp.exp(jnp.sum(dl[2:3] * dl[3:4], axis=-1, keepdims=True)) + lam_init)
    w = o[0:TILE] - lam * o[TILE:2 * TILE]
    o_ref[...] = (_rms(w, gs_ref[...]) * (1.0 - lam_init)).astype(BF16)


def _attn_a(aq, ak, av, tab, dl, gs, lam_init):
    nb, lp, _ = aq.shape
    n_chunks = lp // TILE
    qspec = pl.BlockSpec((None, TILE, LANES), lambda b, h, t: (b, t, h))
    kvspec = pl.BlockSpec((None, lp, LANES), lambda b, h, t: (b, 0, h))
    return pl.pallas_call(
        functools.partial(_attn_a_kernel, lam_init=lam_init, n_chunks=n_chunks),
        grid=(nb, A_HEADS, n_chunks),
        in_specs=[qspec, kvspec, kvspec,
                  pl.BlockSpec((None,) + tab.shape[1:], lambda b, h, t: (h, 0, 0, 0)),
                  _resident((4, A_QK_DIM)), _resident((1, A_V_DIM))],
        out_specs=qspec,
        out_shape=jax.ShapeDtypeStruct((nb, lp, A_HEADS * A_V_DIM), BF16),
        scratch_shapes=[pltpu.VMEM((2 * TILE, LANES), BF16)] + _flash_scratch(),
        compiler_params=_params(3),
        name="attn_a",
    )(aq, ak, av, tab, dl, gs.reshape(1, A_V_DIM))


def _attn_c_kernel(q_ref, k_ref, v_ref, kmask_ref, o_ref, *scratch, n_chunks):
    blocks = 2 * TILE // ROW_BLOCK

    def q_row(t):
        return _aligned(((t % blocks) // 2) * ROW_BLOCK, ROW_BLOCK)

    def k_row(t):
        return _aligned((t // blocks) * TILE, TILE)

    def scores(t, par):
        cols = slice(par * LANES, (par + 1) * LANES)
        return (_dot_nt(q_ref[pl.ds(q_row(t), ROW_BLOCK), cols], k_ref[pl.ds(k_row(t), TILE), cols])
                + kmask_ref[_iselect(t < blocks, 1, 0)])

    def values(t, par):
        return _ones_right(v_ref[pl.ds(k_row(t), TILE), :])

    def stat_row(t, par):
        return _aligned(par * TILE + q_row(t), ROW_BLOCK)

    _flash_pipeline(n_chunks * blocks, scores, values, stat_row, scratch)

    acc_ref = scratch[-1]
    o = acc_ref[:, 0:LANES] / acc_ref[:, LANES:2 * LANES]
    lane = lax.broadcasted_iota(jnp.int32, (TILE, LANES), 1)
    o_ref[...] = jnp.where(lane < C_V_DIM, o[0:TILE], o[TILE:2 * TILE]).astype(BF16)


def _attn_c(qc, kc, vc, kmask):
    nb, lp, _ = qc.shape
    n_chunks = lp // TILE
    return pl.pallas_call(
        functools.partial(_attn_c_kernel, n_chunks=n_chunks),
        grid=(nb, C_HEADS // 2, n_chunks),
        in_specs=[pl.BlockSpec((None, TILE, 2 * LANES), lambda b, p, t: (b, t, p)),
                  pl.BlockSpec((None, lp, 2 * LANES), lambda b, p, t: (b, 0, p)),
                  pl.BlockSpec((None, lp, LANES), lambda b, p, t: (b, 0, p)),
                  _resident(kmask.shape)],
        out_specs=pl.BlockSpec((None, TILE, LANES), lambda b, p, t: (b, t, p)),
        out_shape=jax.ShapeDtypeStruct((nb, lp, C_HEADS * C_V_DIM), BF16),
        scratch_shapes=_flash_scratch(),
        compiler_params=_params(3),
        name="attn_c",
    )(qc, kc, vc, kmask)


_B_ORDER = (0, 3, 1, 2)


def _attn_b_kernel(sink_ref, q_ref, k_ref, v_ref, vs_ref, tab_ref, o_ref, *, lp):
    t = pl.program_id(1)
    lane = lax.broadcasted_iota(jnp.int32, (BLOCK, LANES), 1)
    kk = lax.broadcasted_iota(jnp.int32, (1, 3 * BLOCK), 1)
    row = lax.broadcasted_iota(jnp.int32, (B_HEADS * BLOCK, 1), 0)
    sink = jnp.full((B_HEADS * BLOCK, 1), sink_ref[_B_ORDER[0]], F32)
    for n in range(1, B_HEADS):
        sink = jnp.where(row >= n * BLOCK, sink_ref[_B_ORDER[n]], sink)
    half = B_HEADS * BLOCK // 2
    for blk in range(TILE // BLOCK):
        gblk = t * (TILE // BLOCK) + blk
        row0 = pl.multiple_of(gblk * BLOCK, BLOCK)
        kw = k_ref[pl.ds(row0, 3 * BLOCK), :]
        vw = v_ref[pl.ds(row0, 3 * BLOCK), :]
        vsw = vs_ref[pl.ds(row0, 3 * BLOCK), :]
        kslot = (gblk - 1) * BLOCK + kk
        row_mask = jnp.where((kslot >= META_START) & (kslot < lp), 0.0, NEG_INF)
        rows = slice(blk * BLOCK, (blk + 1) * BLOCK)
        qs = jnp.concatenate([q_ref[rows, h * LANES:(h + 1) * LANES] for h in _B_ORDER], axis=0)
        s = _dot_nt(qs, kw) + tab_ref[...] + row_mask
        m = jnp.maximum(jnp.max(s, axis=-1, keepdims=True), sink)
        p = jnp.exp(s - m)
        denom = jnp.sum(p, axis=-1, keepdims=True) + jnp.exp(sink - m)
        pb = p.astype(BF16)
        o03 = _dot(pb[0:half], vw) / denom[0:half]
        o12 = _dot(pb[half:], vsw) / denom[half:]
        o_ref[rows, 0:LANES] = jnp.where(lane < B_HEAD_DIM, o03[0:BLOCK], o12[0:BLOCK]).astype(BF16)
        o_ref[rows, LANES:2 * LANES] = jnp.where(lane < B_HEAD_DIM, o12[BLOCK:], o03[BLOCK:]).astype(BF16)


def _attn_b(bq, bk_pad, bv_pad, bvs_pad, tab, sinks):
    nb, lp, _ = bq.shape
    kvspec = pl.BlockSpec((None, lp + 2 * BLOCK, LANES), lambda b, t: (b, 0, 0))
    return pl.pallas_call(
        functools.partial(_attn_b_kernel, lp=lp),
        grid=(nb, lp // TILE),
        in_specs=[pl.BlockSpec(memory_space=pltpu.SMEM),
                  pl.BlockSpec((None, TILE, B_HEADS * LANES), lambda b, t: (b, t, 0)),
                  kvspec, kvspec, kvspec, _resident(tab.shape)],
        out_specs=pl.BlockSpec((None, TILE, B_HEADS * B_HEAD_DIM), lambda b, t: (b, t, 0)),
        out_shape=jax.ShapeDtypeStruct((nb, lp, B_HEADS * B_HEAD_DIM), BF16),
        compiler_params=_params(2),
        name="attn_b",
    )(sinks, bq, bk_pad, bv_pad, bvs_pad, tab)


def _t5_bucket(rel):
    half = N_BUCKETS // 2
    max_exact = half // 2
    ret = jnp.where(rel > 0, half, 0)
    n = jnp.abs(rel)
    nf = jnp.maximum(n, 1).astype(F32)
    large = max_exact + (jnp.log(nf / max_exact) / math.log(MAX_DISTANCE / max_exact)
                         * (half - max_exact)).astype(jnp.int32)
    large = jnp.minimum(large, half - 1)
    return ret + jnp.where(n < max_exact, n, large)


def _rot_cols(w):
    half = w.shape[-1] // 2
    return jnp.concatenate([-w[..., half:], w[..., :half]], axis=-1)


def _bias_of_rel(table, rel):
    bucket = _t5_bucket(rel)[None]
    out = jnp.zeros((table.shape[1],) + rel.shape, F32)
    for b in range(N_BUCKETS):
        out = jnp.where(bucket == b, table[b].reshape((-1,) + (1,) * rel.ndim), out)
    return out


def _bias_tables(rel_bias, lp):
    rb = rel_bias.astype(F32)
    rela = (jnp.arange(3 * TILE) - TILE)[None, :] - jnp.arange(TILE)[:, None]
    wide = _bias_of_rel(rb[:, :A_HEADS], rela)
    left, diag, right = (wide[:, :, d * TILE:(d + 1) * TILE] for d in range(3))
    far_a = rb[:, :A_HEADS][_t5_bucket(jnp.array([-2 * TILE, 2 * TILE]))].T
    lead = jnp.where(jnp.arange(TILE) >= META_START, 0.0, NEG_INF).astype(F32).reshape(1, 1, TILE)
    far_left = jnp.broadcast_to(far_a[:, 0][:, None, None], left.shape)
    far_right = jnp.broadcast_to(far_a[:, 1][:, None, None], left.shape)
    tab_a = jnp.stack([left, diag, right, far_left, far_right, left + lead, diag + lead, far_left + lead], axis=1)
    kmask = jnp.concatenate([jnp.zeros_like(lead), lead], axis=0)
    relb = (jnp.arange(3 * BLOCK) - BLOCK)[None, :] - jnp.arange(BLOCK)[:, None]
    tab_b = jnp.where((jnp.abs(relb) <= WINDOW)[None], _bias_of_rel(rb[:, A_HEADS:], relb), NEG_INF)
    tab_b = jnp.concatenate([tab_b[h] for h in _B_ORDER], axis=0)
    slot = jnp.arange(lp)
    inv = ROPE_THETA ** (-jnp.arange(0, C_ROPE_DIM, 2, dtype=F32) / C_ROPE_DIM)
    ang = (slot - META_START).astype(F32)[:, None] * inv[None, :]
    ang = jnp.concatenate([ang, ang], axis=-1)
    ones = jnp.ones((lp, C_NOPE_DIM), F32)
    zeros_n = jnp.zeros((lp, C_NOPE_DIM), F32)
    zeros_p = jnp.zeros((lp, LANES - C_NOPE_DIM - C_ROPE_DIM), F32)
    cos_t = jnp.concatenate([ones, jnp.cos(ang), zeros_p], axis=-1)
    sin_t = jnp.concatenate([zeros_n, jnp.sin(ang), zeros_p], axis=-1)
    return tab_a, tab_b, kmask, cos_t, sin_t


def _layer_weights(w_in, w_uq, w_ukv, w_out):
    d = w_in.shape[0]
    sizes = (512, 512, 512, 256, 128, 128, 256, 128, 32)
    offs = [0]
    for s in sizes:
        offs.append(offs[-1] + s)
    aq, ak, av, bq, bk, bv, cq, ckv, kr = (w_in[:, offs[i]:offs[i + 1]] for i in range(9))
    zeros64 = jnp.zeros((d, B_HEAD_DIM), F32)
    bq_pad = []
    for h in range(B_HEADS):
        qh = bq[:, h * B_HEAD_DIM:(h + 1) * B_HEAD_DIM] * (B_HEAD_DIM ** -0.5)
        bq_pad += [qh, zeros64] if h // (B_HEADS // B_KV_HEADS) == 0 else [zeros64, qh]
    bvs = jnp.concatenate([bv[:, B_HEAD_DIM:], bv[:, :B_HEAD_DIM]], axis=1)
    pad_lo = jnp.zeros((d, C_NOPE_DIM), F32)
    pad_hi = jnp.zeros((d, LANES - C_NOPE_DIM - C_ROPE_DIM), F32)
    w_big = jnp.concatenate(
        [aq * (A_QK_DIM ** -0.5), ak, av] + bq_pad + [bk, bv, bvs, cq, ckv,
                                                      pad_lo, kr, pad_hi, pad_lo, _rot_cols(kr), pad_hi],
        axis=1).astype(BF16)
    uq = w_uq.reshape(C_Q_RANK, C_HEADS, C_NOPE_DIM + C_ROPE_DIM)
    uq_nope, uq_rope = uq[..., :C_NOPE_DIM], uq[..., C_NOPE_DIM:]
    zq = jnp.zeros((C_Q_RANK, C_HEADS, LANES - C_NOPE_DIM - C_ROPE_DIM), F32)
    wuqm = jnp.concatenate([uq_nope, uq_rope, zq], axis=-1).reshape(C_Q_RANK, C_HEADS * LANES).astype(BF16)
    wuqr = jnp.concatenate([jnp.zeros_like(uq_nope), _rot_cols(uq_rope), zq], axis=-1
                           ).reshape(C_Q_RANK, C_HEADS * LANES).astype(BF16)
    ukv = w_ukv.reshape(C_KV_RANK, C_HEADS, C_NOPE_DIM + C_V_DIM)
    wukvk = jnp.concatenate([ukv[..., :C_NOPE_DIM], jnp.zeros((C_KV_RANK, C_HEADS, LANES - C_NOPE_DIM), F32)],
                            axis=-1).reshape(C_KV_RANK, C_HEADS * LANES).astype(BF16)
    wukvv = ukv[..., C_NOPE_DIM:].reshape(C_KV_RANK, C_HEADS * C_V_DIM).astype(BF16)
    n_a = A_HEADS * A_V_DIM
    n_b = B_HEADS * B_HEAD_DIM
    wo = w_out.astype(BF16)
    return w_big, wuqm, wuqr, wukvk, wukvv, wo[:n_a], wo[n_a:n_a + n_b], wo[n_a + n_b:]


def _trunk(x, meta, rel_bias, g_ffn1, w_ffn1_gu, w_ffn1_down, g_mix, w_in, diff_lambda, g_subln, sinks, g_cq,
           g_ckv, w_uq, w_ukv, w_out, g_ffn2, w_ffn2_gu, w_ffn2_down, g_final):
    nb, seq, d = x.shape
    lp = BLOCK + seq
    assert lp % TILE == 0 and d == D_MODEL
    lead = jnp.concatenate([jnp.zeros((META_START, d), x.dtype), meta.astype(x.dtype)], axis=0)
    h = jnp.concatenate([jnp.broadcast_to(lead[None], (nb, BLOCK, d)), x], axis=1)
    tab_a, tab_b, kmask, cos_t, sin_t = _bias_tables(rel_bias, lp)
    pad = ((0, 0), (BLOCK, BLOCK), (0, 0))
    for l in range(DEPTH):
        w_big, wuqm, wuqr, wukvk, wukvv, wo_a, wo_b, wo_c = _layer_weights(w_in[l], w_uq[l], w_ukv[l], w_out[l])
        h = _ffn(h, g_ffn1[l], w_ffn1_gu[l].astype(BF16), w_ffn1_down[l].astype(BF16))
        aq, ak, av, bq, bk, bv, bvs, qc, kc, vc = _inproj(h, g_mix[l], w_big, g_cq[l], g_ckv[l], wuqm, wuqr,
                                                           wukvk, wukvv, cos_t, sin_t)
        lam_init = 0.8 - 0.6 * math.exp(-0.3 * l)
        oa = _attn_a(aq, ak, av, tab_a, diff_lambda[l].astype(F32), g_subln[l], lam_init)
        ob = _attn_b(bq, jnp.pad(bk, pad), jnp.pad(bv, pad), jnp.pad(bvs, pad), tab_b, sinks[l].astype(F32))
        oc = _attn_c(qc, kc, vc, kmask)
        h = _ffn(h, g_ffn2[l], w_ffn2_gu[l].astype(BF16), w_ffn2_down[l].astype(BF16),
                 mix=(oa, ob, oc, wo_a, wo_b, wo_c), g_final=g_final if l == DEPTH - 1 else None)
    return h[:, BLOCK:]


def kernel(x_prompt, x_sample, meta, rel_bias, g_ffn1, w_ffn1_gu, w_ffn1_down, g_mix, w_in, diff_lambda, g_subln,
           sinks, g_cq, g_ckv, w_uq, w_ukv, w_out, g_ffn2, w_ffn2_gu, w_ffn2_down, g_final):
    n_prompt = x_prompt.shape[0]
    x = jnp.concatenate([x_prompt, x_sample], axis=0)
    y = _trunk(x, meta, rel_bias, g_ffn1, w_ffn1_gu, w_ffn1_down, g_mix, w_in, diff_lambda, g_subln, sinks, g_cq,
               g_ckv, w_uq, w_ukv, w_out, g_ffn2, w_ffn2_gu, w_ffn2_down, g_final)
    return (y[:n_prompt], y[n_prompt:])
```

```python
import functools
import math

import jax
import jax.numpy as jnp
from jax import lax
from jax.experimental import pallas as pl
from jax.experimental.pallas import tpu as pltpu

D_MODEL = 1024
DEPTH = 2
N_META = 16
BLOCK = 128
META_START = BLOCK - N_META
A_HEADS = 4
A_QK_DIM = 64
A_V_DIM = 2 * A_QK_DIM
B_HEADS = 4
B_KV_HEADS = 2
B_HEAD_DIM = 64
WINDOW = 128
C_HEADS = 4
C_Q_RANK = 256
C_KV_RANK = 128
C_NOPE_DIM = 64
C_ROPE_DIM = 32
C_V_DIM = 64
ROPE_THETA = 10000.0
N_BUCKETS = 32
MAX_DISTANCE = 128
D_FF = 2816
EPS = 1e-6

LANES = 128
TILE = 5 * BLOCK
FF_CHUNK = D_FF // 2
ROW_BLOCK = TILE // 2
PIPE_UNROLL = 26
VMEM_LIMIT = 56 * 1024 * 1024

F32 = jnp.float32
BF16 = jnp.bfloat16
NEG_INF = float("-inf")

_W_AQ, _W_AK, _W_AV, _W_BQ = 0, 512, 1024, 1536
_W_BK, _W_BV, _W_BVS = 2048, 2176, 2304
_W_CQ, _W_CKV, _W_KR, _W_KRR = 2432, 2688, 2816, 2944
_W_TOTAL = 3072


def _params(n_axes):
    return pltpu.CompilerParams(dimension_semantics=("arbitrary",) * n_axes, vmem_limit_bytes=VMEM_LIMIT)


def _resident(shape):
    return pl.BlockSpec(shape, lambda *_: (0,) * len(shape), pipeline_mode=pl.Buffered(1))


def _rms(x, g):
    return x * lax.rsqrt(jnp.mean(x * x, axis=-1, keepdims=True) + EPS) * g


def _dot(a, b):
    return jnp.dot(a, b, preferred_element_type=F32)


def _dot_nt(a, b):
    return lax.dot_general(a, b, (((1,), (1,)), ((), ())), preferred_element_type=F32)


def _ffn_kernel(x_ref, *refs, mixed, final_norm):
    refs = list(refs)
    x = x_ref[...]
    if mixed:
        oa_ref, ob_ref, oc_ref, wa_ref, wb_ref, wc_ref = refs[:6]
        del refs[:6]
        x = x + _dot(oa_ref[...], wa_ref[...]) + _dot(ob_ref[...], wb_ref[...]) + _dot(oc_ref[...], wc_ref[...])
    g_ref, wgu_ref, wd_ref = refs[:3]
    gf_ref = refs[3] if final_norm else None
    o_ref = refs[-1]
    xn = _rms(x, g_ref[...]).astype(BF16)
    acc = None
    for lo in range(0, D_FF, FF_CHUNK):
        g = _dot(xn, wgu_ref[:, lo:lo + FF_CHUNK])
        u = _dot(xn, wgu_ref[:, D_FF + lo:D_FF + lo + FF_CHUNK])
        a = (g * jax.nn.sigmoid(g) * u).astype(BF16)
        d = _dot(a, wd_ref[lo:lo + FF_CHUNK, :])
        acc = d if acc is None else acc + d
    y = x + 0.5 * acc
    if final_norm:
        y = _rms(y, gf_ref[...])
    o_ref[...] = y


def _ffn(h, g, wgu, wd, mix=None, g_final=None):
    nb, lp, d = h.shape

    def rows(n):
        return pl.BlockSpec((None, TILE, n), lambda b, t: (b, t, 0))

    tile = rows(d)
    in_specs = [tile]
    args = [h]
    if mix is not None:
        in_specs += [rows(o.shape[-1]) for o in mix[:3]] + [_resident(w.shape) for w in mix[3:]]
        args += list(mix)
    in_specs += [_resident((1, d)), _resident(wgu.shape), _resident(wd.shape)]
    args += [g.reshape(1, d), wgu, wd]
    if g_final is not None:
        in_specs.append(_resident((1, d)))
        args.append(g_final.reshape(1, d))
    return pl.pallas_call(
        functools.partial(_ffn_kernel, mixed=mix is not None, final_norm=g_final is not None),
        grid=(nb, lp // TILE),
        in_specs=in_specs,
        out_specs=tile,
        out_shape=jax.ShapeDtypeStruct(h.shape, F32),
        compiler_params=_params(2),
        name="ffn",
    )(*args)


def _inproj_kernel(h_ref, g_ref, w_ref, gcq_ref, gckv_ref, wuqm_ref, wuqr_ref, wukvk_ref, wukvv_ref,
                   cos_ref, sin_ref,
                   aq_ref, ak_ref, av_ref, bq_ref, bk_ref, bv_ref, bvs_ref, qc_ref, kc_ref, vc_ref, *, c_scale):
    xn = _rms(h_ref[...], g_ref[...]).astype(BF16)

    def proj(lo, n):
        return _dot(xn, w_ref[:, lo:lo + n])

    aq_ref[...] = proj(_W_AQ, 512).astype(BF16)
    ak_ref[...] = proj(_W_AK, 512).astype(BF16)
    av_ref[...] = proj(_W_AV, 512).astype(BF16)
    bq_ref[...] = proj(_W_BQ, 512).astype(BF16)
    bk_ref[...] = proj(_W_BK, 128).astype(BF16)
    bv_ref[...] = proj(_W_BV, 128).astype(BF16)
    bvs_ref[...] = proj(_W_BVS, 128).astype(BF16)

    cos = cos_ref[...]
    sin = sin_ref[...]
    cos4 = jnp.concatenate([cos] * C_HEADS, axis=1)
    sin4 = jnp.concatenate([sin] * C_HEADS, axis=1)
    cqn = _rms(proj(_W_CQ, C_Q_RANK), gcq_ref[...]).astype(BF16)
    q = _dot(cqn, wuqm_ref[...]) * cos4 + _dot(cqn, wuqr_ref[...]) * sin4
    qc_ref[...] = (q * c_scale).astype(BF16)

    ckvn = _rms(proj(_W_CKV, C_KV_RANK), gckv_ref[...]).astype(BF16)
    k_rope = proj(_W_KR, LANES) * cos + proj(_W_KRR, LANES) * sin
    kc_ref[...] = (_dot(ckvn, wukvk_ref[...]) + jnp.concatenate([k_rope] * C_HEADS, axis=1)).astype(BF16)
    vc_ref[...] = _dot(ckvn, wukvv_ref[...]).astype(BF16)


def _inproj(h, g, w_big, gcq, gckv, wuqm, wuqr, wukvk, wukvv, cos_t, sin_t):
    nb, lp, d = h.shape

    def tile(n):
        return pl.BlockSpec((None, TILE, n), lambda b, t: (b, t, 0))

    pos = pl.BlockSpec((TILE, LANES), lambda b, t: (t, 0))
    widths = (512, 512, 512, 512, 128, 128, 128, 512, 512, 256)
    return pl.pallas_call(
        functools.partial(_inproj_kernel, c_scale=(C_NOPE_DIM + C_ROPE_DIM) ** -0.5),
        grid=(nb, lp // TILE),
        in_specs=[tile(d), _resident((1, d)), _resident(w_big.shape), _resident((1, C_Q_RANK)),
                  _resident((1, C_KV_RANK)), _resident(wuqm.shape), _resident(wuqr.shape),
                  _resident(wukvk.shape), _resident(wukvv.shape), pos, pos],
        out_specs=[tile(n) for n in widths],
        out_shape=[jax.ShapeDtypeStruct((nb, lp, n), BF16) for n in widths],
        compiler_params=_params(2),
        name="inproj",
    )(h, g.reshape(1, d), w_big, gcq.reshape(1, -1), gckv.reshape(1, -1), wuqm, wuqr, wukvk, wukvv, cos_t, sin_t)


def _static(x):
    return isinstance(x, (int, bool))


def _aligned(x, m):
    return x if _static(x) else pl.multiple_of(x, m)


def _imin(a, b):
    return min(a, b) if _static(a) else jnp.minimum(a, b)


def _imax(a, b):
    return max(a, b) if _static(a) else jnp.maximum(a, b)


def _iselect(c, a, b):
    return (a if c else b) if _static(c) else jnp.where(c, a, b)


def _flash_scratch():
    return ([pltpu.VMEM((ROW_BLOCK, TILE), F32)] * 2 + [pltpu.VMEM((ROW_BLOCK, LANES), F32)] * 2
            + [pltpu.VMEM((ROW_BLOCK, TILE), BF16)] * 2 + [pltpu.VMEM((ROW_BLOCK, LANES), F32)] * 2
            + [pltpu.VMEM((2 * TILE, LANES), F32), pltpu.VMEM((2 * TILE, 2 * LANES), F32)])


def _flash_pipeline(n_tiles, scores, values, stat_row, scratch):
    s0, s1, x0, x1, p0, p1, a0, a1, m_ref, acc_ref = scratch
    s_bufs, x_bufs, p_bufs, a_bufs = (s0, s1), (x0, x1), (p0, p1), (a0, a1)
    unroll = max(k for k in range(2, PIPE_UNROLL + 1, 2) if n_tiles % k == 0)
    m_ref[...] = jnp.full(m_ref.shape, NEG_INF, F32)
    acc_ref[...] = jnp.zeros(acc_ref.shape, F32)
    p1[...] = jnp.zeros(p1.shape, BF16)
    a1[...] = jnp.ones(a1.shape, F32)

    def issue(t, par):
        s = scores(t, par)
        s_bufs[par][...] = s
        x_bufs[par][...] = jnp.broadcast_to(jnp.max(s, axis=-1, keepdims=True), (ROW_BLOCK, LANES))

    def accumulate(t, par):
        rows = pl.ds(stat_row(t, par), ROW_BLOCK)
        alpha = a_bufs[par][...]
        acc_ref[rows, :] = (jnp.concatenate([alpha, alpha], axis=1) * acc_ref[rows, :]
                            + _dot(p_bufs[par][...], values(t, par)))

    def step(t, par):
        issue(_imin(t + 1, n_tiles - 1), 1 - par)
        rows = pl.ds(stat_row(t, par), ROW_BLOCK)
        m_old = m_ref[rows, :]
        m_new = jnp.maximum(m_old, x_bufs[par][...])
        p_bufs[par][...] = jnp.exp((s_bufs[par][...] - jnp.concatenate([m_new] * (TILE // LANES), axis=1)
                                    ).astype(BF16))
        a_bufs[par][...] = jnp.exp(m_old - m_new)
        m_ref[rows, :] = m_new
        accumulate(_imax(t - 1, 0), 1 - par)

    issue(0, 0)

    def body(u, carry):
        for k in range(unroll):
            step(unroll * u + k, k % 2)
        return carry

    if unroll == n_tiles:
        for k in range(n_tiles):
            step(k, k % 2)
    else:
        lax.fori_loop(0, n_tiles // unroll, body, 0)
    accumulate(n_tiles - 1, 1)


(_TAB_LEFT, _TAB_DIAG, _TAB_RIGHT, _TAB_FAR_LEFT, _TAB_FAR_RIGHT,
 _TAB_LEFT_LEAD, _TAB_DIAG_LEAD, _TAB_FAR_LEAD) = range(8)


def _ones_right(v):
    return jnp.concatenate([v, jnp.ones(v.shape, v.dtype)], axis=1)


def _attn_a_kernel(q_ref, k_ref, v_ref, tab_ref, dl_ref, gs_ref, o_ref, qq_ref, *scratch, lam_init, n_chunks):
    i = pl.program_id(2)
    q = q_ref[...]
    lane = lax.broadcasted_iota(jnp.int32, q.shape, 1)
    zero = jnp.zeros_like(q)
    qq_ref[0:TILE, :] = jnp.where(lane < A_QK_DIM, q, zero)
    qq_ref[TILE:2 * TILE, :] = jnp.where(lane >= A_QK_DIM, q, zero)
    blocks = 2 * TILE // ROW_BLOCK

    def q_row(t):
        return _aligned((t % blocks) * ROW_BLOCK, ROW_BLOCK)

    def k_row(t):
        return _aligned((t // blocks) * TILE, TILE)

    def scores(t, par):
        j = t // blocks
        lead = jnp.where(i == 0, _TAB_DIAG_LEAD, jnp.where(i == 1, _TAB_LEFT_LEAD, _TAB_FAR_LEAD))
        rest = jnp.where(j < i - 1, _TAB_FAR_LEFT, jnp.where(j > i + 1, _TAB_FAR_RIGHT, j - i + 1))
        table = jnp.where(j == 0, lead, rest)
        row = _aligned((t % (TILE // ROW_BLOCK)) * ROW_BLOCK, ROW_BLOCK)
        return (_dot_nt(qq_ref[pl.ds(q_row(t), ROW_BLOCK), :], k_ref[pl.ds(k_row(t), TILE), :])
                + tab_ref[table, pl.ds(row, ROW_BLOCK), :])

    def values(t, par):
        return _ones_right(v_ref[pl.ds(k_row(t), TILE), :])

    _flash_pipeline(n_chunks * blocks, scores, values, lambda t, par: q_row(t), scratch)

    acc_ref = scratch[-1]
    o = acc_ref[:, 0:A_V_DIM] / acc_ref[:, A_V_DIM:2 * A_V_DIM]
    dl = dl_ref[...]
    lam = (jnp.exp(jnp.sum(dl[0:1] * dl[1:2], axis=-1, keepdims=True))
           - jnp.exp(jnp.sum(dl[2:3] * dl[3:4], axis=-1, keepdims=True)) + lam_init)
    w = o[0:TILE] - lam * o[TILE:2 * TILE]
    o_ref[...] = (_rms(w, gs_ref[...]) * (1.0 - lam_init)).astype(BF16)


def _attn_a(aq, ak, av, tab, dl, gs, lam_init):
    nb, lp, _ = aq.shape
    n_chunks = lp // TILE
    qspec = pl.BlockSpec((None, TILE, LANES), lambda b, h, t: (b, t, h))
    kvspec = pl.BlockSpec((None, lp, LANES), lambda b, h, t: (b, 0, h))
    return pl.pallas_call(
        functools.partial(_attn_a_kernel, lam_init=lam_init, n_chunks=n_chunks),
        grid=(nb, A_HEADS, n_chunks),
        in_specs=[qspec, kvspec, kvspec,
                  pl.BlockSpec((None,) + tab.shape[1:], lambda b, h, t: (h, 0, 0, 0)),
                  _resident((4, A_QK_DIM)), _resident((1, A_V_DIM))],
        out_specs=qspec,
        out_shape=jax.ShapeDtypeStruct((nb, lp, A_HEADS * A_V_DIM), BF16),
        scratch_shapes=[pltpu.VMEM((2 * TILE, LANES), BF16)] + _flash_scratch(),
        compiler_params=_params(3),
        name="attn_a",
    )(aq, ak, av, tab, dl, gs.reshape(1, A_V_DIM))


def _attn_c_kernel(q_ref, k_ref, v_ref, kmask_ref, o_ref, *scratch, n_chunks):
    blocks = 2 * TILE // ROW_BLOCK

    def q_row(t):
        return _aligned(((t % blocks) // 2) * ROW_BLOCK, ROW_BLOCK)

    def k_row(t):
        return _aligned((t // blocks) * TILE, TILE)

    def scores(t, par):
        cols = slice(par * LANES, (par + 1) * LANES)
        return (_dot_nt(q_ref[pl.ds(q_row(t), ROW_BLOCK), cols], k_ref[pl.ds(k_row(t), TILE), cols])
                + kmask_ref[_iselect(t < blocks, 1, 0)])

    def values(t, par):
        return _ones_right(v_ref[pl.ds(k_row(t), TILE), :])

    def stat_row(t, par):
        return _aligned(par * TILE + q_row(t), ROW_BLOCK)

    _flash_pipeline(n_chunks * blocks, scores, values, stat_row, scratch)

    acc_ref = scratch[-1]
    o = acc_ref[:, 0:LANES] / acc_ref[:, LANES:2 * LANES]
    lane = lax.broadcasted_iota(jnp.int32, (TILE, LANES), 1)
    o_ref[...] = jnp.where(lane < C_V_DIM, o[0:TILE], o[TILE:2 * TILE]).astype(BF16)


def _attn_c(qc, kc, vc, kmask):
    nb, lp, _ = qc.shape
    n_chunks = lp // TILE
    return pl.pallas_call(
        functools.partial(_attn_c_kernel, n_chunks=n_chunks),
        grid=(nb, C_HEADS // 2, n_chunks),
        in_specs=[pl.BlockSpec((None, TILE, 2 * LANES), lambda b, p, t: (b, t, p)),
                  pl.BlockSpec((None, lp, 2 * LANES), lambda b, p, t: (b, 0, p)),
                  pl.BlockSpec((None, lp, LANES), lambda b, p, t: (b, 0, p)),
                  _resident(kmask.shape)],
        out_specs=pl.BlockSpec((None, TILE, LANES), lambda b, p, t: (b, t, p)),
        out_shape=jax.ShapeDtypeStruct((nb, lp, C_HEADS * C_V_DIM), BF16),
        scratch_shapes=_flash_scratch(),
        compiler_params=_params(3),
        name="attn_c",
    )(qc, kc, vc, kmask)


_B_ORDER = (0, 3, 1, 2)


def _attn_b_kernel(sink_ref, q_ref, k_ref, v_ref, vs_ref, tab_ref, o_ref, *, lp):
    t = pl.program_id(1)
    lane = lax.broadcasted_iota(jnp.int32, (BLOCK, LANES), 1)
    kk = lax.broadcasted_iota(jnp.int32, (1, 3 * BLOCK), 1)
    row = lax.broadcasted_iota(jnp.int32, (B_HEADS * BLOCK, 1), 0)
    sink = jnp.full((B_HEADS * BLOCK, 1), sink_ref[_B_ORDER[0]], F32)
    for n in range(1, B_HEADS):
        sink = jnp.where(row >= n * BLOCK, sink_ref[_B_ORDER[n]], sink)
    half = B_HEADS * BLOCK // 2
    for blk in range(TILE // BLOCK):
        gblk = t * (TILE // BLOCK) + blk
        row0 = pl.multiple_of(gblk * BLOCK, BLOCK)
        kw = k_ref[pl.ds(row0, 3 * BLOCK), :]
        vw = v_ref[pl.ds(row0, 3 * BLOCK), :]
        vsw = vs_ref[pl.ds(row0, 3 * BLOCK), :]
        kslot = (gblk - 1) * BLOCK + kk
        row_mask = jnp.where((kslot >= META_START) & (kslot < lp), 0.0, NEG_INF)
        rows = slice(blk * BLOCK, (blk + 1) * BLOCK)
        qs = jnp.concatenate([q_ref[rows, h * LANES:(h + 1) * LANES] for h in _B_ORDER], axis=0)
        s = _dot_nt(qs, kw) + tab_ref[...] + row_mask
        m = jnp.maximum(jnp.max(s, axis=-1, keepdims=True), sink)
        p = jnp.exp(s - m)
        denom = jnp.sum(p, axis=-1, keepdims=True) + jnp.exp(sink - m)
        pb = p.astype(BF16)
        o03 = _dot(pb[0:half], vw) / denom[0:half]
        o12 = _dot(pb[half:], vsw) / denom[half:]
        o_ref[rows, 0:LANES] = jnp.where(lane < B_HEAD_DIM, o03[0:BLOCK], o12[0:BLOCK]).astype(BF16)
        o_ref[rows, LANES:2 * LANES] = jnp.where(lane < B_HEAD_DIM, o12[BLOCK:], o03[BLOCK:]).astype(BF16)


def _attn_b(bq, bk_pad, bv_pad, bvs_pad, tab, sinks):
    nb, lp, _ = bq.shape
    kvspec = pl.BlockSpec((None, lp + 2 * BLOCK, LANES), lambda b, t: (b, 0, 0))
    return pl.pallas_call(
        functools.partial(_attn_b_kernel, lp=lp),
        grid=(nb, lp // TILE),
        in_specs=[pl.BlockSpec(memory_space=pltpu.SMEM),
                  pl.BlockSpec((None, TILE, B_HEADS * LANES), lambda b, t: (b, t, 0)),
                  kvspec, kvspec, kvspec, _resident(tab.shape)],
        out_specs=pl.BlockSpec((None, TILE, B_HEADS * B_HEAD_DIM), lambda b, t: (b, t, 0)),
        out_shape=jax.ShapeDtypeStruct((nb, lp, B_HEADS * B_HEAD_DIM), BF16),
        compiler_params=_params(2),
        name="attn_b",
    )(sinks, bq, bk_pad, bv_pad, bvs_pad, tab)


def _t5_bucket(rel):
    half = N_BUCKETS // 2
    max_exact = half // 2
    ret = jnp.where(rel > 0, half, 0)
    n = jnp.abs(rel)
    nf = jnp.maximum(n, 1).astype(F32)
    large = max_exact + (jnp.log(nf / max_exact) / math.log(MAX_DISTANCE / max_exact)
                         * (half - max_exact)).astype(jnp.int32)
    large = jnp.minimum(large, half - 1)
    return ret + jnp.where(n < max_exact, n, large)


def _rot_cols(w):
    half = w.shape[-1] // 2
    return jnp.concatenate([-w[..., half:], w[..., :half]], axis=-1)


def _bias_of_rel(table, rel):
    bucket = _t5_bucket(rel)[None]
    out = jnp.zeros((table.shape[1],) + rel.shape, F32)
    for b in range(N_BUCKETS):
        out = jnp.where(bucket == b, table[b].reshape((-1,) + (1,) * rel.ndim), out)
    return out


def _bias_tables(rel_bias, lp):
    rb = rel_bias.astype(F32)
    rela = (jnp.arange(3 * TILE) - TILE)[None, :] - jnp.arange(TILE)[:, None]
    wide = _bias_of_rel(rb[:, :A_HEADS], rela)
    left, diag, right = (wide[:, :, d * TILE:(d + 1) * TILE] for d in range(3))
    far_a = rb[:, :A_HEADS][_t5_bucket(jnp.array([-2 * TILE, 2 * TILE]))].T
    lead = jnp.where(jnp.arange(TILE) >= META_START, 0.0, NEG_INF).astype(F32).reshape(1, 1, TILE)
    far_left = jnp.broadcast_to(far_a[:, 0][:, None, None], left.shape)
    far_right = jnp.broadcast_to(far_a[:, 1][:, None, None], left.shape)
    tab_a = jnp.stack([left, diag, right, far_left, far_right, left + lead, diag + lead, far_left + lead], axis=1)
    kmask = jnp.concatenate([jnp.zeros_like(lead), lead], axis=0)
    relb = (jnp.arange(3 * BLOCK) - BLOCK)[None, :] - jnp.arange(BLOCK)[:, None]
    tab_b = jnp.where((jnp.abs(relb) <= WINDOW)[None], _bias_of_rel(rb[:, A_HEADS:], relb), NEG_INF)
    tab_b = jnp.concatenate([tab_b[h] for h in _B_ORDER], axis=0)
    slot = jnp.arange(lp)
    inv = ROPE_THETA ** (-jnp.arange(0, C_ROPE_DIM, 2, dtype=F32) / C_ROPE_DIM)
    ang = (slot - META_START).astype(F32)[:, None] * inv[None, :]
    ang = jnp.concatenate([ang, ang], axis=-1)
    ones = jnp.ones((lp, C_NOPE_DIM), F32)
    zeros_n = jnp.zeros((lp, C_NOPE_DIM), F32)
    zeros_p = jnp.zeros((lp, LANES - C_NOPE_DIM - C_ROPE_DIM), F32)
    cos_t = jnp.concatenate([ones, jnp.cos(ang), zeros_p], axis=-1)
    sin_t = jnp.concatenate([zeros_n, jnp.sin(ang), zeros_p], axis=-1)
    return tab_a, tab_b, kmask, cos_t, sin_t


def _layer_weights(w_in, w_uq, w_ukv, w_out):
    d = w_in.shape[0]
    sizes = (512, 512, 512, 256, 128, 128, 256, 128, 32)
    offs = [0]
    for s in sizes:
        offs.append(offs[-1] + s)
    aq, ak, av, bq, bk, bv, cq, ckv, kr = (w_in[:, offs[i]:offs[i + 1]] for i in range(9))
    zeros64 = jnp.zeros((d, B_HEAD_DIM), F32)
    bq_pad = []
    for h in range(B_HEADS):
        qh = bq[:, h * B_HEAD_DIM:(h + 1) * B_HEAD_DIM] * (B_HEAD_DIM ** -0.5)
        bq_pad += [qh, zeros64] if h // (B_HEADS // B_KV_HEADS) == 0 else [zeros64, qh]
    bvs = jnp.concatenate([bv[:, B_HEAD_DIM:], bv[:, :B_HEAD_DIM]], axis=1)
    pad_lo = jnp.zeros((d, C_NOPE_DIM), F32)
    pad_hi = jnp.zeros((d, LANES - C_NOPE_DIM - C_ROPE_DIM), F32)
    w_big = jnp.concatenate(
        [aq * (A_QK_DIM ** -0.5), ak, av] + bq_pad + [bk, bv, bvs, cq, ckv,
                                                      pad_lo, kr, pad_hi, pad_lo, _rot_cols(kr), pad_hi],
        axis=1).astype(BF16)
    uq = w_uq.reshape(C_Q_RANK, C_HEADS, C_NOPE_DIM + C_ROPE_DIM)
    uq_nope, uq_rope = uq[..., :C_NOPE_DIM], uq[..., C_NOPE_DIM:]
    zq = jnp.zeros((C_Q_RANK, C_HEADS, LANES - C_NOPE_DIM - C_ROPE_DIM), F32)
    wuqm = jnp.concatenate([uq_nope, uq_rope, zq], axis=-1).reshape(C_Q_RANK, C_HEADS * LANES).astype(BF16)
    wuqr = jnp.concatenate([jnp.zeros_like(uq_nope), _rot_cols(uq_rope), zq], axis=-1
                           ).reshape(C_Q_RANK, C_HEADS * LANES).astype(BF16)
    ukv = w_ukv.reshape(C_KV_RANK, C_HEADS, C_NOPE_DIM + C_V_DIM)
    wukvk = jnp.concatenate([ukv[..., :C_NOPE_DIM], jnp.zeros((C_KV_RANK, C_HEADS, LANES - C_NOPE_DIM), F32)],
                            axis=-1).reshape(C_KV_RANK, C_HEADS * LANES).astype(BF16)
    wukvv = ukv[..., C_NOPE_DIM:].reshape(C_KV_RANK, C_HEADS * C_V_DIM).astype(BF16)
    n_a = A_HEADS * A_V_DIM
    n_b = B_HEADS * B_HEAD_DIM
    wo = w_out.astype(BF16)
    return w_big, wuqm, wuqr, wukvk, wukvv, wo[:n_a], wo[n_a:n_a + n_b], wo[n_a + n_b:]


def _trunk(x, meta, rel_bias, g_ffn1, w_ffn1_gu, w_ffn1_down, g_mix, w_in, diff_lambda, g_subln, sinks, g_cq,
           g_ckv, w_uq, w_ukv, w_out, g_ffn2, w_ffn2_gu, w_ffn2_down, g_final):
    nb, seq, d = x.shape
    lp = BLOCK + seq
    assert lp % TILE == 0 and d == D_MODEL
    lead = jnp.concatenate([jnp.zeros((META_START, d), x.dtype), meta.astype(x.dtype)], axis=0)
    h = jnp.concatenate([jnp.broadcast_to(lead[None], (nb, BLOCK, d)), x], axis=1)
    tab_a, tab_b, kmask, cos_t, sin_t = _bias_tables(rel_bias, lp)
    pad = ((0, 0), (BLOCK, BLOCK), (0, 0))
    for l in range(DEPTH):
        w_big, wuqm, wuqr, wukvk, wukvv, wo_a, wo_b, wo_c = _layer_weights(w_in[l], w_uq[l], w_ukv[l], w_out[l])
        h = _ffn(h, g_ffn1[l], w_ffn1_gu[l].astype(BF16), w_ffn1_down[l].astype(BF16))
        aq, ak, av, bq, bk, bv, bvs, qc, kc, vc = _inproj(h, g_mix[l], w_big, g_cq[l], g_ckv[l], wuqm, wuqr,
                                                           wukvk, wukvv, cos_t, sin_t)
        lam_init = 0.8 - 0.6 * math.exp(-0.3 * l)
        oa = _attn_a(aq, ak, av, tab_a, diff_lambda[l].astype(F32), g_subln[l], lam_init)
        ob = _attn_b(bq, jnp.pad(bk, pad), jnp.pad(bv, pad), jnp.pad(bvs, pad), tab_b, sinks[l].astype(F32))
        oc = _attn_c(qc, kc, vc, kmask)
        h = _ffn(h, g_ffn2[l], w_ffn2_gu[l].astype(BF16), w_ffn2_down[l].astype(BF16),
                 mix=(oa, ob, oc, wo_a, wo_b, wo_c), g_final=g_final if l == DEPTH - 1 else None)
    return h[:, BLOCK:]


def kernel(x_prompt, x_sample, meta, rel_bias, g_ffn1, w_ffn1_gu, w_ffn1_down, g_mix, w_in, diff_lambda, g_subln,
           sinks, g_cq, g_ckv, w_uq, w_ukv, w_out, g_ffn2, w_ffn2_gu, w_ffn2_down, g_final):
    n_prompt = x_prompt.shape[0]
    x = jnp.concatenate([x_prompt, x_sample], axis=0)
    y = _trunk(x, meta, rel_bias, g_ffn1, w_ffn1_gu, w_ffn1_down, g_mix, w_in, diff_lambda, g_subln, sinks, g_cq,
               g_ckv, w_uq, w_ukv, w_out, g_ffn2, w_ffn2_gu, w_ffn2_down, g_final)
    return (y[:n_prompt], y[n_prompt:])
```

```python
import functools
import math

import jax
import jax.numpy as jnp
from jax import lax
from jax.experimental import pallas as pl
from jax.experimental.pallas import tpu as pltpu

D_MODEL = 1024
DEPTH = 2
N_META = 16
BLOCK = 128
META_START = BLOCK - N_META
A_HEADS = 4
A_QK_DIM = 64
A_V_DIM = 2 * A_QK_DIM
B_HEADS = 4
B_KV_HEADS = 2
B_HEAD_DIM = 64
WINDOW = 128
C_HEADS = 4
C_Q_RANK = 256
C_KV_RANK = 128
C_NOPE_DIM = 64
C_ROPE_DIM = 32
C_V_DIM = 64
ROPE_THETA = 10000.0
N_BUCKETS = 32
MAX_DISTANCE = 128
D_FF = 2816
EPS = 1e-6

LANES = 128
TILE = 5 * BLOCK
FF_CHUNK = D_FF // 2
ROW_BLOCK = TILE // 2
A_PIPE_UNROLL = 4
C_PIPE_UNROLL = 26
VMEM_LIMIT = 56 * 1024 * 1024

F32 = jnp.float32
BF16 = jnp.bfloat16
NEG_INF = float("-inf")

_W_AQ, _W_AK, _W_AV, _W_BQ = 0, 512, 1024, 1536
_W_BK, _W_BV, _W_BVS = 2048, 2176, 2304
_W_CQ, _W_CKV, _W_KR, _W_KRR = 2432, 2688, 2816, 2944
_W_TOTAL = 3072


def _params(n_axes):
    return pltpu.CompilerParams(dimension_semantics=("arbitrary",) * n_axes, vmem_limit_bytes=VMEM_LIMIT)


def _resident(shape):
    return pl.BlockSpec(shape, lambda *_: (0,) * len(shape), pipeline_mode=pl.Buffered(1))


def _rms(x, g):
    return x * lax.rsqrt(jnp.mean(x * x, axis=-1, keepdims=True) + EPS) * g


def _dot(a, b):
    return jnp.dot(a, b, preferred_element_type=F32)


def _dot_nt(a, b):
    return lax.dot_general(a, b, (((1,), (1,)), ((), ())), preferred_element_type=F32)


def _ffn_kernel(x_ref, *refs, mixed, final_norm):
    refs = list(refs)
    x = x_ref[...]
    if mixed:
        oa_ref, ob_ref, oc_ref, wa_ref, wb_ref, wc_ref = refs[:6]
        del refs[:6]
        x = x + _dot(oa_ref[...], wa_ref[...]) + _dot(ob_ref[...], wb_ref[...]) + _dot(oc_ref[...], wc_ref[...])
    g_ref, wgu_ref, wd_ref = refs[:3]
    gf_ref = refs[3] if final_norm else None
    o_ref = refs[-1]
    xn = _rms(x, g_ref[...]).astype(BF16)
    acc = None
    for lo in range(0, D_FF, FF_CHUNK):
        g = _dot(xn, wgu_ref[:, lo:lo + FF_CHUNK])
        u = _dot(xn, wgu_ref[:, D_FF + lo:D_FF + lo + FF_CHUNK])
        a = (g * jax.nn.sigmoid(g) * u).astype(BF16)
        d = _dot(a, wd_ref[lo:lo + FF_CHUNK, :])
        acc = d if acc is None else acc + d
    y = x + 0.5 * acc
    if final_norm:
        y = _rms(y, gf_ref[...])
    o_ref[...] = y


def _ffn(h, g, wgu, wd, mix=None, g_final=None):
    nb, lp, d = h.shape

    def rows(n):
        return pl.BlockSpec((None, TILE, n), lambda b, t: (b, t, 0))

    tile = rows(d)
    in_specs = [tile]
    args = [h]
    if mix is not None:
        in_specs += [rows(o.shape[-1]) for o in mix[:3]] + [_resident(w.shape) for w in mix[3:]]
        args += list(mix)
    in_specs += [_resident((1, d)), _resident(wgu.shape), _resident(wd.shape)]
    args += [g.reshape(1, d), wgu, wd]
    if g_final is not None:
        in_specs.append(_resident((1, d)))
        args.append(g_final.reshape(1, d))
    return pl.pallas_call(
        functools.partial(_ffn_kernel, mixed=mix is not None, final_norm=g_final is not None),
        grid=(nb, lp // TILE),
        in_specs=in_specs,
        out_specs=tile,
        out_shape=jax.ShapeDtypeStruct(h.shape, F32),
        compiler_params=_params(2),
        name="ffn",
    )(*args)


def _inproj_kernel(h_ref, g_ref, w_ref, gcq_ref, gckv_ref, wuqm_ref, wuqr_ref, wukvk_ref, wukvv_ref,
                   cos_ref, sin_ref,
                   aq_ref, ak_ref, av_ref, bq_ref, bk_ref, bv_ref, bvs_ref, qc_ref, kc_ref, vc_ref, *, c_scale):
    xn = _rms(h_ref[...], g_ref[...]).astype(BF16)

    def proj(lo, n):
        return _dot(xn, w_ref[:, lo:lo + n])

    aq_ref[...] = proj(_W_AQ, 512).astype(BF16)
    ak_ref[...] = proj(_W_AK, 512).astype(BF16)
    av_ref[...] = proj(_W_AV, 512).astype(BF16)
    bq_ref[...] = proj(_W_BQ, 512).astype(BF16)
    bk_ref[...] = proj(_W_BK, 128).astype(BF16)
    bv_ref[...] = proj(_W_BV, 128).astype(BF16)
    bvs_ref[...] = proj(_W_BVS, 128).astype(BF16)

    cos = cos_ref[...]
    sin = sin_ref[...]
    cos4 = jnp.concatenate([cos] * C_HEADS, axis=1)
    sin4 = jnp.concatenate([sin] * C_HEADS, axis=1)
    cqn = _rms(proj(_W_CQ, C_Q_RANK), gcq_ref[...]).astype(BF16)
    q = _dot(cqn, wuqm_ref[...]) * cos4 + _dot(cqn, wuqr_ref[...]) * sin4
    qc_ref[...] = (q * c_scale).astype(BF16)

    ckvn = _rms(proj(_W_CKV, C_KV_RANK), gckv_ref[...]).astype(BF16)
    k_rope = proj(_W_KR, LANES) * cos + proj(_W_KRR, LANES) * sin
    kc_ref[...] = (_dot(ckvn, wukvk_ref[...]) + jnp.concatenate([k_rope] * C_HEADS, axis=1)).astype(BF16)
    vc_ref[...] = _dot(ckvn, wukvv_ref[...]).astype(BF16)


def _inproj(h, g, w_big, gcq, gckv, wuqm, wuqr, wukvk, wukvv, cos_t, sin_t):
    nb, lp, d = h.shape

    def tile(n):
        return pl.BlockSpec((None, TILE, n), lambda b, t: (b, t, 0))

    pos = pl.BlockSpec((TILE, LANES), lambda b, t: (t, 0))
    widths = (512, 512, 512, 512, 128, 128, 128, 512, 512, 256)
    return pl.pallas_call(
        functools.partial(_inproj_kernel, c_scale=(C_NOPE_DIM + C_ROPE_DIM) ** -0.5),
        grid=(nb, lp // TILE),
        in_specs=[tile(d), _resident((1, d)), _resident(w_big.shape), _resident((1, C_Q_RANK)),
                  _resident((1, C_KV_RANK)), _resident(wuqm.shape), _resident(wuqr.shape),
                  _resident(wukvk.shape), _resident(wukvv.shape), pos, pos],
        out_specs=[tile(n) for n in widths],
        out_shape=[jax.ShapeDtypeStruct((nb, lp, n), BF16) for n in widths],
        compiler_params=_params(2),
        name="inproj",
    )(h, g.reshape(1, d), w_big, gcq.reshape(1, -1), gckv.reshape(1, -1), wuqm, wuqr, wukvk, wukvv, cos_t, sin_t)


def _static(x):
    return isinstance(x, (int, bool))


def _aligned(x, m):
    return x if _static(x) else pl.multiple_of(x, m)


def _imin(a, b):
    return min(a, b) if _static(a) else jnp.minimum(a, b)


def _imax(a, b):
    return max(a, b) if _static(a) else jnp.maximum(a, b)


def _iselect(c, a, b):
    return (a if c else b) if _static(c) else jnp.where(c, a, b)


def _flash_scratch():
    return ([pltpu.VMEM((ROW_BLOCK, TILE), F32)] * 2 + [pltpu.VMEM((ROW_BLOCK, LANES), F32)] * 2
            + [pltpu.VMEM((ROW_BLOCK, TILE), BF16)] * 2 + [pltpu.VMEM((ROW_BLOCK, LANES), F32)] * 2
            + [pltpu.VMEM((2 * TILE, LANES), F32), pltpu.VMEM((2 * TILE, 2 * LANES), F32)])


def _flash_pipeline(n_tiles, scores, values, stat_row, scratch, max_unroll):
    s0, s1, x0, x1, p0, p1, a0, a1, m_ref, acc_ref = scratch
    s_bufs, x_bufs, p_bufs, a_bufs = (s0, s1), (x0, x1), (p0, p1), (a0, a1)
    unroll = max(k for k in range(2, max_unroll + 1, 2) if n_tiles % k == 0)
    m_ref[...] = jnp.full(m_ref.shape, NEG_INF, F32)
    acc_ref[...] = jnp.zeros(acc_ref.shape, F32)
    p1[...] = jnp.zeros(p1.shape, BF16)
    a1[...] = jnp.ones(a1.shape, F32)

    def issue(t, par):
        s = scores(t, par)
        s_bufs[par][...] = s
        x_bufs[par][...] = jnp.broadcast_to(jnp.max(s, axis=-1, keepdims=True), (ROW_BLOCK, LANES))

    def accumulate(t, par):
        rows = pl.ds(stat_row(t, par), ROW_BLOCK)
        alpha = a_bufs[par][...]
        acc_ref[rows, :] = (jnp.concatenate([alpha, alpha], axis=1) * acc_ref[rows, :]
                            + _dot(p_bufs[par][...], values(t, par)))

    def step(t, par):
        issue(_imin(t + 1, n_tiles - 1), 1 - par)
        rows = pl.ds(stat_row(t, par), ROW_BLOCK)
        m_old = m_ref[rows, :]
        m_new = jnp.maximum(m_old, x_bufs[par][...])
        p_bufs[par][...] = jnp.exp((s_bufs[par][...] - jnp.concatenate([m_new] * (TILE // LANES), axis=1)
                                    ).astype(BF16))
        a_bufs[par][...] = jnp.exp(m_old - m_new)
        m_ref[rows, :] = m_new
        accumulate(_imax(t - 1, 0), 1 - par)

    issue(0, 0)

    def body(u, carry):
        for k in range(unroll):
            step(unroll * u + k, k % 2)
        return carry

    if unroll == n_tiles:
        for k in range(n_tiles):
            step(k, k % 2)
    else:
        lax.fori_loop(0, n_tiles // unroll, body, 0)
    accumulate(n_tiles - 1, 1)


(_TAB_LEFT, _TAB_DIAG, _TAB_RIGHT, _TAB_FAR_LEFT, _TAB_FAR_RIGHT,
 _TAB_LEFT_LEAD, _TAB_DIAG_LEAD, _TAB_FAR_LEAD) = range(8)


def _ones_right(v):
    return jnp.concatenate([v, jnp.ones(v.shape, v.dtype)], axis=1)


def _attn_a_kernel(q_ref, k_ref, v_ref, tab_ref, dl_ref, gs_ref, o_ref, qq_ref, *scratch, lam_init, n_chunks):
    i = pl.program_id(2)
    q = q_ref[...]
    lane = lax.broadcasted_iota(jnp.int32, q.shape, 1)
    zero = jnp.zeros_like(q)
    qq_ref[0:TILE, :] = jnp.where(lane < A_QK_DIM, q, zero)
    qq_ref[TILE:2 * TILE, :] = jnp.where(lane >= A_QK_DIM, q, zero)
    blocks = 2 * TILE // ROW_BLOCK

    def q_row(t):
        return _aligned((t % blocks) * ROW_BLOCK, ROW_BLOCK)

    def k_row(t):
        return _aligned((t // blocks) * TILE, TILE)

    def scores(t, par):
        j = t // blocks
        lead = jnp.where(i == 0, _TAB_DIAG_LEAD, jnp.where(i == 1, _TAB_LEFT_LEAD, _TAB_FAR_LEAD))
        rest = jnp.where(j < i - 1, _TAB_FAR_LEFT, jnp.where(j > i + 1, _TAB_FAR_RIGHT, j - i + 1))
        table = jnp.where(j == 0, lead, rest)
        row = _aligned((t % (TILE // ROW_BLOCK)) * ROW_BLOCK, ROW_BLOCK)
        return (_dot_nt(qq_ref[pl.ds(q_row(t), ROW_BLOCK), :], k_ref[pl.ds(k_row(t), TILE), :])
                + tab_ref[table, pl.ds(row, ROW_BLOCK), :])

    def values(t, par):
        return _ones_right(v_ref[pl.ds(k_row(t), TILE), :])

    _flash_pipeline(n_chunks * blocks, scores, values, lambda t, par: q_row(t), scratch, A_PIPE_UNROLL)

    acc_ref = scratch[-1]
    o = acc_ref[:, 0:A_V_DIM] / acc_ref[:, A_V_DIM:2 * A_V_DIM]
    dl = dl_ref[...]
    lam = (jnp.exp(jnp.sum(dl[0:1] * dl[1:2], axis=-1, keepdims=True))
           - jnp.exp(jnp.sum(dl[2:3] * dl[3:4], axis=-1, keepdims=True)) + lam_init)
    w = o[0:TILE] - lam * o[TILE:2 * TILE]
    o_ref[...] = (_rms(w, gs_ref[...]) * (1.0 - lam_init)).astype(BF16)


def _attn_a(aq, ak, av, tab, dl, gs, lam_init):
    nb, lp, _ = aq.shape
    n_chunks = lp // TILE
    qspec = pl.BlockSpec((None, TILE, LANES), lambda b, h, t: (b, t, h))
    kvspec = pl.BlockSpec((None, lp, LANES), lambda b, h, t: (b, 0, h))
    return pl.pallas_call(
        functools.partial(_attn_a_kernel, lam_init=lam_init, n_chunks=n_chunks),
        grid=(nb, A_HEADS, n_chunks),
        in_specs=[qspec, kvspec, kvspec,
                  pl.BlockSpec((None,) + tab.shape[1:], lambda b, h, t: (h, 0, 0, 0)),
                  _resident((4, A_QK_DIM)), _resident((1, A_V_DIM))],
        out_specs=qspec,
        out_shape=jax.ShapeDtypeStruct((nb, lp, A_HEADS * A_V_DIM), BF16),
        scratch_shapes=[pltpu.VMEM((2 * TILE, LANES), BF16)] + _flash_scratch(),
        compiler_params=_params(3),
        name="attn_a",
    )(aq, ak, av, tab, dl, gs.reshape(1, A_V_DIM))


def _attn_c_kernel(q_ref, k_ref, v_ref, kmask_ref, o_ref, *scratch, n_chunks):
    blocks = 2 * TILE // ROW_BLOCK

    def q_row(t):
        return _aligned(((t % blocks) // 2) * ROW_BLOCK, ROW_BLOCK)

    def k_row(t):
        return _aligned((t // blocks) * TILE, TILE)

    def scores(t, par):
        cols = slice(par * LANES, (par + 1) * LANES)
        return (_dot_nt(q_ref[pl.ds(q_row(t), ROW_BLOCK), cols], k_ref[pl.ds(k_row(t), TILE), cols])
                + kmask_ref[_iselect(t < blocks, 1, 0)])

    def values(t, par):
        return _ones_right(v_ref[pl.ds(k_row(t), TILE), :])

    def stat_row(t, par):
        return _aligned(par * TILE + q_row(t), ROW_BLOCK)

    _flash_pipeline(n_chunks * blocks, scores, values, stat_row, scratch, C_PIPE_UNROLL)

    acc_ref = scratch[-1]
    o = acc_ref[:, 0:LANES] / acc_ref[:, LANES:2 * LANES]
    lane = lax.broadcasted_iota(jnp.int32, (TILE, LANES), 1)
    o_ref[...] = jnp.where(lane < C_V_DIM, o[0:TILE], o[TILE:2 * TILE]).astype(BF16)


def _attn_c(qc, kc, vc, kmask):
    nb, lp, _ = qc.shape
    n_chunks = lp // TILE
    return pl.pallas_call(
        functools.partial(_attn_c_kernel, n_chunks=n_chunks),
        grid=(nb, C_HEADS // 2, n_chunks),
        in_specs=[pl.BlockSpec((None, TILE, 2 * LANES), lambda b, p, t: (b, t, p)),
                  pl.BlockSpec((None, lp, 2 * LANES), lambda b, p, t: (b, 0, p)),
                  pl.BlockSpec((None, lp, LANES), lambda b, p, t: (b, 0, p)),
                  _resident(kmask.shape)],
        out_specs=pl.BlockSpec((None, TILE, LANES), lambda b, p, t: (b, t, p)),
        out_shape=jax.ShapeDtypeStruct((nb, lp, C_HEADS * C_V_DIM), BF16),
        scratch_shapes=_flash_scratch(),
        compiler_params=_params(3),
        name="attn_c",
    )(qc, kc, vc, kmask)


_B_ORDER = (0, 3, 1, 2)


def _attn_b_kernel(sink_ref, q_ref, k_ref, v_ref, vs_ref, tab_ref, o_ref, *, lp):
    t = pl.program_id(1)
    lane = lax.broadcasted_iota(jnp.int32, (BLOCK, LANES), 1)
    kk = lax.broadcasted_iota(jnp.int32, (1, 3 * BLOCK), 1)
    row = lax.broadcasted_iota(jnp.int32, (B_HEADS * BLOCK, 1), 0)
    sink = jnp.full((B_HEADS * BLOCK, 1), sink_ref[_B_ORDER[0]], F32)
    for n in range(1, B_HEADS):
        sink = jnp.where(row >= n * BLOCK, sink_ref[_B_ORDER[n]], sink)
    half = B_HEADS * BLOCK // 2
    for blk in range(TILE // BLOCK):
        gblk = t * (TILE // BLOCK) + blk
        row0 = pl.multiple_of(gblk * BLOCK, BLOCK)
        kw = k_ref[pl.ds(row0, 3 * BLOCK), :]
        vw = v_ref[pl.ds(row0, 3 * BLOCK), :]
        vsw = vs_ref[pl.ds(row0, 3 * BLOCK), :]
        kslot = (gblk - 1) * BLOCK + kk
        row_mask = jnp.where((kslot >= META_START) & (kslot < lp), 0.0, NEG_INF)
        rows = slice(blk * BLOCK, (blk + 1) * BLOCK)
        qs = jnp.concatenate([q_ref[rows, h * LANES:(h + 1) * LANES] for h in _B_ORDER], axis=0)
        s = _dot_nt(qs, kw) + tab_ref[...] + row_mask
        m = jnp.maximum(jnp.max(s, axis=-1, keepdims=True), sink)
        p = jnp.exp(s - m)
        denom = jnp.sum(p, axis=-1, keepdims=True) + jnp.exp(sink - m)
        pb = p.astype(BF16)
        o03 = _dot(pb[0:half], vw) / denom[0:half]
        o12 = _dot(pb[half:], vsw) / denom[half:]
        o_ref[rows, 0:LANES] = jnp.where(lane < B_HEAD_DIM, o03[0:BLOCK], o12[0:BLOCK]).astype(BF16)
        o_ref[rows, LANES:2 * LANES] = jnp.where(lane < B_HEAD_DIM, o12[BLOCK:], o03[BLOCK:]).astype(BF16)


def _attn_b(bq, bk_pad, bv_pad, bvs_pad, tab, sinks):
    nb, lp, _ = bq.shape
    kvspec = pl.BlockSpec((None, lp + 2 * BLOCK, LANES), lambda b, t: (b, 0, 0))
    return pl.pallas_call(
        functools.partial(_attn_b_kernel, lp=lp),
        grid=(nb, lp // TILE),
        in_specs=[pl.BlockSpec(memory_space=pltpu.SMEM),
                  pl.BlockSpec((None, TILE, B_HEADS * LANES), lambda b, t: (b, t, 0)),
                  kvspec, kvspec, kvspec, _resident(tab.shape)],
        out_specs=pl.BlockSpec((None, TILE, B_HEADS * B_HEAD_DIM), lambda b, t: (b, t, 0)),
        out_shape=jax.ShapeDtypeStruct((nb, lp, B_HEADS * B_HEAD_DIM), BF16),
        compiler_params=_params(2),
        name="attn_b",
    )(sinks, bq, bk_pad, bv_pad, bvs_pad, tab)


def _t5_bucket(rel):
    half = N_BUCKETS // 2
    max_exact = half // 2
    ret = jnp.where(rel > 0, half, 0)
    n = jnp.abs(rel)
    nf = jnp.maximum(n, 1).astype(F32)
    large = max_exact + (jnp.log(nf / max_exact) / math.log(MAX_DISTANCE / max_exact)
                         * (half - max_exact)).astype(jnp.int32)
    large = jnp.minimum(large, half - 1)
    return ret + jnp.where(n < max_exact, n, large)


def _rot_cols(w):
    half = w.shape[-1] // 2
    return jnp.concatenate([-w[..., half:], w[..., :half]], axis=-1)


def _bias_of_rel(table, rel):
    bucket = _t5_bucket(rel)[None]
    out = jnp.zeros((table.shape[1],) + rel.shape, F32)
    for b in range(N_BUCKETS):
        out = jnp.where(bucket == b, table[b].reshape((-1,) + (1,) * rel.ndim), out)
    return out


def _bias_tables(rel_bias, lp):
    rb = rel_bias.astype(F32)
    rela = (jnp.arange(3 * TILE) - TILE)[None, :] - jnp.arange(TILE)[:, None]
    wide = _bias_of_rel(rb[:, :A_HEADS], rela)
    left, diag, right = (wide[:, :, d * TILE:(d + 1) * TILE] for d in range(3))
    far_a = rb[:, :A_HEADS][_t5_bucket(jnp.array([-2 * TILE, 2 * TILE]))].T
    lead = jnp.where(jnp.arange(TILE) >= META_START, 0.0, NEG_INF).astype(F32).reshape(1, 1, TILE)
    far_left = jnp.broadcast_to(far_a[:, 0][:, None, None], left.shape)
    far_right = jnp.broadcast_to(far_a[:, 1][:, None, None], left.shape)
    tab_a = jnp.stack([left, diag, right, far_left, far_right, left + lead, diag + lead, far_left + lead], axis=1)
    kmask = jnp.concatenate([jnp.zeros_like(lead), lead], axis=0)
    relb = (jnp.arange(3 * BLOCK) - BLOCK)[None, :] - jnp.arange(BLOCK)[:, None]
    tab_b = jnp.where((jnp.abs(relb) <= WINDOW)[None], _bias_of_rel(rb[:, A_HEADS:], relb), NEG_INF)
    tab_b = jnp.concatenate([tab_b[h] for h in _B_ORDER], axis=0)
    slot = jnp.arange(lp)
    inv = ROPE_THETA ** (-jnp.arange(0, C_ROPE_DIM, 2, dtype=F32) / C_ROPE_DIM)
    ang = (slot - META_START).astype(F32)[:, None] * inv[None, :]
    ang = jnp.concatenate([ang, ang], axis=-1)
    ones = jnp.ones((lp, C_NOPE_DIM), F32)
    zeros_n = jnp.zeros((lp, C_NOPE_DIM), F32)
    zeros_p = jnp.zeros((lp, LANES - C_NOPE_DIM - C_ROPE_DIM), F32)
    cos_t = jnp.concatenate([ones, jnp.cos(ang), zeros_p], axis=-1)
    sin_t = jnp.concatenate([zeros_n, jnp.sin(ang), zeros_p], axis=-1)
    return tab_a, tab_b, kmask, cos_t, sin_t


def _layer_weights(w_in, w_uq, w_ukv, w_out):
    d = w_in.shape[0]
    sizes = (512, 512, 512, 256, 128, 128, 256, 128, 32)
    offs = [0]
    for s in sizes:
        offs.append(offs[-1] + s)
    aq, ak, av, bq, bk, bv, cq, ckv, kr = (w_in[:, offs[i]:offs[i + 1]] for i in range(9))
    zeros64 = jnp.zeros((d, B_HEAD_DIM), F32)
    bq_pad = []
    for h in range(B_HEADS):
        qh = bq[:, h * B_HEAD_DIM:(h + 1) * B_HEAD_DIM] * (B_HEAD_DIM ** -0.5)
        bq_pad += [qh, zeros64] if h // (B_HEADS // B_KV_HEADS) == 0 else [zeros64, qh]
    bvs = jnp.concatenate([bv[:, B_HEAD_DIM:], bv[:, :B_HEAD_DIM]], axis=1)
    pad_lo = jnp.zeros((d, C_NOPE_DIM), F32)
    pad_hi = jnp.zeros((d, LANES - C_NOPE_DIM - C_ROPE_DIM), F32)
    w_big = jnp.concatenate(
        [aq * (A_QK_DIM ** -0.5), ak, av] + bq_pad + [bk, bv, bvs, cq, ckv,
                                                      pad_lo, kr, pad_hi, pad_lo, _rot_cols(kr), pad_hi],
        axis=1).astype(BF16)
    uq = w_uq.reshape(C_Q_RANK, C_HEADS, C_NOPE_DIM + C_ROPE_DIM)
    uq_nope, uq_rope = uq[..., :C_NOPE_DIM], uq[..., C_NOPE_DIM:]
    zq = jnp.zeros((C_Q_RANK, C_HEADS, LANES - C_NOPE_DIM - C_ROPE_DIM), F32)
    wuqm = jnp.concatenate([uq_nope, uq_rope, zq], axis=-1).reshape(C_Q_RANK, C_HEADS * LANES).astype(BF16)
    wuqr = jnp.concatenate([jnp.zeros_like(uq_nope), _rot_cols(uq_rope), zq], axis=-1
                           ).reshape(C_Q_RANK, C_HEADS * LANES).astype(BF16)
    ukv = w_ukv.reshape(C_KV_RANK, C_HEADS, C_NOPE_DIM + C_V_DIM)
    wukvk = jnp.concatenate([ukv[..., :C_NOPE_DIM], jnp.zeros((C_KV_RANK, C_HEADS, LANES - C_NOPE_DIM), F32)],
                            axis=-1).reshape(C_KV_RANK, C_HEADS * LANES).astype(BF16)
    wukvv = ukv[..., C_NOPE_DIM:].reshape(C_KV_RANK, C_HEADS * C_V_DIM).astype(BF16)
    n_a = A_HEADS * A_V_DIM
    n_b = B_HEADS * B_HEAD_DIM
    wo = w_out.astype(BF16)
    return w_big, wuqm, wuqr, wukvk, wukvv, wo[:n_a], wo[n_a:n_a + n_b], wo[n_a + n_b:]


def _trunk(x, meta, rel_bias, g_ffn1, w_ffn1_gu, w_ffn1_down, g_mix, w_in, diff_lambda, g_subln, sinks, g_cq,
           g_ckv, w_uq, w_ukv, w_out, g_ffn2, w_ffn2_gu, w_ffn2_down, g_final):
    nb, seq, d = x.shape
    lp = BLOCK + seq
    assert lp % TILE == 0 and d == D_MODEL
    lead = jnp.concatenate([jnp.zeros((META_START, d), x.dtype), meta.astype(x.dtype)], axis=0)
    h = jnp.concatenate([jnp.broadcast_to(lead[None], (nb, BLOCK, d)), x], axis=1)
    tab_a, tab_b, kmask, cos_t, sin_t = _bias_tables(rel_bias, lp)
    pad = ((0, 0), (BLOCK, BLOCK), (0, 0))
    for l in range(DEPTH):
        w_big, wuqm, wuqr, wukvk, wukvv, wo_a, wo_b, wo_c = _layer_weights(w_in[l], w_uq[l], w_ukv[l], w_out[l])
        h = _ffn(h, g_ffn1[l], w_ffn1_gu[l].astype(BF16), w_ffn1_down[l].astype(BF16))
        aq, ak, av, bq, bk, bv, bvs, qc, kc, vc = _inproj(h, g_mix[l], w_big, g_cq[l], g_ckv[l], wuqm, wuqr,
                                                           wukvk, wukvv, cos_t, sin_t)
        lam_init = 0.8 - 0.6 * math.exp(-0.3 * l)
        oa = _attn_a(aq, ak, av, tab_a, diff_lambda[l].astype(F32), g_subln[l], lam_init)
        ob = _attn_b(bq, jnp.pad(bk, pad), jnp.pad(bv, pad), jnp.pad(bvs, pad), tab_b, sinks[l].astype(F32))
        oc = _attn_c(qc, kc, vc, kmask)
        h = _ffn(h, g_ffn2[l], w_ffn2_gu[l].astype(BF16), w_ffn2_down[l].astype(BF16),
                 mix=(oa, ob, oc, wo_a, wo_b, wo_c), g_final=g_final if l == DEPTH - 1 else None)
    return h[:, BLOCK:]


def kernel(x_prompt, x_sample, meta, rel_bias, g_ffn1, w_ffn1_gu, w_ffn1_down, g_mix, w_in, diff_lambda, g_subln,
           sinks, g_cq, g_ckv, w_uq, w_ukv, w_out, g_ffn2, w_ffn2_gu, w_ffn2_down, g_final):
    n_prompt = x_prompt.shape[0]
    x = jnp.concatenate([x_prompt, x_sample], axis=0)
    y = _trunk(x, meta, rel_bias, g_ffn1, w_ffn1_gu, w_ffn1_down, g_mix, w_in, diff_lambda, g_subln, sinks, g_cq,
               g_ckv, w_uq, w_ukv, w_out, g_ffn2, w_ffn2_gu, w_ffn2_down, g_final)
    return (y[:n_prompt], y[n_prompt:])
```

```python
import functools
import math

import jax
import jax.numpy as jnp
from jax import lax
from jax.experimental import pallas as pl
from jax.experimental.pallas import tpu as pltpu

D_MODEL = 1024
DEPTH = 2
N_META = 16
BLOCK = 128
META_START = BLOCK - N_META
A_HEADS = 4
A_QK_DIM = 64
A_V_DIM = 2 * A_QK_DIM
B_HEADS = 4
B_KV_HEADS = 2
B_HEAD_DIM = 64
WINDOW = 128
C_HEADS = 4
C_Q_RANK = 256
C_KV_RANK = 128
C_NOPE_DIM = 64
C_ROPE_DIM = 32
C_V_DIM = 64
ROPE_THETA = 10000.0
N_BUCKETS = 32
MAX_DISTANCE = 128
D_FF = 2816
EPS = 1e-6

LANES = 128
TILE = 5 * BLOCK
FF_CHUNK = D_FF // 2
ROW_BLOCK = TILE // 4
A_PIPE_UNROLL = 26
C_PIPE_UNROLL = 26
VMEM_LIMIT = 56 * 1024 * 1024

F32 = jnp.float32
BF16 = jnp.bfloat16
NEG_INF = float("-inf")

_W_AQ, _W_AK, _W_AV, _W_BQ = 0, 512, 1024, 1536
_W_BK, _W_BV, _W_BVS = 2048, 2176, 2304
_W_CQ, _W_CKV, _W_KR, _W_KRR = 2432, 2688, 2816, 2944
_W_TOTAL = 3072


def _params(n_axes):
    return pltpu.CompilerParams(dimension_semantics=("arbitrary",) * n_axes, vmem_limit_bytes=VMEM_LIMIT)


def _resident(shape):
    return pl.BlockSpec(shape, lambda *_: (0,) * len(shape), pipeline_mode=pl.Buffered(1))


def _rms(x, g):
    return x * lax.rsqrt(jnp.mean(x * x, axis=-1, keepdims=True) + EPS) * g


def _dot(a, b):
    return jnp.dot(a, b, preferred_element_type=F32)


def _dot_nt(a, b):
    return lax.dot_general(a, b, (((1,), (1,)), ((), ())), preferred_element_type=F32)


def _ffn_kernel(x_ref, *refs, mixed, final_norm):
    refs = list(refs)
    x = x_ref[...]
    if mixed:
        oa_ref, ob_ref, oc_ref, wa_ref, wb_ref, wc_ref = refs[:6]
        del refs[:6]
        x = x + _dot(oa_ref[...], wa_ref[...]) + _dot(ob_ref[...], wb_ref[...]) + _dot(oc_ref[...], wc_ref[...])
    g_ref, wgu_ref, wd_ref = refs[:3]
    gf_ref = refs[3] if final_norm else None
    o_ref = refs[-1]
    xn = _rms(x, g_ref[...]).astype(BF16)
    acc = None
    for lo in range(0, D_FF, FF_CHUNK):
        g = _dot(xn, wgu_ref[:, lo:lo + FF_CHUNK])
        u = _dot(xn, wgu_ref[:, D_FF + lo:D_FF + lo + FF_CHUNK])
        a = (g * jax.nn.sigmoid(g) * u).astype(BF16)
        d = _dot(a, wd_ref[lo:lo + FF_CHUNK, :])
        acc = d if acc is None else acc + d
    y = x + 0.5 * acc
    if final_norm:
        y = _rms(y, gf_ref[...])
    o_ref[...] = y


def _ffn(h, g, wgu, wd, mix=None, g_final=None):
    nb, lp, d = h.shape

    def rows(n):
        return pl.BlockSpec((None, TILE, n), lambda b, t: (b, t, 0))

    tile = rows(d)
    in_specs = [tile]
    args = [h]
    if mix is not None:
        in_specs += [rows(o.shape[-1]) for o in mix[:3]] + [_resident(w.shape) for w in mix[3:]]
        args += list(mix)
    in_specs += [_resident((1, d)), _resident(wgu.shape), _resident(wd.shape)]
    args += [g.reshape(1, d), wgu, wd]
    if g_final is not None:
        in_specs.append(_resident((1, d)))
        args.append(g_final.reshape(1, d))
    return pl.pallas_call(
        functools.partial(_ffn_kernel, mixed=mix is not None, final_norm=g_final is not None),
        grid=(nb, lp // TILE),
        in_specs=in_specs,
        out_specs=tile,
        out_shape=jax.ShapeDtypeStruct(h.shape, F32),
        compiler_params=_params(2),
        name="ffn",
    )(*args)


def _inproj_kernel(h_ref, g_ref, w_ref, gcq_ref, gckv_ref, wuqm_ref, wuqr_ref, wukvk_ref, wukvv_ref,
                   cos_ref, sin_ref,
                   aq_ref, ak_ref, av_ref, bq_ref, bk_ref, bv_ref, bvs_ref, qc_ref, kc_ref, vc_ref, *, c_scale):
    xn = _rms(h_ref[...], g_ref[...]).astype(BF16)

    def proj(lo, n):
        return _dot(xn, w_ref[:, lo:lo + n])

    aq_ref[...] = proj(_W_AQ, 512).astype(BF16)
    ak_ref[...] = proj(_W_AK, 512).astype(BF16)
    av_ref[...] = proj(_W_AV, 512).astype(BF16)
    bq_ref[...] = proj(_W_BQ, 512).astype(BF16)
    bk_ref[...] = proj(_W_BK, 128).astype(BF16)
    bv_ref[...] = proj(_W_BV, 128).astype(BF16)
    bvs_ref[...] = proj(_W_BVS, 128).astype(BF16)

    cos = cos_ref[...]
    sin = sin_ref[...]
    cos4 = jnp.concatenate([cos] * C_HEADS, axis=1)
    sin4 = jnp.concatenate([sin] * C_HEADS, axis=1)
    cqn = _rms(proj(_W_CQ, C_Q_RANK), gcq_ref[...]).astype(BF16)
    q = _dot(cqn, wuqm_ref[...]) * cos4 + _dot(cqn, wuqr_ref[...]) * sin4
    qc_ref[...] = (q * c_scale).astype(BF16)

    ckvn = _rms(proj(_W_CKV, C_KV_RANK), gckv_ref[...]).astype(BF16)
    k_rope = proj(_W_KR, LANES) * cos + proj(_W_KRR, LANES) * sin
    kc_ref[...] = (_dot(ckvn, wukvk_ref[...]) + jnp.concatenate([k_rope] * C_HEADS, axis=1)).astype(BF16)
    vc_ref[...] = _dot(ckvn, wukvv_ref[...]).astype(BF16)


def _inproj(h, g, w_big, gcq, gckv, wuqm, wuqr, wukvk, wukvv, cos_t, sin_t):
    nb, lp, d = h.shape

    def tile(n):
        return pl.BlockSpec((None, TILE, n), lambda b, t: (b, t, 0))

    pos = pl.BlockSpec((TILE, LANES), lambda b, t: (t, 0))
    widths = (512, 512, 512, 512, 128, 128, 128, 512, 512, 256)
    return pl.pallas_call(
        functools.partial(_inproj_kernel, c_scale=(C_NOPE_DIM + C_ROPE_DIM) ** -0.5),
        grid=(nb, lp // TILE),
        in_specs=[tile(d), _resident((1, d)), _resident(w_big.shape), _resident((1, C_Q_RANK)),
                  _resident((1, C_KV_RANK)), _resident(wuqm.shape), _resident(wuqr.shape),
                  _resident(wukvk.shape), _resident(wukvv.shape), pos, pos],
        out_specs=[tile(n) for n in widths],
        out_shape=[jax.ShapeDtypeStruct((nb, lp, n), BF16) for n in widths],
        compiler_params=_params(2),
        name="inproj",
    )(h, g.reshape(1, d), w_big, gcq.reshape(1, -1), gckv.reshape(1, -1), wuqm, wuqr, wukvk, wukvv, cos_t, sin_t)


def _static(x):
    return isinstance(x, (int, bool))


def _aligned(x, m):
    return x if _static(x) else pl.multiple_of(x, m)


def _imin(a, b):
    return min(a, b) if _static(a) else jnp.minimum(a, b)


def _imax(a, b):
    return max(a, b) if _static(a) else jnp.maximum(a, b)


def _iselect(c, a, b):
    return (a if c else b) if _static(c) else jnp.where(c, a, b)


def _flash_scratch():
    return ([pltpu.VMEM((ROW_BLOCK, TILE), F32)] * 2 + [pltpu.VMEM((ROW_BLOCK, LANES), F32)] * 2
            + [pltpu.VMEM((ROW_BLOCK, TILE), BF16)] * 2 + [pltpu.VMEM((ROW_BLOCK, LANES), F32)] * 2
            + [pltpu.VMEM((2 * TILE, LANES), F32), pltpu.VMEM((2 * TILE, 2 * LANES), F32)])


def _flash_pipeline(n_tiles, scores, values, stat_row, scratch, max_unroll):
    s0, s1, x0, x1, p0, p1, a0, a1, m_ref, acc_ref = scratch
    s_bufs, x_bufs, p_bufs, a_bufs = (s0, s1), (x0, x1), (p0, p1), (a0, a1)
    unroll = max(k for k in range(2, max_unroll + 1, 2) if n_tiles % k == 0)
    m_ref[...] = jnp.full(m_ref.shape, NEG_INF, F32)
    acc_ref[...] = jnp.zeros(acc_ref.shape, F32)
    p1[...] = jnp.zeros(p1.shape, BF16)
    a1[...] = jnp.ones(a1.shape, F32)

    def issue(t, par):
        s = scores(t, par)
        s_bufs[par][...] = s
        x_bufs[par][...] = jnp.broadcast_to(jnp.max(s, axis=-1, keepdims=True), (ROW_BLOCK, LANES))

    def accumulate(t, par):
        rows = pl.ds(stat_row(t, par), ROW_BLOCK)
        alpha = a_bufs[par][...]
        acc_ref[rows, :] = (jnp.concatenate([alpha, alpha], axis=1) * acc_ref[rows, :]
                            + _dot(p_bufs[par][...], values(t, par)))

    def step(t, par):
        issue(_imin(t + 1, n_tiles - 1), 1 - par)
        rows = pl.ds(stat_row(t, par), ROW_BLOCK)
        m_old = m_ref[rows, :]
        m_new = jnp.maximum(m_old, x_bufs[par][...])
        p_bufs[par][...] = jnp.exp((s_bufs[par][...] - jnp.concatenate([m_new] * (TILE // LANES), axis=1)
                                    ).astype(BF16))
        a_bufs[par][...] = jnp.exp(m_old - m_new)
        m_ref[rows, :] = m_new
        accumulate(_imax(t - 1, 0), 1 - par)

    issue(0, 0)

    def body(u, carry):
        for k in range(unroll):
            step(unroll * u + k, k % 2)
        return carry

    if unroll == n_tiles:
        for k in range(n_tiles):
            step(k, k % 2)
    else:
        lax.fori_loop(0, n_tiles // unroll, body, 0)
    accumulate(n_tiles - 1, 1)


(_TAB_LEFT, _TAB_DIAG, _TAB_RIGHT, _TAB_FAR_LEFT, _TAB_FAR_RIGHT,
 _TAB_LEFT_LEAD, _TAB_DIAG_LEAD, _TAB_FAR_LEAD) = range(8)


def _ones_right(v):
    return jnp.concatenate([v, jnp.ones(v.shape, v.dtype)], axis=1)


def _attn_a_kernel(q_ref, k_ref, v_ref, tab_ref, dl_ref, gs_ref, o_ref, qq_ref, *scratch, lam_init, n_chunks):
    i = pl.program_id(2)
    q = q_ref[...]
    lane = lax.broadcasted_iota(jnp.int32, q.shape, 1)
    zero = jnp.zeros_like(q)
    qq_ref[0:TILE, :] = jnp.where(lane < A_QK_DIM, q, zero)
    qq_ref[TILE:2 * TILE, :] = jnp.where(lane >= A_QK_DIM, q, zero)
    blocks = 2 * TILE // ROW_BLOCK

    def q_row(t):
        return _aligned((t % blocks) * ROW_BLOCK, ROW_BLOCK)

    def k_row(t):
        return _aligned((t // blocks) * TILE, TILE)

    def scores(t, par):
        j = t // blocks
        lead = jnp.where(i == 0, _TAB_DIAG_LEAD, jnp.where(i == 1, _TAB_LEFT_LEAD, _TAB_FAR_LEAD))
        rest = jnp.where(j < i - 1, _TAB_FAR_LEFT, jnp.where(j > i + 1, _TAB_FAR_RIGHT, j - i + 1))
        table = jnp.where(j == 0, lead, rest)
        row = _aligned((t % (TILE // ROW_BLOCK)) * ROW_BLOCK, ROW_BLOCK)
        return (_dot_nt(qq_ref[pl.ds(q_row(t), ROW_BLOCK), :], k_ref[pl.ds(k_row(t), TILE), :])
                + tab_ref[table, pl.ds(row, ROW_BLOCK), :])

    def values(t, par):
        return _ones_right(v_ref[pl.ds(k_row(t), TILE), :])

    _flash_pipeline(n_chunks * blocks, scores, values, lambda t, par: q_row(t), scratch, A_PIPE_UNROLL)

    acc_ref = scratch[-1]
    o = acc_ref[:, 0:A_V_DIM] / acc_ref[:, A_V_DIM:2 * A_V_DIM]
    dl = dl_ref[...]
    lam = (jnp.exp(jnp.sum(dl[0:1] * dl[1:2], axis=-1, keepdims=True))
           - jnp.exp(jnp.sum(dl[2:3] * dl[3:4], axis=-1, keepdims=True)) + lam_init)
    w = o[0:TILE] - lam * o[TILE:2 * TILE]
    o_ref[...] = (_rms(w, gs_ref[...]) * (1.0 - lam_init)).astype(BF16)


def _attn_a(aq, ak, av, tab, dl, gs, lam_init):
    nb, lp, _ = aq.shape
    n_chunks = lp // TILE
    qspec = pl.BlockSpec((None, TILE, LANES), lambda b, h, t: (b, t, h))
    kvspec = pl.BlockSpec((None, lp, LANES), lambda b, h, t: (b, 0, h))
    return pl.pallas_call(
        functools.partial(_attn_a_kernel, lam_init=lam_init, n_chunks=n_chunks),
        grid=(nb, A_HEADS, n_chunks),
        in_specs=[qspec, kvspec, kvspec,
                  pl.BlockSpec((None,) + tab.shape[1:], lambda b, h, t: (h, 0, 0, 0)),
                  _resident((4, A_QK_DIM)), _resident((1, A_V_DIM))],
        out_specs=qspec,
        out_shape=jax.ShapeDtypeStruct((nb, lp, A_HEADS * A_V_DIM), BF16),
        scratch_shapes=[pltpu.VMEM((2 * TILE, LANES), BF16)] + _flash_scratch(),
        compiler_params=_params(3),
        name="attn_a",
    )(aq, ak, av, tab, dl, gs.reshape(1, A_V_DIM))


def _attn_c_kernel(q_ref, k_ref, v_ref, kmask_ref, o_ref, *scratch, n_chunks):
    blocks = 2 * TILE // ROW_BLOCK

    def q_row(t):
        return _aligned(((t % blocks) // 2) * ROW_BLOCK, ROW_BLOCK)

    def k_row(t):
        return _aligned((t // blocks) * TILE, TILE)

    def scores(t, par):
        cols = slice(par * LANES, (par + 1) * LANES)
        return (_dot_nt(q_ref[pl.ds(q_row(t), ROW_BLOCK), cols], k_ref[pl.ds(k_row(t), TILE), cols])
                + kmask_ref[_iselect(t < blocks, 1, 0)])

    def values(t, par):
        return _ones_right(v_ref[pl.ds(k_row(t), TILE), :])

    def stat_row(t, par):
        return _aligned(par * TILE + q_row(t), ROW_BLOCK)

    _flash_pipeline(n_chunks * blocks, scores, values, stat_row, scratch, C_PIPE_UNROLL)

    acc_ref = scratch[-1]
    o = acc_ref[:, 0:LANES] / acc_ref[:, LANES:2 * LANES]
    lane = lax.broadcasted_iota(jnp.int32, (TILE, LANES), 1)
    o_ref[...] = jnp.where(lane < C_V_DIM, o[0:TILE], o[TILE:2 * TILE]).astype(BF16)


def _attn_c(qc, kc, vc, kmask):
    nb, lp, _ = qc.shape
    n_chunks = lp // TILE
    return pl.pallas_call(
        functools.partial(_attn_c_kernel, n_chunks=n_chunks),
        grid=(nb, C_HEADS // 2, n_chunks),
        in_specs=[pl.BlockSpec((None, TILE, 2 * LANES), lambda b, p, t: (b, t, p)),
                  pl.BlockSpec((None, lp, 2 * LANES), lambda b, p, t: (b, 0, p)),
                  pl.BlockSpec((None, lp, LANES), lambda b, p, t: (b, 0, p)),
                  _resident(kmask.shape)],
        out_specs=pl.BlockSpec((None, TILE, LANES), lambda b, p, t: (b, t, p)),
        out_shape=jax.ShapeDtypeStruct((nb, lp, C_HEADS * C_V_DIM), BF16),
        scratch_shapes=_flash_scratch(),
        compiler_params=_params(3),
        name="attn_c",
    )(qc, kc, vc, kmask)


_B_ORDER = (0, 3, 1, 2)


def _attn_b_kernel(sink_ref, q_ref, k_ref, v_ref, vs_ref, tab_ref, o_ref, *, lp):
    t = pl.program_id(1)
    lane = lax.broadcasted_iota(jnp.int32, (BLOCK, LANES), 1)
    kk = lax.broadcasted_iota(jnp.int32, (1, 3 * BLOCK), 1)
    row = lax.broadcasted_iota(jnp.int32, (B_HEADS * BLOCK, 1), 0)
    sink = jnp.full((B_HEADS * BLOCK, 1), sink_ref[_B_ORDER[0]], F32)
    for n in range(1, B_HEADS):
        sink = jnp.where(row >= n * BLOCK, sink_ref[_B_ORDER[n]], sink)
    half = B_HEADS * BLOCK // 2
    for blk in range(TILE // BLOCK):
        gblk = t * (TILE // BLOCK) + blk
        row0 = pl.multiple_of(gblk * BLOCK, BLOCK)
        kw = k_ref[pl.ds(row0, 3 * BLOCK), :]
        vw = v_ref[pl.ds(row0, 3 * BLOCK), :]
        vsw = vs_ref[pl.ds(row0, 3 * BLOCK), :]
        kslot = (gblk - 1) * BLOCK + kk
        row_mask = jnp.where((kslot >= META_START) & (kslot < lp), 0.0, NEG_INF)
        rows = slice(blk * BLOCK, (blk + 1) * BLOCK)
        qs = jnp.concatenate([q_ref[rows, h * LANES:(h + 1) * LANES] for h in _B_ORDER], axis=0)
        s = _dot_nt(qs, kw) + tab_ref[...] + row_mask
        m = jnp.maximum(jnp.max(s, axis=-1, keepdims=True), sink)
        p = jnp.exp(s - m)
        denom = jnp.sum(p, axis=-1, keepdims=True) + jnp.exp(sink - m)
        pb = p.astype(BF16)
        o03 = _dot(pb[0:half], vw) / denom[0:half]
        o12 = _dot(pb[half:], vsw) / denom[half:]
        o_ref[rows, 0:LANES] = jnp.where(lane < B_HEAD_DIM, o03[0:BLOCK], o12[0:BLOCK]).astype(BF16)
        o_ref[rows, LANES:2 * LANES] = jnp.where(lane < B_HEAD_DIM, o12[BLOCK:], o03[BLOCK:]).astype(BF16)


def _attn_b(bq, bk_pad, bv_pad, bvs_pad, tab, sinks):
    nb, lp, _ = bq.shape
    kvspec = pl.BlockSpec((None, lp + 2 * BLOCK, LANES), lambda b, t: (b, 0, 0))
    return pl.pallas_call(
        functools.partial(_attn_b_kernel, lp=lp),
        grid=(nb, lp // TILE),
        in_specs=[pl.BlockSpec(memory_space=pltpu.SMEM),
                  pl.BlockSpec((None, TILE, B_HEADS * LANES), lambda b, t: (b, t, 0)),
                  kvspec, kvspec, kvspec, _resident(tab.shape)],
        out_specs=pl.BlockSpec((None, TILE, B_HEADS * B_HEAD_DIM), lambda b, t: (b, t, 0)),
        out_shape=jax.ShapeDtypeStruct((nb, lp, B_HEADS * B_HEAD_DIM), BF16),
        compiler_params=_params(2),
        name="attn_b",
    )(sinks, bq, bk_pad, bv_pad, bvs_pad, tab)


def _t5_bucket(rel):
    half = N_BUCKETS // 2
    max_exact = half // 2
    ret = jnp.where(rel > 0, half, 0)
    n = jnp.abs(rel)
    nf = jnp.maximum(n, 1).astype(F32)
    large = max_exact + (jnp.log(nf / max_exact) / math.log(MAX_DISTANCE / max_exact)
                         * (half - max_exact)).astype(jnp.int32)
    large = jnp.minimum(large, half - 1)
    return ret + jnp.where(n < max_exact, n, large)


def _rot_cols(w):
    half = w.shape[-1] // 2
    return jnp.concatenate([-w[..., half:], w[..., :half]], axis=-1)


def _bias_of_rel(table, rel):
    bucket = _t5_bucket(rel)[None]
    out = jnp.zeros((table.shape[1],) + rel.shape, F32)
    for b in range(N_BUCKETS):
        out = jnp.where(bucket == b, table[b].reshape((-1,) + (1,) * rel.ndim), out)
    return out


def _bias_tables(rel_bias, lp):
    rb = rel_bias.astype(F32)
    rela = (jnp.arange(3 * TILE) - TILE)[None, :] - jnp.arange(TILE)[:, None]
    wide = _bias_of_rel(rb[:, :A_HEADS], rela)
    left, diag, right = (wide[:, :, d * TILE:(d + 1) * TILE] for d in range(3))
    far_a = rb[:, :A_HEADS][_t5_bucket(jnp.array([-2 * TILE, 2 * TILE]))].T
    lead = jnp.where(jnp.arange(TILE) >= META_START, 0.0, NEG_INF).astype(F32).reshape(1, 1, TILE)
    far_left = jnp.broadcast_to(far_a[:, 0][:, None, None], left.shape)
    far_right = jnp.broadcast_to(far_a[:, 1][:, None, None], left.shape)
    tab_a = jnp.stack([left, diag, right, far_left, far_right, left + lead, diag + lead, far_left + lead], axis=1)
    kmask = jnp.concatenate([jnp.zeros_like(lead), lead], axis=0)
    relb = (jnp.arange(3 * BLOCK) - BLOCK)[None, :] - jnp.arange(BLOCK)[:, None]
    tab_b = jnp.where((jnp.abs(relb) <= WINDOW)[None], _bias_of_rel(rb[:, A_HEADS:], relb), NEG_INF)
    tab_b = jnp.concatenate([tab_b[h] for h in _B_ORDER], axis=0)
    slot = jnp.arange(lp)
    inv = ROPE_THETA ** (-jnp.arange(0, C_ROPE_DIM, 2, dtype=F32) / C_ROPE_DIM)
    ang = (slot - META_START).astype(F32)[:, None] * inv[None, :]
    ang = jnp.concatenate([ang, ang], axis=-1)
    ones = jnp.ones((lp, C_NOPE_DIM), F32)
    zeros_n = jnp.zeros((lp, C_NOPE_DIM), F32)
    zeros_p = jnp.zeros((lp, LANES - C_NOPE_DIM - C_ROPE_DIM), F32)
    cos_t = jnp.concatenate([ones, jnp.cos(ang), zeros_p], axis=-1)
    sin_t = jnp.concatenate([zeros_n, jnp.sin(ang), zeros_p], axis=-1)
    return tab_a, tab_b, kmask, cos_t, sin_t


def _layer_weights(w_in, w_uq, w_ukv, w_out):
    d = w_in.shape[0]
    sizes = (512, 512, 512, 256, 128, 128, 256, 128, 32)
    offs = [0]
    for s in sizes:
        offs.append(offs[-1] + s)
    aq, ak, av, bq, bk, bv, cq, ckv, kr = (w_in[:, offs[i]:offs[i + 1]] for i in range(9))
    zeros64 = jnp.zeros((d, B_HEAD_DIM), F32)
    bq_pad = []
    for h in range(B_HEADS):
        qh = bq[:, h * B_HEAD_DIM:(h + 1) * B_HEAD_DIM] * (B_HEAD_DIM ** -0.5)
        bq_pad += [qh, zeros64] if h // (B_HEADS // B_KV_HEADS) == 0 else [zeros64, qh]
    bvs = jnp.concatenate([bv[:, B_HEAD_DIM:], bv[:, :B_HEAD_DIM]], axis=1)
    pad_lo = jnp.zeros((d, C_NOPE_DIM), F32)
    pad_hi = jnp.zeros((d, LANES - C_NOPE_DIM - C_ROPE_DIM), F32)
    w_big = jnp.concatenate(
        [aq * (A_QK_DIM ** -0.5), ak, av] + bq_pad + [bk, bv, bvs, cq, ckv,
                                                      pad_lo, kr, pad_hi, pad_lo, _rot_cols(kr), pad_hi],
        axis=1).astype(BF16)
    uq = w_uq.reshape(C_Q_RANK, C_HEADS, C_NOPE_DIM + C_ROPE_DIM)
    uq_nope, uq_rope = uq[..., :C_NOPE_DIM], uq[..., C_NOPE_DIM:]
    zq = jnp.zeros((C_Q_RANK, C_HEADS, LANES - C_NOPE_DIM - C_ROPE_DIM), F32)
    wuqm = jnp.concatenate([uq_nope, uq_rope, zq], axis=-1).reshape(C_Q_RANK, C_HEADS * LANES).astype(BF16)
    wuqr = jnp.concatenate([jnp.zeros_like(uq_nope), _rot_cols(uq_rope), zq], axis=-1
                           ).reshape(C_Q_RANK, C_HEADS * LANES).astype(BF16)
    ukv = w_ukv.reshape(C_KV_RANK, C_HEADS, C_NOPE_DIM + C_V_DIM)
    wukvk = jnp.concatenate([ukv[..., :C_NOPE_DIM], jnp.zeros((C_KV_RANK, C_HEADS, LANES - C_NOPE_DIM), F32)],
                            axis=-1).reshape(C_KV_RANK, C_HEADS * LANES).astype(BF16)
    wukvv = ukv[..., C_NOPE_DIM:].reshape(C_KV_RANK, C_HEADS * C_V_DIM).astype(BF16)
    n_a = A_HEADS * A_V_DIM
    n_b = B_HEADS * B_HEAD_DIM
    wo = w_out.astype(BF16)
    return w_big, wuqm, wuqr, wukvk, wukvv, wo[:n_a], wo[n_a:n_a + n_b], wo[n_a + n_b:]


def _trunk(x, meta, rel_bias, g_ffn1, w_ffn1_gu, w_ffn1_down, g_mix, w_in, diff_lambda, g_subln, sinks, g_cq,
           g_ckv, w_uq, w_ukv, w_out, g_ffn2, w_ffn2_gu, w_ffn2_down, g_final):
    nb, seq, d = x.shape
    lp = BLOCK + seq
    assert lp % TILE == 0 and d == D_MODEL
    lead = jnp.concatenate([jnp.zeros((META_START, d), x.dtype), meta.astype(x.dtype)], axis=0)
    h = jnp.concatenate([jnp.broadcast_to(lead[None], (nb, BLOCK, d)), x], axis=1)
    tab_a, tab_b, kmask, cos_t, sin_t = _bias_tables(rel_bias, lp)
    pad = ((0, 0), (BLOCK, BLOCK), (0, 0))
    for l in range(DEPTH):
        w_big, wuqm, wuqr, wukvk, wukvv, wo_a, wo_b, wo_c = _layer_weights(w_in[l], w_uq[l], w_ukv[l], w_out[l])
        h = _ffn(h, g_ffn1[l], w_ffn1_gu[l].astype(BF16), w_ffn1_down[l].astype(BF16))
        aq, ak, av, bq, bk, bv, bvs, qc, kc, vc = _inproj(h, g_mix[l], w_big, g_cq[l], g_ckv[l], wuqm, wuqr,
                                                           wukvk, wukvv, cos_t, sin_t)
        lam_init = 0.8 - 0.6 * math.exp(-0.3 * l)
        oa = _attn_a(aq, ak, av, tab_a, diff_lambda[l].astype(F32), g_subln[l], lam_init)
        ob = _attn_b(bq, jnp.pad(bk, pad), jnp.pad(bv, pad), jnp.pad(bvs, pad), tab_b, sinks[l].astype(F32))
        oc = _attn_c(qc, kc, vc, kmask)
        h = _ffn(h, g_ffn2[l], w_ffn2_gu[l].astype(BF16), w_ffn2_down[l].astype(BF16),
                 mix=(oa, ob, oc, wo_a, wo_b, wo_c), g_final=g_final if l == DEPTH - 1 else None)
    return h[:, BLOCK:]


def kernel(x_prompt, x_sample, meta, rel_bias, g_ffn1, w_ffn1_gu, w_ffn1_down, g_mix, w_in, diff_lambda, g_subln,
           sinks, g_cq, g_ckv, w_uq, w_ukv, w_out, g_ffn2, w_ffn2_gu, w_ffn2_down, g_final):
    n_prompt = x_prompt.shape[0]
    x = jnp.concatenate([x_prompt, x_sample], axis=0)
    y = _trunk(x, meta, rel_bias, g_ffn1, w_ffn1_gu, w_ffn1_down, g_mix, w_in, diff_lambda, g_subln, sinks, g_cq,
               g_ckv, w_uq, w_ukv, w_out, g_ffn2, w_ffn2_gu, w_ffn2_down, g_final)
    return (y[:n_prompt], y[n_prompt:])
```

```python
import functools
import math

import jax
import jax.numpy as jnp
from jax import lax
from jax.experimental import pallas as pl
from jax.experimental.pallas import tpu as pltpu

D_MODEL = 1024
DEPTH = 2
N_META = 16
BLOCK = 128
META_START = BLOCK - N_META
A_HEADS = 4
A_QK_DIM = 64
A_V_DIM = 2 * A_QK_DIM
B_HEADS = 4
B_KV_HEADS = 2
B_HEAD_DIM = 64
WINDOW = 128
C_HEADS = 4
C_Q_RANK = 256
C_KV_RANK = 128
C_NOPE_DIM = 64
C_ROPE_DIM = 32
C_V_DIM = 64
ROPE_THETA = 10000.0
N_BUCKETS = 32
MAX_DISTANCE = 128
D_FF = 2816
EPS = 1e-6

LANES = 128
TILE = 5 * BLOCK
MXU_COLS = 256
FF_SPLITS = (0, 6 * MXU_COLS, D_FF)
ROW_BLOCK = TILE // 4
A_PIPE_UNROLL = 26
C_PIPE_UNROLL = 26
VMEM_LIMIT = 56 * 1024 * 1024

F32 = jnp.float32
BF16 = jnp.bfloat16
NEG_INF = float("-inf")

_W_AQ, _W_AK, _W_AV, _W_BQ = 0, 512, 1024, 1536
_W_BK, _W_BV, _W_BVS = 2048, 2176, 2304
_W_CQ, _W_CKV, _W_KR, _W_KRR = 2432, 2688, 2816, 2944
_W_TOTAL = 3072


def _params(n_axes):
    return pltpu.CompilerParams(dimension_semantics=("arbitrary",) * n_axes, vmem_limit_bytes=VMEM_LIMIT)


def _resident(shape):
    return pl.BlockSpec(shape, lambda *_: (0,) * len(shape), pipeline_mode=pl.Buffered(1))


def _rms(x, g):
    return x * lax.rsqrt(jnp.mean(x * x, axis=-1, keepdims=True) + EPS) * g


def _dot(a, b):
    return jnp.dot(a, b, preferred_element_type=F32)


def _dot_nt(a, b):
    return lax.dot_general(a, b, (((1,), (1,)), ((), ())), preferred_element_type=F32)


def _ffn_kernel(x_ref, *refs, mixed, final_norm):
    refs = list(refs)
    x = x_ref[...]
    if mixed:
        oa_ref, ob_ref, oc_ref, wa_ref, wb_ref, wc_ref = refs[:6]
        del refs[:6]
        x = x + _dot(oa_ref[...], wa_ref[...]) + _dot(ob_ref[...], wb_ref[...]) + _dot(oc_ref[...], wc_ref[...])
    g_ref, wgu_ref, wd_ref = refs[:3]
    gf_ref = refs[3] if final_norm else None
    o_ref = refs[-1]
    xn = _rms(x, g_ref[...]).astype(BF16)
    acc = None
    for lo, hi in zip(FF_SPLITS[:-1], FF_SPLITS[1:]):
        g = _dot(xn, wgu_ref[:, lo:hi])
        u = _dot(xn, wgu_ref[:, D_FF + lo:D_FF + hi])
        a = (g * jax.nn.sigmoid(g) * u).astype(BF16)
        d = _dot(a, wd_ref[lo:hi, :])
        acc = d if acc is None else acc + d
    y = x + 0.5 * acc
    if final_norm:
        y = _rms(y, gf_ref[...])
    o_ref[...] = y


def _ffn(h, g, wgu, wd, mix=None, g_final=None):
    nb, lp, d = h.shape

    def rows(n):
        return pl.BlockSpec((None, TILE, n), lambda b, t: (b, t, 0))

    tile = rows(d)
    in_specs = [tile]
    args = [h]
    if mix is not None:
        in_specs += [rows(o.shape[-1]) for o in mix[:3]] + [_resident(w.shape) for w in mix[3:]]
        args += list(mix)
    in_specs += [_resident((1, d)), _resident(wgu.shape), _resident(wd.shape)]
    args += [g.reshape(1, d), wgu, wd]
    if g_final is not None:
        in_specs.append(_resident((1, d)))
        args.append(g_final.reshape(1, d))
    return pl.pallas_call(
        functools.partial(_ffn_kernel, mixed=mix is not None, final_norm=g_final is not None),
        grid=(nb, lp // TILE),
        in_specs=in_specs,
        out_specs=tile,
        out_shape=jax.ShapeDtypeStruct(h.shape, F32),
        compiler_params=_params(2),
        name="ffn",
    )(*args)


def _inproj_kernel(h_ref, g_ref, w_ref, gcq_ref, gckv_ref, wuq_ref, wukv_ref, cos_ref, sin_ref,
                   aq_ref, ak_ref, av_ref, bq_ref, bk_ref, bv_ref, bvs_ref, qc_ref, kc_ref, vc_ref, *, c_scale):
    xn = _rms(h_ref[...], g_ref[...]).astype(BF16)

    wide = 2 * _W_AK
    y = _dot(xn, w_ref[:, 0:wide])
    aq_ref[...] = y[:, _W_AQ:_W_AK].astype(BF16)
    ak_ref[...] = y[:, _W_AK:wide].astype(BF16)
    y = _dot(xn, w_ref[:, wide:2 * wide])
    av_ref[...] = y[:, 0:_W_BQ - _W_AV].astype(BF16)
    bq_ref[...] = y[:, _W_BQ - _W_AV:wide].astype(BF16)
    y = _dot(xn, w_ref[:, _W_BK:_W_TOTAL])

    def piece(lo, n):
        return y[:, lo - _W_BK:lo - _W_BK + n]

    bk_ref[...] = piece(_W_BK, LANES).astype(BF16)
    bv_ref[...] = piece(_W_BV, LANES).astype(BF16)
    bvs_ref[...] = piece(_W_BVS, LANES).astype(BF16)

    cos = cos_ref[...]
    sin = sin_ref[...]
    cos4 = jnp.concatenate([cos] * C_HEADS, axis=1)
    sin4 = jnp.concatenate([sin] * C_HEADS, axis=1)
    cqn = _rms(piece(_W_CQ, C_Q_RANK), gcq_ref[...]).astype(BF16)
    q = _dot(cqn, wuq_ref[...])
    n_q = C_HEADS * LANES
    qc_ref[...] = ((q[:, 0:n_q] * cos4 + q[:, n_q:2 * n_q] * sin4) * c_scale).astype(BF16)

    ckvn = _rms(piece(_W_CKV, C_KV_RANK), gckv_ref[...]).astype(BF16)
    k_rope = piece(_W_KR, LANES) * cos + piece(_W_KRR, LANES) * sin
    kv = _dot(ckvn, wukv_ref[...])
    kc_ref[...] = (kv[:, 0:n_q] + jnp.concatenate([k_rope] * C_HEADS, axis=1)).astype(BF16)
    vc_ref[...] = kv[:, n_q:n_q + C_HEADS * C_V_DIM].astype(BF16)


def _inproj(h, g, w_big, gcq, gckv, wuq, wukv, cos_t, sin_t):
    nb, lp, d = h.shape

    def tile(n):
        return pl.BlockSpec((None, TILE, n), lambda b, t: (b, t, 0))

    pos = pl.BlockSpec((TILE, LANES), lambda b, t: (t, 0))
    widths = (512, 512, 512, 512, 128, 128, 128, 512, 512, 256)
    return pl.pallas_call(
        functools.partial(_inproj_kernel, c_scale=(C_NOPE_DIM + C_ROPE_DIM) ** -0.5),
        grid=(nb, lp // TILE),
        in_specs=[tile(d), _resident((1, d)), _resident(w_big.shape), _resident((1, C_Q_RANK)),
                  _resident((1, C_KV_RANK)), _resident(wuq.shape), _resident(wukv.shape), pos, pos],
        out_specs=[tile(n) for n in widths],
        out_shape=[jax.ShapeDtypeStruct((nb, lp, n), BF16) for n in widths],
        compiler_params=_params(2),
        name="inproj",
    )(h, g.reshape(1, d), w_big, gcq.reshape(1, -1), gckv.reshape(1, -1), wuq, wukv, cos_t, sin_t)


def _static(x):
    return isinstance(x, (int, bool))


def _aligned(x, m):
    return x if _static(x) else pl.multiple_of(x, m)


def _imin(a, b):
    return min(a, b) if _static(a) else jnp.minimum(a, b)


def _imax(a, b):
    return max(a, b) if _static(a) else jnp.maximum(a, b)


def _iselect(c, a, b):
    return (a if c else b) if _static(c) else jnp.where(c, a, b)


def _flash_scratch():
    return ([pltpu.VMEM((ROW_BLOCK, TILE), F32)] * 2 + [pltpu.VMEM((ROW_BLOCK, LANES), F32)] * 2
            + [pltpu.VMEM((ROW_BLOCK, TILE), BF16)] * 2 + [pltpu.VMEM((ROW_BLOCK, LANES), F32)] * 2
            + [pltpu.VMEM((2 * TILE, LANES), F32), pltpu.VMEM((2 * TILE, 2 * LANES), F32)])


def _flash_pipeline(n_tiles, scores, values, stat_row, scratch, max_unroll):
    s0, s1, x0, x1, p0, p1, a0, a1, m_ref, acc_ref = scratch
    s_bufs, x_bufs, p_bufs, a_bufs = (s0, s1), (x0, x1), (p0, p1), (a0, a1)
    unroll = max(k for k in range(2, max_unroll + 1, 2) if n_tiles % k == 0)
    m_ref[...] = jnp.full(m_ref.shape, NEG_INF, F32)
    acc_ref[...] = jnp.zeros(acc_ref.shape, F32)
    p1[...] = jnp.zeros(p1.shape, BF16)
    a1[...] = jnp.ones(a1.shape, F32)

    def issue(t, par):
        s = scores(t, par)
        s_bufs[par][...] = s
        x_bufs[par][...] = jnp.broadcast_to(jnp.max(s, axis=-1, keepdims=True), (ROW_BLOCK, LANES))

    def accumulate(t, par):
        rows = pl.ds(stat_row(t, par), ROW_BLOCK)
        alpha = a_bufs[par][...]
        acc_ref[rows, :] = (jnp.concatenate([alpha, alpha], axis=1) * acc_ref[rows, :]
                            + _dot(p_bufs[par][...], values(t, par)))

    def step(t, par):
        issue(_imin(t + 1, n_tiles - 1), 1 - par)
        rows = pl.ds(stat_row(t, par), ROW_BLOCK)
        m_old = m_ref[rows, :]
        m_new = jnp.maximum(m_old, x_bufs[par][...])
        p_bufs[par][...] = jnp.exp((s_bufs[par][...] - jnp.concatenate([m_new] * (TILE // LANES), axis=1)
                                    ).astype(BF16))
        a_bufs[par][...] = jnp.exp(m_old - m_new)
        m_ref[rows, :] = m_new
        accumulate(_imax(t - 1, 0), 1 - par)

    issue(0, 0)

    def body(u, carry):
        for k in range(unroll):
            step(unroll * u + k, k % 2)
        return carry

    if unroll == n_tiles:
        for k in range(n_tiles):
            step(k, k % 2)
    else:
        lax.fori_loop(0, n_tiles // unroll, body, 0)
    accumulate(n_tiles - 1, 1)


(_TAB_LEFT, _TAB_DIAG, _TAB_RIGHT, _TAB_FAR_LEFT, _TAB_FAR_RIGHT,
 _TAB_LEFT_LEAD, _TAB_DIAG_LEAD, _TAB_FAR_LEAD) = range(8)


def _ones_right(v):
    return jnp.concatenate([v, jnp.ones(v.shape, v.dtype)], axis=1)


def _attn_a_kernel(q_ref, k_ref, v_ref, tab_ref, dl_ref, gs_ref, o_ref, qq_ref, *scratch, lam_init, n_chunks):
    i = pl.program_id(2)
    q = q_ref[...]
    lane = lax.broadcasted_iota(jnp.int32, q.shape, 1)
    zero = jnp.zeros_like(q)
    qq_ref[0:TILE, :] = jnp.where(lane < A_QK_DIM, q, zero)
    qq_ref[TILE:2 * TILE, :] = jnp.where(lane >= A_QK_DIM, q, zero)
    blocks = 2 * TILE // ROW_BLOCK

    def q_row(t):
        return _aligned((t % blocks) * ROW_BLOCK, ROW_BLOCK)

    def k_row(t):
        return _aligned((t // blocks) * TILE, TILE)

    def scores(t, par):
        j = t // blocks
        lead = jnp.where(i == 0, _TAB_DIAG_LEAD, jnp.where(i == 1, _TAB_LEFT_LEAD, _TAB_FAR_LEAD))
        rest = jnp.where(j < i - 1, _TAB_FAR_LEFT, jnp.where(j > i + 1, _TAB_FAR_RIGHT, j - i + 1))
        table = jnp.where(j == 0, lead, rest)
        row = _aligned((t % (TILE // ROW_BLOCK)) * ROW_BLOCK, ROW_BLOCK)
        return (_dot_nt(qq_ref[pl.ds(q_row(t), ROW_BLOCK), :], k_ref[pl.ds(k_row(t), TILE), :])
                + tab_ref[table, pl.ds(row, ROW_BLOCK), :])

    def values(t, par):
        return _ones_right(v_ref[pl.ds(k_row(t), TILE), :])

    _flash_pipeline(n_chunks * blocks, scores, values, lambda t, par: q_row(t), scratch, A_PIPE_UNROLL)

    acc_ref = scratch[-1]
    o = acc_ref[:, 0:A_V_DIM] / acc_ref[:, A_V_DIM:2 * A_V_DIM]
    dl = dl_ref[...]
    lam = (jnp.exp(jnp.sum(dl[0:1] * dl[1:2], axis=-1, keepdims=True))
           - jnp.exp(jnp.sum(dl[2:3] * dl[3:4], axis=-1, keepdims=True)) + lam_init)
    w = o[0:TILE] - lam * o[TILE:2 * TILE]
    o_ref[...] = (_rms(w, gs_ref[...]) * (1.0 - lam_init)).astype(BF16)


def _attn_a(aq, ak, av, tab, dl, gs, lam_init):
    nb, lp, _ = aq.shape
    n_chunks = lp // TILE
    qspec = pl.BlockSpec((None, TILE, LANES), lambda b, h, t: (b, t, h))
    kvspec = pl.BlockSpec((None, lp, LANES), lambda b, h, t: (b, 0, h))
    return pl.pallas_call(
        functools.partial(_attn_a_kernel, lam_init=lam_init, n_chunks=n_chunks),
        grid=(nb, A_HEADS, n_chunks),
        in_specs=[qspec, kvspec, kvspec,
                  pl.BlockSpec((None,) + tab.shape[1:], lambda b, h, t: (h, 0, 0, 0)),
                  _resident((4, A_QK_DIM)), _resident((1, A_V_DIM))],
        out_specs=qspec,
        out_shape=jax.ShapeDtypeStruct((nb, lp, A_HEADS * A_V_DIM), BF16),
        scratch_shapes=[pltpu.VMEM((2 * TILE, LANES), BF16)] + _flash_scratch(),
        compiler_params=_params(3),
        name="attn_a",
    )(aq, ak, av, tab, dl, gs.reshape(1, A_V_DIM))


def _attn_c_kernel(q_ref, k_ref, v_ref, kmask_ref, o_ref, *scratch, n_chunks):
    blocks = 2 * TILE // ROW_BLOCK

    def q_row(t):
        return _aligned(((t % blocks) // 2) * ROW_BLOCK, ROW_BLOCK)

    def k_row(t):
        return _aligned((t // blocks) * TILE, TILE)

    def scores(t, par):
        cols = slice(par * LANES, (par + 1) * LANES)
        return (_dot_nt(q_ref[pl.ds(q_row(t), ROW_BLOCK), cols], k_ref[pl.ds(k_row(t), TILE), cols])
                + kmask_ref[_iselect(t < blocks, 1, 0)])

    def values(t, par):
        return _ones_right(v_ref[pl.ds(k_row(t), TILE), :])

    def stat_row(t, par):
        return _aligned(par * TILE + q_row(t), ROW_BLOCK)

    _flash_pipeline(n_chunks * blocks, scores, values, stat_row, scratch, C_PIPE_UNROLL)

    acc_ref = scratch[-1]
    o = acc_ref[:, 0:LANES] / acc_ref[:, LANES:2 * LANES]
    lane = lax.broadcasted_iota(jnp.int32, (TILE, LANES), 1)
    o_ref[...] = jnp.where(lane < C_V_DIM, o[0:TILE], o[TILE:2 * TILE]).astype(BF16)


def _attn_c(qc, kc, vc, kmask):
    nb, lp, _ = qc.shape
    n_chunks = lp // TILE
    return pl.pallas_call(
        functools.partial(_attn_c_kernel, n_chunks=n_chunks),
        grid=(nb, C_HEADS // 2, n_chunks),
        in_specs=[pl.BlockSpec((None, TILE, 2 * LANES), lambda b, p, t: (b, t, p)),
                  pl.BlockSpec((None, lp, 2 * LANES), lambda b, p, t: (b, 0, p)),
                  pl.BlockSpec((None, lp, LANES), lambda b, p, t: (b, 0, p)),
                  _resident(kmask.shape)],
        out_specs=pl.BlockSpec((None, TILE, LANES), lambda b, p, t: (b, t, p)),
        out_shape=jax.ShapeDtypeStruct((nb, lp, C_HEADS * C_V_DIM), BF16),
        scratch_shapes=_flash_scratch(),
        compiler_params=_params(3),
        name="attn_c",
    )(qc, kc, vc, kmask)


_B_ORDER = (0, 3, 1, 2)


def _attn_b_kernel(sink_ref, q_ref, k_ref, v_ref, vs_ref, tab_ref, o_ref, *, lp):
    t = pl.program_id(1)
    lane = lax.broadcasted_iota(jnp.int32, (BLOCK, LANES), 1)
    kk = lax.broadcasted_iota(jnp.int32, (1, 3 * BLOCK), 1)
    row = lax.broadcasted_iota(jnp.int32, (B_HEADS * BLOCK, 1), 0)
    sink = jnp.full((B_HEADS * BLOCK, 1), sink_ref[_B_ORDER[0]], F32)
    for n in range(1, B_HEADS):
        sink = jnp.where(row >= n * BLOCK, sink_ref[_B_ORDER[n]], sink)
    half = B_HEADS * BLOCK // 2
    for blk in range(TILE // BLOCK):
        gblk = t * (TILE // BLOCK) + blk
        row0 = pl.multiple_of(gblk * BLOCK, BLOCK)
        kw = k_ref[pl.ds(row0, 3 * BLOCK), :]
        vw = v_ref[pl.ds(row0, 3 * BLOCK), :]
        vsw = vs_ref[pl.ds(row0, 3 * BLOCK), :]
        kslot = (gblk - 1) * BLOCK + kk
        row_mask = jnp.where((kslot >= META_START) & (kslot < lp), 0.0, NEG_INF)
        rows = slice(blk * BLOCK, (blk + 1) * BLOCK)
        qs = jnp.concatenate([q_ref[rows, h * LANES:(h + 1) * LANES] for h in _B_ORDER], axis=0)
        s = _dot_nt(qs, kw) + tab_ref[...] + row_mask
        m = jnp.maximum(jnp.max(s, axis=-1, keepdims=True), sink)
        p = jnp.exp(s - m)
        denom = jnp.sum(p, axis=-1, keepdims=True) + jnp.exp(sink - m)
        pb = p.astype(BF16)
        o03 = _dot(pb[0:half], vw) / denom[0:half]
        o12 = _dot(pb[half:], vsw) / denom[half:]
        o_ref[rows, 0:LANES] = jnp.where(lane < B_HEAD_DIM, o03[0:BLOCK], o12[0:BLOCK]).astype(BF16)
        o_ref[rows, LANES:2 * LANES] = jnp.where(lane < B_HEAD_DIM, o12[BLOCK:], o03[BLOCK:]).astype(BF16)


def _attn_b(bq, bk_pad, bv_pad, bvs_pad, tab, sinks):
    nb, lp, _ = bq.shape
    kvspec = pl.BlockSpec((None, lp + 2 * BLOCK, LANES), lambda b, t: (b, 0, 0))
    return pl.pallas_call(
        functools.partial(_attn_b_kernel, lp=lp),
        grid=(nb, lp // TILE),
        in_specs=[pl.BlockSpec(memory_space=pltpu.SMEM),
                  pl.BlockSpec((None, TILE, B_HEADS * LANES), lambda b, t: (b, t, 0)),
                  kvspec, kvspec, kvspec, _resident(tab.shape)],
        out_specs=pl.BlockSpec((None, TILE, B_HEADS * B_HEAD_DIM), lambda b, t: (b, t, 0)),
        out_shape=jax.ShapeDtypeStruct((nb, lp, B_HEADS * B_HEAD_DIM), BF16),
        compiler_params=_params(2),
        name="attn_b",
    )(sinks, bq, bk_pad, bv_pad, bvs_pad, tab)


def _t5_bucket(rel):
    half = N_BUCKETS // 2
    max_exact = half // 2
    ret = jnp.where(rel > 0, half, 0)
    n = jnp.abs(rel)
    nf = jnp.maximum(n, 1).astype(F32)
    large = max_exact + (jnp.log(nf / max_exact) / math.log(MAX_DISTANCE / max_exact)
                         * (half - max_exact)).astype(jnp.int32)
    large = jnp.minimum(large, half - 1)
    return ret + jnp.where(n < max_exact, n, large)


def _rot_cols(w):
    half = w.shape[-1] // 2
    return jnp.concatenate([-w[..., half:], w[..., :half]], axis=-1)


def _bias_of_rel(table, rel):
    bucket = _t5_bucket(rel)[None]
    out = jnp.zeros((table.shape[1],) + rel.shape, F32)
    for b in range(N_BUCKETS):
        out = jnp.where(bucket == b, table[b].reshape((-1,) + (1,) * rel.ndim), out)
    return out


def _bias_tables(rel_bias, lp):
    rb = rel_bias.astype(F32)
    rela = (jnp.arange(3 * TILE) - TILE)[None, :] - jnp.arange(TILE)[:, None]
    wide = _bias_of_rel(rb[:, :A_HEADS], rela)
    left, diag, right = (wide[:, :, d * TILE:(d + 1) * TILE] for d in range(3))
    far_a = rb[:, :A_HEADS][_t5_bucket(jnp.array([-2 * TILE, 2 * TILE]))].T
    lead = jnp.where(jnp.arange(TILE) >= META_START, 0.0, NEG_INF).astype(F32).reshape(1, 1, TILE)
    far_left = jnp.broadcast_to(far_a[:, 0][:, None, None], left.shape)
    far_right = jnp.broadcast_to(far_a[:, 1][:, None, None], left.shape)
    tab_a = jnp.stack([left, diag, right, far_left, far_right, left + lead, diag + lead, far_left + lead], axis=1)
    kmask = jnp.concatenate([jnp.zeros_like(lead), lead], axis=0)
    relb = (jnp.arange(3 * BLOCK) - BLOCK)[None, :] - jnp.arange(BLOCK)[:, None]
    tab_b = jnp.where((jnp.abs(relb) <= WINDOW)[None], _bias_of_rel(rb[:, A_HEADS:], relb), NEG_INF)
    tab_b = jnp.concatenate([tab_b[h] for h in _B_ORDER], axis=0)
    slot = jnp.arange(lp)
    inv = ROPE_THETA ** (-jnp.arange(0, C_ROPE_DIM, 2, dtype=F32) / C_ROPE_DIM)
    ang = (slot - META_START).astype(F32)[:, None] * inv[None, :]
    ang = jnp.concatenate([ang, ang], axis=-1)
    ones = jnp.ones((lp, C_NOPE_DIM), F32)
    zeros_n = jnp.zeros((lp, C_NOPE_DIM), F32)
    zeros_p = jnp.zeros((lp, LANES - C_NOPE_DIM - C_ROPE_DIM), F32)
    cos_t = jnp.concatenate([ones, jnp.cos(ang), zeros_p], axis=-1)
    sin_t = jnp.concatenate([zeros_n, jnp.sin(ang), zeros_p], axis=-1)
    return tab_a, tab_b, kmask, cos_t, sin_t


def _layer_weights(w_in, w_uq, w_ukv, w_out):
    d = w_in.shape[0]
    sizes = (512, 512, 512, 256, 128, 128, 256, 128, 32)
    offs = [0]
    for s in sizes:
        offs.append(offs[-1] + s)
    aq, ak, av, bq, bk, bv, cq, ckv, kr = (w_in[:, offs[i]:offs[i + 1]] for i in range(9))
    zeros64 = jnp.zeros((d, B_HEAD_DIM), F32)
    bq_pad = []
    for h in range(B_HEADS):
        qh = bq[:, h * B_HEAD_DIM:(h + 1) * B_HEAD_DIM] * (B_HEAD_DIM ** -0.5)
        bq_pad += [qh, zeros64] if h // (B_HEADS // B_KV_HEADS) == 0 else [zeros64, qh]
    bvs = jnp.concatenate([bv[:, B_HEAD_DIM:], bv[:, :B_HEAD_DIM]], axis=1)
    pad_lo = jnp.zeros((d, C_NOPE_DIM), F32)
    pad_hi = jnp.zeros((d, LANES - C_NOPE_DIM - C_ROPE_DIM), F32)
    w_big = jnp.concatenate(
        [aq * (A_QK_DIM ** -0.5), ak, av] + bq_pad + [bk, bv, bvs, cq, ckv,
                                                      pad_lo, kr, pad_hi, pad_lo, _rot_cols(kr), pad_hi],
        axis=1).astype(BF16)
    uq = w_uq.reshape(C_Q_RANK, C_HEADS, C_NOPE_DIM + C_ROPE_DIM)
    uq_nope, uq_rope = uq[..., :C_NOPE_DIM], uq[..., C_NOPE_DIM:]
    zq = jnp.zeros((C_Q_RANK, C_HEADS, LANES - C_NOPE_DIM - C_ROPE_DIM), F32)
    wuqm = jnp.concatenate([uq_nope, uq_rope, zq], axis=-1).reshape(C_Q_RANK, C_HEADS * LANES).astype(BF16)
    wuqr = jnp.concatenate([jnp.zeros_like(uq_nope), _rot_cols(uq_rope), zq], axis=-1
                           ).reshape(C_Q_RANK, C_HEADS * LANES).astype(BF16)
    ukv = w_ukv.reshape(C_KV_RANK, C_HEADS, C_NOPE_DIM + C_V_DIM)
    wukvk = jnp.concatenate([ukv[..., :C_NOPE_DIM], jnp.zeros((C_KV_RANK, C_HEADS, LANES - C_NOPE_DIM), F32)],
                            axis=-1).reshape(C_KV_RANK, C_HEADS * LANES).astype(BF16)
    wukvv = ukv[..., C_NOPE_DIM:].reshape(C_KV_RANK, C_HEADS * C_V_DIM).astype(BF16)
    n_a = A_HEADS * A_V_DIM
    n_b = B_HEADS * B_HEAD_DIM
    wo = w_out.astype(BF16)
    wuq = jnp.concatenate([wuqm, wuqr], axis=1)
    wukv = jnp.concatenate([wukvk, wukvv], axis=1)
    return w_big, wuq, wukv, wo[:n_a], wo[n_a:n_a + n_b], wo[n_a + n_b:]


def _trunk(x, meta, rel_bias, g_ffn1, w_ffn1_gu, w_ffn1_down, g_mix, w_in, diff_lambda, g_subln, sinks, g_cq,
           g_ckv, w_uq, w_ukv, w_out, g_ffn2, w_ffn2_gu, w_ffn2_down, g_final):
    nb, seq, d = x.shape
    lp = BLOCK + seq
    assert lp % TILE == 0 and d == D_MODEL
    lead = jnp.concatenate([jnp.zeros((META_START, d), x.dtype), meta.astype(x.dtype)], axis=0)
    h = jnp.concatenate([jnp.broadcast_to(lead[None], (nb, BLOCK, d)), x], axis=1)
    tab_a, tab_b, kmask, cos_t, sin_t = _bias_tables(rel_bias, lp)
    pad = ((0, 0), (BLOCK, BLOCK), (0, 0))
    for l in range(DEPTH):
        w_big, wuq, wukv, wo_a, wo_b, wo_c = _layer_weights(w_in[l], w_uq[l], w_ukv[l], w_out[l])
        h = _ffn(h, g_ffn1[l], w_ffn1_gu[l].astype(BF16), w_ffn1_down[l].astype(BF16))
        aq, ak, av, bq, bk, bv, bvs, qc, kc, vc = _inproj(h, g_mix[l], w_big, g_cq[l], g_ckv[l], wuq, wukv,
                                                           cos_t, sin_t)
        lam_init = 0.8 - 0.6 * math.exp(-0.3 * l)
        oa = _attn_a(aq, ak, av, tab_a, diff_lambda[l].astype(F32), g_subln[l], lam_init)
        ob = _attn_b(bq, jnp.pad(bk, pad), jnp.pad(bv, pad), jnp.pad(bvs, pad), tab_b, sinks[l].astype(F32))
        oc = _attn_c(qc, kc, vc, kmask)
        h = _ffn(h, g_ffn2[l], w_ffn2_gu[l].astype(BF16), w_ffn2_down[l].astype(BF16),
                 mix=(oa, ob, oc, wo_a, wo_b, wo_c), g_final=g_final if l == DEPTH - 1 else None)
    return h[:, BLOCK:]


def kernel(x_prompt, x_sample, meta, rel_bias, g_ffn1, w_ffn1_gu, w_ffn1_down, g_mix, w_in, diff_lambda, g_subln,
           sinks, g_cq, g_ckv, w_uq, w_ukv, w_out, g_ffn2, w_ffn2_gu, w_ffn2_down, g_final):
    n_prompt = x_prompt.shape[0]
    x = jnp.concatenate([x_prompt, x_sample], axis=0)
    y = _trunk(x, meta, rel_bias, g_ffn1, w_ffn1_gu, w_ffn1_down, g_mix, w_in, diff_lambda, g_subln, sinks, g_cq,
               g_ckv, w_uq, w_ukv, w_out, g_ffn2, w_ffn2_gu, w_ffn2_down, g_final)
    return (y[:n_prompt], y[n_prompt:])
```

```python
import functools
import math

import jax
import jax.numpy as jnp
from jax import lax
from jax.experimental import pallas as pl
from jax.experimental.pallas import tpu as pltpu

D_MODEL = 1024
DEPTH = 2
N_META = 16
BLOCK = 128
META_START = BLOCK - N_META
A_HEADS = 4
A_QK_DIM = 64
A_V_DIM = 2 * A_QK_DIM
B_HEADS = 4
B_KV_HEADS = 2
B_HEAD_DIM = 64
WINDOW = 128
C_HEADS = 4
C_Q_RANK = 256
C_KV_RANK = 128
C_NOPE_DIM = 64
C_ROPE_DIM = 32
C_V_DIM = 64
ROPE_THETA = 10000.0
N_BUCKETS = 32
MAX_DISTANCE = 128
D_FF = 2816
EPS = 1e-6

LANES = 128
TILE = 5 * BLOCK
MXU_COLS = 256
FF_SPLITS = (0, 6 * MXU_COLS, D_FF)
ROW_BLOCK = TILE // 4
A_PIPE_UNROLL = 52
C_PIPE_UNROLL = 52
VMEM_LIMIT = 56 * 1024 * 1024

F32 = jnp.float32
BF16 = jnp.bfloat16
NEG_INF = float("-inf")

_W_AQ, _W_AK, _W_AV, _W_BQ = 0, 512, 1024, 1536
_W_BK, _W_BV, _W_BVS = 2048, 2176, 2304
_W_CQ, _W_CKV, _W_KR, _W_KRR = 2432, 2688, 2816, 2944
_W_TOTAL = 3072


def _params(n_axes):
    return pltpu.CompilerParams(dimension_semantics=("arbitrary",) * n_axes, vmem_limit_bytes=VMEM_LIMIT)


def _resident(shape):
    return pl.BlockSpec(shape, lambda *_: (0,) * len(shape), pipeline_mode=pl.Buffered(1))


def _rms(x, g):
    return x * lax.rsqrt(jnp.mean(x * x, axis=-1, keepdims=True) + EPS) * g


def _dot(a, b):
    return jnp.dot(a, b, preferred_element_type=F32)


def _dot_nt(a, b):
    return lax.dot_general(a, b, (((1,), (1,)), ((), ())), preferred_element_type=F32)


def _ffn_kernel(x_ref, *refs, mixed, final_norm):
    refs = list(refs)
    x = x_ref[...]
    if mixed:
        oa_ref, ob_ref, oc_ref, wa_ref, wb_ref, wc_ref = refs[:6]
        del refs[:6]
        x = x + _dot(oa_ref[...], wa_ref[...]) + _dot(ob_ref[...], wb_ref[...]) + _dot(oc_ref[...], wc_ref[...])
    g_ref, wgu_ref, wd_ref = refs[:3]
    gf_ref = refs[3] if final_norm else None
    o_ref = refs[-1]
    xn = _rms(x, g_ref[...]).astype(BF16)
    acc = None
    for lo, hi in zip(FF_SPLITS[:-1], FF_SPLITS[1:]):
        g = _dot(xn, wgu_ref[:, lo:hi])
        u = _dot(xn, wgu_ref[:, D_FF + lo:D_FF + hi])
        a = (g * jax.nn.sigmoid(g) * u).astype(BF16)
        d = _dot(a, wd_ref[lo:hi, :])
        acc = d if acc is None else acc + d
    y = x + 0.5 * acc
    if final_norm:
        y = _rms(y, gf_ref[...])
    o_ref[...] = y


def _ffn(h, g, wgu, wd, mix=None, g_final=None):
    nb, lp, d = h.shape

    def rows(n):
        return pl.BlockSpec((None, TILE, n), lambda b, t: (b, t, 0))

    tile = rows(d)
    in_specs = [tile]
    args = [h]
    if mix is not None:
        in_specs += [rows(o.shape[-1]) for o in mix[:3]] + [_resident(w.shape) for w in mix[3:]]
        args += list(mix)
    in_specs += [_resident((1, d)), _resident(wgu.shape), _resident(wd.shape)]
    args += [g.reshape(1, d), wgu, wd]
    if g_final is not None:
        in_specs.append(_resident((1, d)))
        args.append(g_final.reshape(1, d))
    return pl.pallas_call(
        functools.partial(_ffn_kernel, mixed=mix is not None, final_norm=g_final is not None),
        grid=(nb, lp // TILE),
        in_specs=in_specs,
        out_specs=tile,
        out_shape=jax.ShapeDtypeStruct(h.shape, F32),
        compiler_params=_params(2),
        name="ffn",
    )(*args)


def _inproj_kernel(h_ref, g_ref, w_ref, gcq_ref, gckv_ref, wuq_ref, wukv_ref, cos_ref, sin_ref,
                   aq_ref, ak_ref, av_ref, bq_ref, bk_ref, bv_ref, bvs_ref, qc_ref, kc_ref, vc_ref, *, c_scale):
    xn = _rms(h_ref[...], g_ref[...]).astype(BF16)

    wide = 2 * _W_AK
    y = _dot(xn, w_ref[:, 0:wide])
    aq_ref[...] = y[:, _W_AQ:_W_AK].astype(BF16)
    ak_ref[...] = y[:, _W_AK:wide].astype(BF16)
    y = _dot(xn, w_ref[:, wide:2 * wide])
    av_ref[...] = y[:, 0:_W_BQ - _W_AV].astype(BF16)
    bq_ref[...] = y[:, _W_BQ - _W_AV:wide].astype(BF16)
    y = _dot(xn, w_ref[:, _W_BK:_W_TOTAL])

    def piece(lo, n):
        return y[:, lo - _W_BK:lo - _W_BK + n]

    bk_ref[...] = piece(_W_BK, LANES).astype(BF16)
    bv_ref[...] = piece(_W_BV, LANES).astype(BF16)
    bvs_ref[...] = piece(_W_BVS, LANES).astype(BF16)

    cos = cos_ref[...]
    sin = sin_ref[...]
    cos4 = jnp.concatenate([cos] * C_HEADS, axis=1)
    sin4 = jnp.concatenate([sin] * C_HEADS, axis=1)
    cqn = _rms(piece(_W_CQ, C_Q_RANK), gcq_ref[...]).astype(BF16)
    q = _dot(cqn, wuq_ref[...])
    n_q = C_HEADS * LANES
    qc_ref[...] = ((q[:, 0:n_q] * cos4 + q[:, n_q:2 * n_q] * sin4) * c_scale).astype(BF16)

    ckvn = _rms(piece(_W_CKV, C_KV_RANK), gckv_ref[...]).astype(BF16)
    k_rope = piece(_W_KR, LANES) * cos + piece(_W_KRR, LANES) * sin
    kv = _dot(ckvn, wukv_ref[...])
    kc_ref[...] = (kv[:, 0:n_q] + jnp.concatenate([k_rope] * C_HEADS, axis=1)).astype(BF16)
    vc_ref[...] = kv[:, n_q:n_q + C_HEADS * C_V_DIM].astype(BF16)


def _inproj(h, g, w_big, gcq, gckv, wuq, wukv, cos_t, sin_t):
    nb, lp, d = h.shape

    def tile(n):
        return pl.BlockSpec((None, TILE, n), lambda b, t: (b, t, 0))

    pos = pl.BlockSpec((TILE, LANES), lambda b, t: (t, 0))
    widths = (512, 512, 512, 512, 128, 128, 128, 512, 512, 256)
    return pl.pallas_call(
        functools.partial(_inproj_kernel, c_scale=(C_NOPE_DIM + C_ROPE_DIM) ** -0.5),
        grid=(nb, lp // TILE),
        in_specs=[tile(d), _resident((1, d)), _resident(w_big.shape), _resident((1, C_Q_RANK)),
                  _resident((1, C_KV_RANK)), _resident(wuq.shape), _resident(wukv.shape), pos, pos],
        out_specs=[tile(n) for n in widths],
        out_shape=[jax.ShapeDtypeStruct((nb, lp, n), BF16) for n in widths],
        compiler_params=_params(2),
        name="inproj",
    )(h, g.reshape(1, d), w_big, gcq.reshape(1, -1), gckv.reshape(1, -1), wuq, wukv, cos_t, sin_t)


def _static(x):
    return isinstance(x, (int, bool))


def _aligned(x, m):
    return x if _static(x) else pl.multiple_of(x, m)


def _imin(a, b):
    return min(a, b) if _static(a) else jnp.minimum(a, b)


def _imax(a, b):
    return max(a, b) if _static(a) else jnp.maximum(a, b)


def _iselect(c, a, b):
    return (a if c else b) if _static(c) else jnp.where(c, a, b)


def _flash_scratch():
    return ([pltpu.VMEM((ROW_BLOCK, TILE), F32)] * 2 + [pltpu.VMEM((ROW_BLOCK, LANES), F32)] * 2
            + [pltpu.VMEM((ROW_BLOCK, TILE), BF16)] * 2 + [pltpu.VMEM((ROW_BLOCK, LANES), F32)] * 2
            + [pltpu.VMEM((2 * TILE, LANES), F32), pltpu.VMEM((2 * TILE, 2 * LANES), F32)])


def _flash_pipeline(n_tiles, scores, values, stat_row, scratch, max_unroll):
    s0, s1, x0, x1, p0, p1, a0, a1, m_ref, acc_ref = scratch
    s_bufs, x_bufs, p_bufs, a_bufs = (s0, s1), (x0, x1), (p0, p1), (a0, a1)
    unroll = max(k for k in range(2, max_unroll + 1, 2) if n_tiles % k == 0)
    m_ref[...] = jnp.full(m_ref.shape, NEG_INF, F32)
    acc_ref[...] = jnp.zeros(acc_ref.shape, F32)
    p1[...] = jnp.zeros(p1.shape, BF16)
    a1[...] = jnp.ones(a1.shape, F32)

    def issue(t, par):
        s = scores(t, par)
        s_bufs[par][...] = s
        x_bufs[par][...] = jnp.broadcast_to(jnp.max(s, axis=-1, keepdims=True), (ROW_BLOCK, LANES))

    def accumulate(t, par):
        rows = pl.ds(stat_row(t, par), ROW_BLOCK)
        alpha = a_bufs[par][...]
        acc_ref[rows, :] = (jnp.concatenate([alpha, alpha], axis=1) * acc_ref[rows, :]
                            + _dot(p_bufs[par][...], values(t, par)))

    def step(t, par):
        issue(_imin(t + 1, n_tiles - 1), 1 - par)
        rows = pl.ds(stat_row(t, par), ROW_BLOCK)
        m_old = m_ref[rows, :]
        m_new = jnp.maximum(m_old, x_bufs[par][...])
        p_bufs[par][...] = jnp.exp((s_bufs[par][...] - jnp.concatenate([m_new] * (TILE // LANES), axis=1)
                                    ).astype(BF16))
        a_bufs[par][...] = jnp.exp(m_old - m_new)
        m_ref[rows, :] = m_new
        accumulate(_imax(t - 1, 0), 1 - par)

    issue(0, 0)

    def body(u, carry):
        for k in range(unroll):
            step(unroll * u + k, k % 2)
        return carry

    if unroll == n_tiles:
        for k in range(n_tiles):
            step(k, k % 2)
    else:
        lax.fori_loop(0, n_tiles // unroll, body, 0)
    accumulate(n_tiles - 1, 1)


(_TAB_LEFT, _TAB_DIAG, _TAB_RIGHT, _TAB_FAR_LEFT, _TAB_FAR_RIGHT,
 _TAB_LEFT_LEAD, _TAB_DIAG_LEAD, _TAB_FAR_LEAD) = range(8)


def _ones_right(v):
    return jnp.concatenate([v, jnp.ones(v.shape, v.dtype)], axis=1)


def _attn_a_kernel(q_ref, k_ref, v_ref, tab_ref, dl_ref, gs_ref, o_ref, qq_ref, *scratch, lam_init, n_chunks):
    i = pl.program_id(2)
    q = q_ref[...]
    lane = lax.broadcasted_iota(jnp.int32, q.shape, 1)
    zero = jnp.zeros_like(q)
    qq_ref[0:TILE, :] = jnp.where(lane < A_QK_DIM, q, zero)
    qq_ref[TILE:2 * TILE, :] = jnp.where(lane >= A_QK_DIM, q, zero)
    blocks = 2 * TILE // ROW_BLOCK

    def q_row(t):
        return _aligned((t % blocks) * ROW_BLOCK, ROW_BLOCK)

    def k_row(t):
        return _aligned((t // blocks) * TILE, TILE)

    def scores(t, par):
        j = t // blocks
        lead = jnp.where(i == 0, _TAB_DIAG_LEAD, jnp.where(i == 1, _TAB_LEFT_LEAD, _TAB_FAR_LEAD))
        rest = jnp.where(j < i - 1, _TAB_FAR_LEFT, jnp.where(j > i + 1, _TAB_FAR_RIGHT, j - i + 1))
        table = jnp.where(j == 0, lead, rest)
        row = _aligned((t % (TILE // ROW_BLOCK)) * ROW_BLOCK, ROW_BLOCK)
        return (_dot_nt(qq_ref[pl.ds(q_row(t), ROW_BLOCK), :], k_ref[pl.ds(k_row(t), TILE), :])
                + tab_ref[table, pl.ds(row, ROW_BLOCK), :])

    def values(t, par):
        return _ones_right(v_ref[pl.ds(k_row(t), TILE), :])

    _flash_pipeline(n_chunks * blocks, scores, values, lambda t, par: q_row(t), scratch, A_PIPE_UNROLL)

    acc_ref = scratch[-1]
    o = acc_ref[:, 0:A_V_DIM] / acc_ref[:, A_V_DIM:2 * A_V_DIM]
    dl = dl_ref[...]
    lam = (jnp.exp(jnp.sum(dl[0:1] * dl[1:2], axis=-1, keepdims=True))
           - jnp.exp(jnp.sum(dl[2:3] * dl[3:4], axis=-1, keepdims=True)) + lam_init)
    w = o[0:TILE] - lam * o[TILE:2 * TILE]
    o_ref[...] = (_rms(w, gs_ref[...]) * (1.0 - lam_init)).astype(BF16)


def _attn_a(aq, ak, av, tab, dl, gs, lam_init):
    nb, lp, _ = aq.shape
    n_chunks = lp // TILE
    qspec = pl.BlockSpec((None, TILE, LANES), lambda b, h, t: (b, t, h))
    kvspec = pl.BlockSpec((None, lp, LANES), lambda b, h, t: (b, 0, h))
    return pl.pallas_call(
        functools.partial(_attn_a_kernel, lam_init=lam_init, n_chunks=n_chunks),
        grid=(nb, A_HEADS, n_chunks),
        in_specs=[qspec, kvspec, kvspec,
                  pl.BlockSpec((None,) + tab.shape[1:], lambda b, h, t: (h, 0, 0, 0)),
                  _resident((4, A_QK_DIM)), _resident((1, A_V_DIM))],
        out_specs=qspec,
        out_shape=jax.ShapeDtypeStruct((nb, lp, A_HEADS * A_V_DIM), BF16),
        scratch_shapes=[pltpu.VMEM((2 * TILE, LANES), BF16)] + _flash_scratch(),
        compiler_params=_params(3),
        name="attn_a",
    )(aq, ak, av, tab, dl, gs.reshape(1, A_V_DIM))


def _attn_c_kernel(q_ref, k_ref, v_ref, kmask_ref, o_ref, *scratch, n_chunks):
    blocks = 2 * TILE // ROW_BLOCK

    def q_row(t):
        return _aligned(((t % blocks) // 2) * ROW_BLOCK, ROW_BLOCK)

    def k_row(t):
        return _aligned((t // blocks) * TILE, TILE)

    def scores(t, par):
        cols = slice(par * LANES, (par + 1) * LANES)
        return (_dot_nt(q_ref[pl.ds(q_row(t), ROW_BLOCK), cols], k_ref[pl.ds(k_row(t), TILE), cols])
                + kmask_ref[_iselect(t < blocks, 1, 0)])

    def values(t, par):
        return _ones_right(v_ref[pl.ds(k_row(t), TILE), :])

    def stat_row(t, par):
        return _aligned(par * TILE + q_row(t), ROW_BLOCK)

    _flash_pipeline(n_chunks * blocks, scores, values, stat_row, scratch, C_PIPE_UNROLL)

    acc_ref = scratch[-1]
    o = acc_ref[:, 0:LANES] / acc_ref[:, LANES:2 * LANES]
    lane = lax.broadcasted_iota(jnp.int32, (TILE, LANES), 1)
    o_ref[...] = jnp.where(lane < C_V_DIM, o[0:TILE], o[TILE:2 * TILE]).astype(BF16)


def _attn_c(qc, kc, vc, kmask):
    nb, lp, _ = qc.shape
    n_chunks = lp // TILE
    return pl.pallas_call(
        functools.partial(_attn_c_kernel, n_chunks=n_chunks),
        grid=(nb, C_HEADS // 2, n_chunks),
        in_specs=[pl.BlockSpec((None, TILE, 2 * LANES), lambda b, p, t: (b, t, p)),
                  pl.BlockSpec((None, lp, 2 * LANES), lambda b, p, t: (b, 0, p)),
                  pl.BlockSpec((None, lp, LANES), lambda b, p, t: (b, 0, p)),
                  _resident(kmask.shape)],
        out_specs=pl.BlockSpec((None, TILE, LANES), lambda b, p, t: (b, t, p)),
        out_shape=jax.ShapeDtypeStruct((nb, lp, C_HEADS * C_V_DIM), BF16),
        scratch_shapes=_flash_scratch(),
        compiler_params=_params(3),
        name="attn_c",
    )(qc, kc, vc, kmask)


_B_ORDER = (0, 3, 1, 2)


def _attn_b_kernel(sink_ref, q_ref, k_ref, v_ref, vs_ref, tab_ref, o_ref, *, lp):
    t = pl.program_id(1)
    lane = lax.broadcasted_iota(jnp.int32, (BLOCK, LANES), 1)
    kk = lax.broadcasted_iota(jnp.int32, (1, 3 * BLOCK), 1)
    row = lax.broadcasted_iota(jnp.int32, (B_HEADS * BLOCK, 1), 0)
    sink = jnp.full((B_HEADS * BLOCK, 1), sink_ref[_B_ORDER[0]], F32)
    for n in range(1, B_HEADS):
        sink = jnp.where(row >= n * BLOCK, sink_ref[_B_ORDER[n]], sink)
    half = B_HEADS * BLOCK // 2
    for blk in range(TILE // BLOCK):
        gblk = t * (TILE // BLOCK) + blk
        row0 = pl.multiple_of(gblk * BLOCK, BLOCK)
        kw = k_ref[pl.ds(row0, 3 * BLOCK), :]
        vw = v_ref[pl.ds(row0, 3 * BLOCK), :]
        vsw = vs_ref[pl.ds(row0, 3 * BLOCK), :]
        kslot = (gblk - 1) * BLOCK + kk
        row_mask = jnp.where((kslot >= META_START) & (kslot < lp), 0.0, NEG_INF)
        rows = slice(blk * BLOCK, (blk + 1) * BLOCK)
        qs = jnp.concatenate([q_ref[rows, h * LANES:(h + 1) * LANES] for h in _B_ORDER], axis=0)
        s = _dot_nt(qs, kw) + tab_ref[...] + row_mask
        m = jnp.maximum(jnp.max(s, axis=-1, keepdims=True), sink)
        p = jnp.exp(s - m)
        denom = jnp.sum(p, axis=-1, keepdims=True) + jnp.exp(sink - m)
        pb = p.astype(BF16)
        o03 = _dot(pb[0:half], vw) / denom[0:half]
        o12 = _dot(pb[half:], vsw) / denom[half:]
        o_ref[rows, 0:LANES] = jnp.where(lane < B_HEAD_DIM, o03[0:BLOCK], o12[0:BLOCK]).astype(BF16)
        o_ref[rows, LANES:2 * LANES] = jnp.where(lane < B_HEAD_DIM, o12[BLOCK:], o03[BLOCK:]).astype(BF16)


def _attn_b(bq, bk_pad, bv_pad, bvs_pad, tab, sinks):
    nb, lp, _ = bq.shape
    kvspec = pl.BlockSpec((None, lp + 2 * BLOCK, LANES), lambda b, t: (b, 0, 0))
    return pl.pallas_call(
        functools.partial(_attn_b_kernel, lp=lp),
        grid=(nb, lp // TILE),
        in_specs=[pl.BlockSpec(memory_space=pltpu.SMEM),
                  pl.BlockSpec((None, TILE, B_HEADS * LANES), lambda b, t: (b, t, 0)),
                  kvspec, kvspec, kvspec, _resident(tab.shape)],
        out_specs=pl.BlockSpec((None, TILE, B_HEADS * B_HEAD_DIM), lambda b, t: (b, t, 0)),
        out_shape=jax.ShapeDtypeStruct((nb, lp, B_HEADS * B_HEAD_DIM), BF16),
        compiler_params=_params(2),
        name="attn_b",
    )(sinks, bq, bk_pad, bv_pad, bvs_pad, tab)


def _t5_bucket(rel):
    half = N_BUCKETS // 2
    max_exact = half // 2
    ret = jnp.where(rel > 0, half, 0)
    n = jnp.abs(rel)
    nf = jnp.maximum(n, 1).astype(F32)
    large = max_exact + (jnp.log(nf / max_exact) / math.log(MAX_DISTANCE / max_exact)
                         * (half - max_exact)).astype(jnp.int32)
    large = jnp.minimum(large, half - 1)
    return ret + jnp.where(n < max_exact, n, large)


def _rot_cols(w):
    half = w.shape[-1] // 2
    return jnp.concatenate([-w[..., half:], w[..., :half]], axis=-1)


def _bias_of_rel(table, rel):
    bucket = _t5_bucket(rel)[None]
    out = jnp.zeros((table.shape[1],) + rel.shape, F32)
    for b in range(N_BUCKETS):
        out = jnp.where(bucket == b, table[b].reshape((-1,) + (1,) * rel.ndim), out)
    return out


def _bias_tables(rel_bias, lp):
    rb = rel_bias.astype(F32)
    rela = (jnp.arange(3 * TILE) - TILE)[None, :] - jnp.arange(TILE)[:, None]
    wide = _bias_of_rel(rb[:, :A_HEADS], rela)
    left, diag, right = (wide[:, :, d * TILE:(d + 1) * TILE] for d in range(3))
    far_a = rb[:, :A_HEADS][_t5_bucket(jnp.array([-2 * TILE, 2 * TILE]))].T
    lead = jnp.where(jnp.arange(TILE) >= META_START, 0.0, NEG_INF).astype(F32).reshape(1, 1, TILE)
    far_left = jnp.broadcast_to(far_a[:, 0][:, None, None], left.shape)
    far_right = jnp.broadcast_to(far_a[:, 1][:, None, None], left.shape)
    tab_a = jnp.stack([left, diag, right, far_left, far_right, left + lead, diag + lead, far_left + lead], axis=1)
    kmask = jnp.concatenate([jnp.zeros_like(lead), lead], axis=0)
    relb = (jnp.arange(3 * BLOCK) - BLOCK)[None, :] - jnp.arange(BLOCK)[:, None]
    tab_b = jnp.where((jnp.abs(relb) <= WINDOW)[None], _bias_of_rel(rb[:, A_HEADS:], relb), NEG_INF)
    tab_b = jnp.concatenate([tab_b[h] for h in _B_ORDER], axis=0)
    slot = jnp.arange(lp)
    inv = ROPE_THETA ** (-jnp.arange(0, C_ROPE_DIM, 2, dtype=F32) / C_ROPE_DIM)
    ang = (slot - META_START).astype(F32)[:, None] * inv[None, :]
    ang = jnp.concatenate([ang, ang], axis=-1)
    ones = jnp.ones((lp, C_NOPE_DIM), F32)
    zeros_n = jnp.zeros((lp, C_NOPE_DIM), F32)
    zeros_p = jnp.zeros((lp, LANES - C_NOPE_DIM - C_ROPE_DIM), F32)
    cos_t = jnp.concatenate([ones, jnp.cos(ang), zeros_p], axis=-1)
    sin_t = jnp.concatenate([zeros_n, jnp.sin(ang), zeros_p], axis=-1)
    return tab_a, tab_b, kmask, cos_t, sin_t


def _layer_weights(w_in, w_uq, w_ukv, w_out):
    d = w_in.shape[0]
    sizes = (512, 512, 512, 256, 128, 128, 256, 128, 32)
    offs = [0]
    for s in sizes:
        offs.append(offs[-1] + s)
    aq, ak, av, bq, bk, bv, cq, ckv, kr = (w_in[:, offs[i]:offs[i + 1]] for i in range(9))
    zeros64 = jnp.zeros((d, B_HEAD_DIM), F32)
    bq_pad = []
    for h in range(B_HEADS):
        qh = bq[:, h * B_HEAD_DIM:(h + 1) * B_HEAD_DIM] * (B_HEAD_DIM ** -0.5)
        bq_pad += [qh, zeros64] if h // (B_HEADS // B_KV_HEADS) == 0 else [zeros64, qh]
    bvs = jnp.concatenate([bv[:, B_HEAD_DIM:], bv[:, :B_HEAD_DIM]], axis=1)
    pad_lo = jnp.zeros((d, C_NOPE_DIM), F32)
    pad_hi = jnp.zeros((d, LANES - C_NOPE_DIM - C_ROPE_DIM), F32)
    w_big = jnp.concatenate(
        [aq * (A_QK_DIM ** -0.5), ak, av] + bq_pad + [bk, bv, bvs, cq, ckv,
                                                      pad_lo, kr, pad_hi, pad_lo, _rot_cols(kr), pad_hi],
        axis=1).astype(BF16)
    uq = w_uq.reshape(C_Q_RANK, C_HEADS, C_NOPE_DIM + C_ROPE_DIM)
    uq_nope, uq_rope = uq[..., :C_NOPE_DIM], uq[..., C_NOPE_DIM:]
    zq = jnp.zeros((C_Q_RANK, C_HEADS, LANES - C_NOPE_DIM - C_ROPE_DIM), F32)
    wuqm = jnp.concatenate([uq_nope, uq_rope, zq], axis=-1).reshape(C_Q_RANK, C_HEADS * LANES).astype(BF16)
    wuqr = jnp.concatenate([jnp.zeros_like(uq_nope), _rot_cols(uq_rope), zq], axis=-1
                           ).reshape(C_Q_RANK, C_HEADS * LANES).astype(BF16)
    ukv = w_ukv.reshape(C_KV_RANK, C_HEADS, C_NOPE_DIM + C_V_DIM)
    wukvk = jnp.concatenate([ukv[..., :C_NOPE_DIM], jnp.zeros((C_KV_RANK, C_HEADS, LANES - C_NOPE_DIM), F32)],
                            axis=-1).reshape(C_KV_RANK, C_HEADS * LANES).astype(BF16)
    wukvv = ukv[..., C_NOPE_DIM:].reshape(C_KV_RANK, C_HEADS * C_V_DIM).astype(BF16)
    n_a = A_HEADS * A_V_DIM
    n_b = B_HEADS * B_HEAD_DIM
    wo = w_out.astype(BF16)
    wuq = jnp.concatenate([wuqm, wuqr], axis=1)
    wukv = jnp.concatenate([wukvk, wukvv], axis=1)
    return w_big, wuq, wukv, wo[:n_a], wo[n_a:n_a + n_b], wo[n_a + n_b:]


def _trunk(x, meta, rel_bias, g_ffn1, w_ffn1_gu, w_ffn1_down, g_mix, w_in, diff_lambda, g_subln, sinks, g_cq,
           g_ckv, w_uq, w_ukv, w_out, g_ffn2, w_ffn2_gu, w_ffn2_down, g_final):
    nb, seq, d = x.shape
    lp = BLOCK + seq
    assert lp % TILE == 0 and d == D_MODEL
    lead = jnp.concatenate([jnp.zeros((META_START, d), x.dtype), meta.astype(x.dtype)], axis=0)
    h = jnp.concatenate([jnp.broadcast_to(lead[None], (nb, BLOCK, d)), x], axis=1)
    tab_a, tab_b, kmask, cos_t, sin_t = _bias_tables(rel_bias, lp)
    pad = ((0, 0), (BLOCK, BLOCK), (0, 0))
    for l in range(DEPTH):
        w_big, wuq, wukv, wo_a, wo_b, wo_c = _layer_weights(w_in[l], w_uq[l], w_ukv[l], w_out[l])
        h = _ffn(h, g_ffn1[l], w_ffn1_gu[l].astype(BF16), w_ffn1_down[l].astype(BF16))
        aq, ak, av, bq, bk, bv, bvs, qc, kc, vc = _inproj(h, g_mix[l], w_big, g_cq[l], g_ckv[l], wuq, wukv,
                                                           cos_t, sin_t)
        lam_init = 0.8 - 0.6 * math.exp(-0.3 * l)
        oa = _attn_a(aq, ak, av, tab_a, diff_lambda[l].astype(F32), g_subln[l], lam_init)
        ob = _attn_b(bq, jnp.pad(bk, pad), jnp.pad(bv, pad), jnp.pad(bvs, pad), tab_b, sinks[l].astype(F32))
        oc = _attn_c(qc, kc, vc, kmask)
        h = _ffn(h, g_ffn2[l], w_ffn2_gu[l].astype(BF16), w_ffn2_down[l].astype(BF16),
                 mix=(oa, ob, oc, wo_a, wo_b, wo_c), g_final=g_final if l == DEPTH - 1 else None)
    return h[:, BLOCK:]


def kernel(x_prompt, x_sample, meta, rel_bias, g_ffn1, w_ffn1_gu, w_ffn1_down, g_mix, w_in, diff_lambda, g_subln,
           sinks, g_cq, g_ckv, w_uq, w_ukv, w_out, g_ffn2, w_ffn2_gu, w_ffn2_down, g_final):
    n_prompt = x_prompt.shape[0]
    x = jnp.concatenate([x_prompt, x_sample], axis=0)
    y = _trunk(x, meta, rel_bias, g_ffn1, w_ffn1_gu, w_ffn1_down, g_mix, w_in, diff_lambda, g_subln, sinks, g_cq,
               g_ckv, w_uq, w_ukv, w_out, g_ffn2, w_ffn2_gu, w_ffn2_down, g_final)
    return (y[:n_prompt], y[n_prompt:])
```

```python
import functools
import math

import jax
import jax.numpy as jnp
from jax import lax
from jax.experimental import pallas as pl
from jax.experimental.pallas import tpu as pltpu

D_MODEL = 1024
DEPTH = 2
N_META = 16
BLOCK = 128
META_START = BLOCK - N_META
A_HEADS = 4
A_QK_DIM = 64
A_V_DIM = 2 * A_QK_DIM
B_HEADS = 4
B_KV_HEADS = 2
B_HEAD_DIM = 64
WINDOW = 128
C_HEADS = 4
C_Q_RANK = 256
C_KV_RANK = 128
C_NOPE_DIM = 64
C_ROPE_DIM = 32
C_V_DIM = 64
ROPE_THETA = 10000.0
N_BUCKETS = 32
MAX_DISTANCE = 128
D_FF = 2816
EPS = 1e-6

LANES = 128
TILE = 5 * BLOCK
OUT_TILE = 4 * BLOCK
MXU_COLS = 256
FF_SPLITS = (0, 6 * MXU_COLS, D_FF)
ROW_BLOCK = TILE // 4
A_PIPE_UNROLL = 52
C_PIPE_UNROLL = 52
VMEM_LIMIT = 56 * 1024 * 1024

F32 = jnp.float32
BF16 = jnp.bfloat16
NEG_INF = float("-inf")

_W_AQ, _W_AK, _W_AV, _W_BQ = 0, 512, 1024, 1536
_W_BK, _W_BV, _W_BVS = 2048, 2176, 2304
_W_CQ, _W_CKV, _W_KR, _W_KRR = 2432, 2688, 2816, 2944
_W_TOTAL = 3072


def _params(n_axes):
    return pltpu.CompilerParams(dimension_semantics=("arbitrary",) * n_axes, vmem_limit_bytes=VMEM_LIMIT)


def _resident(shape):
    return pl.BlockSpec(shape, lambda *_: (0,) * len(shape), pipeline_mode=pl.Buffered(1))


def _rms(x, g):
    return x * lax.rsqrt(jnp.mean(x * x, axis=-1, keepdims=True) + EPS) * g


def _dot(a, b):
    return jnp.dot(a, b, preferred_element_type=F32)


def _dot_nt(a, b):
    return lax.dot_general(a, b, (((1,), (1,)), ((), ())), preferred_element_type=F32)


def _ffn_kernel(x_ref, *refs, mixed, final_norm):
    refs = list(refs)
    x = x_ref[...]
    if mixed:
        oa_ref, ob_ref, oc_ref, wa_ref, wb_ref, wc_ref = refs[:6]
        del refs[:6]
        x = x + _dot(oa_ref[...], wa_ref[...]) + _dot(ob_ref[...], wb_ref[...]) + _dot(oc_ref[...], wc_ref[...])
    g_ref, wgu_ref, wd_ref = refs[:3]
    gf_ref = refs[3] if final_norm else None
    o_ref = refs[-1]
    xn = _rms(x, g_ref[...]).astype(BF16)
    acc = None
    for lo, hi in zip(FF_SPLITS[:-1], FF_SPLITS[1:]):
        g = _dot(xn, wgu_ref[:, lo:hi])
        u = _dot(xn, wgu_ref[:, D_FF + lo:D_FF + hi])
        a = (g * jax.nn.sigmoid(g) * u).astype(BF16)
        d = _dot(a, wd_ref[lo:hi, :])
        acc = d if acc is None else acc + d
    y = x + 0.5 * acc
    if final_norm:
        y = _rms(y, gf_ref[...])
    o_ref[...] = y


def _ffn(h, g, wgu, wd, mix=None, g_final=None, out_batches=None):
    nb, lp, d = h.shape
    if out_batches is None:
        first, count, n_rows, tile_rows = 0, nb, lp, TILE

        def rows(n):
            return pl.BlockSpec((None, TILE, n), lambda b, t: (b, t, 0))

        out_spec = rows(d)
    else:
        first, count = out_batches
        n_rows, tile_rows = lp - BLOCK, OUT_TILE

        def rows(n):
            return pl.BlockSpec((pl.Element(OUT_TILE), pl.Element(n)),
                                lambda b, t: (pl.multiple_of((b + first) * lp + BLOCK + t * OUT_TILE, BLOCK), 0))

        out_spec = pl.BlockSpec((None, OUT_TILE, d), lambda b, t: (b, t, 0))
        h = h.reshape(nb * lp, d)
        if mix is not None:
            mix = tuple(o.reshape(nb * lp, o.shape[-1]) for o in mix[:3]) + tuple(mix[3:])
    assert n_rows % tile_rows == 0

    tile = rows(d)
    in_specs = [tile]
    args = [h]
    if mix is not None:
        in_specs += [rows(o.shape[-1]) for o in mix[:3]] + [_resident(w.shape) for w in mix[3:]]
        args += list(mix)
    in_specs += [_resident((1, d)), _resident(wgu.shape), _resident(wd.shape)]
    args += [g.reshape(1, d), wgu, wd]
    if g_final is not None:
        in_specs.append(_resident((1, d)))
        args.append(g_final.reshape(1, d))
    return pl.pallas_call(
        functools.partial(_ffn_kernel, mixed=mix is not None, final_norm=g_final is not None),
        grid=(count, n_rows // tile_rows),
        in_specs=in_specs,
        out_specs=out_spec,
        out_shape=jax.ShapeDtypeStruct((count, n_rows, d), F32),
        compiler_params=_params(2),
        name="ffn",
    )(*args)


def _inproj_kernel(h_ref, g_ref, w_ref, gcq_ref, gckv_ref, wuq_ref, wukv_ref, cos_ref, sin_ref,
                   aq_ref, ak_ref, av_ref, bq_ref, bk_ref, bv_ref, bvs_ref, qc_ref, kc_ref, vc_ref, *, c_scale):
    xn = _rms(h_ref[...], g_ref[...]).astype(BF16)

    wide = 2 * _W_AK
    y = _dot(xn, w_ref[:, 0:wide])
    aq_ref[...] = y[:, _W_AQ:_W_AK].astype(BF16)
    ak_ref[...] = y[:, _W_AK:wide].astype(BF16)
    y = _dot(xn, w_ref[:, wide:2 * wide])
    av_ref[...] = y[:, 0:_W_BQ - _W_AV].astype(BF16)
    bq_ref[...] = y[:, _W_BQ - _W_AV:wide].astype(BF16)
    y = _dot(xn, w_ref[:, _W_BK:_W_TOTAL])

    def piece(lo, n):
        return y[:, lo - _W_BK:lo - _W_BK + n]

    bk_ref[...] = piece(_W_BK, LANES).astype(BF16)
    bv_ref[...] = piece(_W_BV, LANES).astype(BF16)
    bvs_ref[...] = piece(_W_BVS, LANES).astype(BF16)

    cos = cos_ref[...]
    sin = sin_ref[...]
    cos4 = jnp.concatenate([cos] * C_HEADS, axis=1)
    sin4 = jnp.concatenate([sin] * C_HEADS, axis=1)
    cqn = _rms(piece(_W_CQ, C_Q_RANK), gcq_ref[...]).astype(BF16)
    q = _dot(cqn, wuq_ref[...])
    n_q = C_HEADS * LANES
    qc_ref[...] = ((q[:, 0:n_q] * cos4 + q[:, n_q:2 * n_q] * sin4) * c_scale).astype(BF16)

    ckvn = _rms(piece(_W_CKV, C_KV_RANK), gckv_ref[...]).astype(BF16)
    k_rope = piece(_W_KR, LANES) * cos + piece(_W_KRR, LANES) * sin
    kv = _dot(ckvn, wukv_ref[...])
    kc_ref[...] = (kv[:, 0:n_q] + jnp.concatenate([k_rope] * C_HEADS, axis=1)).astype(BF16)
    vc_ref[...] = kv[:, n_q:n_q + C_HEADS * C_V_DIM].astype(BF16)


def _inproj(h, g, w_big, gcq, gckv, wuq, wukv, cos_t, sin_t):
    nb, lp, d = h.shape

    def tile(n):
        return pl.BlockSpec((None, TILE, n), lambda b, t: (b, t, 0))

    pos = pl.BlockSpec((TILE, LANES), lambda b, t: (t, 0))
    widths = (512, 512, 512, 512, 128, 128, 128, 512, 512, 256)
    return pl.pallas_call(
        functools.partial(_inproj_kernel, c_scale=(C_NOPE_DIM + C_ROPE_DIM) ** -0.5),
        grid=(nb, lp // TILE),
        in_specs=[tile(d), _resident((1, d)), _resident(w_big.shape), _resident((1, C_Q_RANK)),
                  _resident((1, C_KV_RANK)), _resident(wuq.shape), _resident(wukv.shape), pos, pos],
        out_specs=[tile(n) for n in widths],
        out_shape=[jax.ShapeDtypeStruct((nb, lp, n), BF16) for n in widths],
        compiler_params=_params(2),
        name="inproj",
    )(h, g.reshape(1, d), w_big, gcq.reshape(1, -1), gckv.reshape(1, -1), wuq, wukv, cos_t, sin_t)


def _static(x):
    return isinstance(x, (int, bool))


def _aligned(x, m):
    return x if _static(x) else pl.multiple_of(x, m)


def _imin(a, b):
    return min(a, b) if _static(a) else jnp.minimum(a, b)


def _imax(a, b):
    return max(a, b) if _static(a) else jnp.maximum(a, b)


def _iselect(c, a, b):
    return (a if c else b) if _static(c) else jnp.where(c, a, b)


def _flash_scratch():
    return ([pltpu.VMEM((ROW_BLOCK, TILE), F32)] * 2 + [pltpu.VMEM((ROW_BLOCK, LANES), F32)] * 2
            + [pltpu.VMEM((ROW_BLOCK, TILE), BF16)] * 2 + [pltpu.VMEM((ROW_BLOCK, LANES), F32)] * 2
            + [pltpu.VMEM((2 * TILE, LANES), F32), pltpu.VMEM((2 * TILE, 2 * LANES), F32)])


def _flash_pipeline(n_tiles, scores, values, stat_row, scratch, max_unroll):
    s0, s1, x0, x1, p0, p1, a0, a1, m_ref, acc_ref = scratch
    s_bufs, x_bufs, p_bufs, a_bufs = (s0, s1), (x0, x1), (p0, p1), (a0, a1)
    unroll = max(k for k in range(2, max_unroll + 1, 2) if n_tiles % k == 0)
    m_ref[...] = jnp.full(m_ref.shape, NEG_INF, F32)
    acc_ref[...] = jnp.zeros(acc_ref.shape, F32)
    p1[...] = jnp.zeros(p1.shape, BF16)
    a1[...] = jnp.ones(a1.shape, F32)

    def issue(t, par):
        s = scores(t, par)
        s_bufs[par][...] = s
        x_bufs[par][...] = jnp.broadcast_to(jnp.max(s, axis=-1, keepdims=True), (ROW_BLOCK, LANES))

    def accumulate(t, par):
        rows = pl.ds(stat_row(t, par), ROW_BLOCK)
        alpha = a_bufs[par][...]
        acc_ref[rows, :] = (jnp.concatenate([alpha, alpha], axis=1) * acc_ref[rows, :]
                            + _dot(p_bufs[par][...], values(t, par)))

    def step(t, par):
        issue(_imin(t + 1, n_tiles - 1), 1 - par)
        rows = pl.ds(stat_row(t, par), ROW_BLOCK)
        m_old = m_ref[rows, :]
        m_new = jnp.maximum(m_old, x_bufs[par][...])
        p_bufs[par][...] = jnp.exp((s_bufs[par][...] - jnp.concatenate([m_new] * (TILE // LANES), axis=1)
                                    ).astype(BF16))
        a_bufs[par][...] = jnp.exp(m_old - m_new)
        m_ref[rows, :] = m_new
        accumulate(_imax(t - 1, 0), 1 - par)

    issue(0, 0)

    def body(u, carry):
        for k in range(unroll):
            step(unroll * u + k, k % 2)
        return carry

    if unroll == n_tiles:
        for k in range(n_tiles):
            step(k, k % 2)
    else:
        lax.fori_loop(0, n_tiles // unroll, body, 0)
    accumulate(n_tiles - 1, 1)


(_TAB_LEFT, _TAB_DIAG, _TAB_RIGHT, _TAB_FAR_LEFT, _TAB_FAR_RIGHT,
 _TAB_LEFT_LEAD, _TAB_DIAG_LEAD, _TAB_FAR_LEAD) = range(8)


def _ones_right(v):
    return jnp.concatenate([v, jnp.ones(v.shape, v.dtype)], axis=1)


def _attn_a_kernel(q_ref, k_ref, v_ref, tab_ref, dl_ref, gs_ref, o_ref, qq_ref, *scratch, lam_init, n_chunks):
    i = pl.program_id(2)
    q = q_ref[...]
    lane = lax.broadcasted_iota(jnp.int32, q.shape, 1)
    zero = jnp.zeros_like(q)
    qq_ref[0:TILE, :] = jnp.where(lane < A_QK_DIM, q, zero)
    qq_ref[TILE:2 * TILE, :] = jnp.where(lane >= A_QK_DIM, q, zero)
    blocks = 2 * TILE // ROW_BLOCK

    def q_row(t):
        return _aligned((t % blocks) * ROW_BLOCK, ROW_BLOCK)

    def k_row(t):
        return _aligned((t // blocks) * TILE, TILE)

    def scores(t, par):
        j = t // blocks
        lead = jnp.where(i == 0, _TAB_DIAG_LEAD, jnp.where(i == 1, _TAB_LEFT_LEAD, _TAB_FAR_LEAD))
        rest = jnp.where(j < i - 1, _TAB_FAR_LEFT, jnp.where(j > i + 1, _TAB_FAR_RIGHT, j - i + 1))
        table = jnp.where(j == 0, lead, rest)
        row = _aligned((t % (TILE // ROW_BLOCK)) * ROW_BLOCK, ROW_BLOCK)
        return (_dot_nt(qq_ref[pl.ds(q_row(t), ROW_BLOCK), :], k_ref[pl.ds(k_row(t), TILE), :])
                + tab_ref[table, pl.ds(row, ROW_BLOCK), :])

    def values(t, par):
        return _ones_right(v_ref[pl.ds(k_row(t), TILE), :])

    _flash_pipeline(n_chunks * blocks, scores, values, lambda t, par: q_row(t), scratch, A_PIPE_UNROLL)

    acc_ref = scratch[-1]
    o = acc_ref[:, 0:A_V_DIM] / acc_ref[:, A_V_DIM:2 * A_V_DIM]
    dl = dl_ref[...]
    lam = (jnp.exp(jnp.sum(dl[0:1] * dl[1:2], axis=-1, keepdims=True))
           - jnp.exp(jnp.sum(dl[2:3] * dl[3:4], axis=-1, keepdims=True)) + lam_init)
    w = o[0:TILE] - lam * o[TILE:2 * TILE]
    o_ref[...] = (_rms(w, gs_ref[...]) * (1.0 - lam_init)).astype(BF16)


def _attn_a(aq, ak, av, tab, dl, gs, lam_init):
    nb, lp, _ = aq.shape
    n_chunks = lp // TILE
    qspec = pl.BlockSpec((None, TILE, LANES), lambda b, h, t: (b, t, h))
    kvspec = pl.BlockSpec((None, lp, LANES), lambda b, h, t: (b, 0, h))
    return pl.pallas_call(
        functools.partial(_attn_a_kernel, lam_init=lam_init, n_chunks=n_chunks),
        grid=(nb, A_HEADS, n_chunks),
        in_specs=[qspec, kvspec, kvspec,
                  pl.BlockSpec((None,) + tab.shape[1:], lambda b, h, t: (h, 0, 0, 0)),
                  _resident((4, A_QK_DIM)), _resident((1, A_V_DIM))],
        out_specs=qspec,
        out_shape=jax.ShapeDtypeStruct((nb, lp, A_HEADS * A_V_DIM), BF16),
        scratch_shapes=[pltpu.VMEM((2 * TILE, LANES), BF16)] + _flash_scratch(),
        compiler_params=_params(3),
        name="attn_a",
    )(aq, ak, av, tab, dl, gs.reshape(1, A_V_DIM))


def _attn_c_kernel(q_ref, k_ref, v_ref, kmask_ref, o_ref, *scratch, n_chunks):
    blocks = 2 * TILE // ROW_BLOCK

    def q_row(t):
        return _aligned(((t % blocks) // 2) * ROW_BLOCK, ROW_BLOCK)

    def k_row(t):
        return _aligned((t // blocks) * TILE, TILE)

    def scores(t, par):
        cols = slice(par * LANES, (par + 1) * LANES)
        return (_dot_nt(q_ref[pl.ds(q_row(t), ROW_BLOCK), cols], k_ref[pl.ds(k_row(t), TILE), cols])
                + kmask_ref[_iselect(t < blocks, 1, 0)])

    def values(t, par):
        return _ones_right(v_ref[pl.ds(k_row(t), TILE), :])

    def stat_row(t, par):
        return _aligned(par * TILE + q_row(t), ROW_BLOCK)

    _flash_pipeline(n_chunks * blocks, scores, values, stat_row, scratch, C_PIPE_UNROLL)

    acc_ref = scratch[-1]
    o = acc_ref[:, 0:LANES] / acc_ref[:, LANES:2 * LANES]
    lane = lax.broadcasted_iota(jnp.int32, (TILE, LANES), 1)
    o_ref[...] = jnp.where(lane < C_V_DIM, o[0:TILE], o[TILE:2 * TILE]).astype(BF16)


def _attn_c(qc, kc, vc, kmask):
    nb, lp, _ = qc.shape
    n_chunks = lp // TILE
    return pl.pallas_call(
        functools.partial(_attn_c_kernel, n_chunks=n_chunks),
        grid=(nb, C_HEADS // 2, n_chunks),
        in_specs=[pl.BlockSpec((None, TILE, 2 * LANES), lambda b, p, t: (b, t, p)),
                  pl.BlockSpec((None, lp, 2 * LANES), lambda b, p, t: (b, 0, p)),
                  pl.BlockSpec((None, lp, LANES), lambda b, p, t: (b, 0, p)),
                  _resident(kmask.shape)],
        out_specs=pl.BlockSpec((None, TILE, LANES), lambda b, p, t: (b, t, p)),
        out_shape=jax.ShapeDtypeStruct((nb, lp, C_HEADS * C_V_DIM), BF16),
        scratch_shapes=_flash_scratch(),
        compiler_params=_params(3),
        name="attn_c",
    )(qc, kc, vc, kmask)


_B_ORDER = (0, 3, 1, 2)


def _attn_b_kernel(sink_ref, q_ref, k_ref, v_ref, vs_ref, tab_ref, o_ref, *, lp):
    t = pl.program_id(1)
    lane = lax.broadcasted_iota(jnp.int32, (BLOCK, LANES), 1)
    kk = lax.broadcasted_iota(jnp.int32, (1, 3 * BLOCK), 1)
    row = lax.broadcasted_iota(jnp.int32, (B_HEADS * BLOCK, 1), 0)
    sink = jnp.full((B_HEADS * BLOCK, 1), sink_ref[_B_ORDER[0]], F32)
    for n in range(1, B_HEADS):
        sink = jnp.where(row >= n * BLOCK, sink_ref[_B_ORDER[n]], sink)
    half = B_HEADS * BLOCK // 2
    for blk in range(TILE // BLOCK):
        gblk = t * (TILE // BLOCK) + blk
        row0 = pl.multiple_of(gblk * BLOCK, BLOCK)
        kw = k_ref[pl.ds(row0, 3 * BLOCK), :]
        vw = v_ref[pl.ds(row0, 3 * BLOCK), :]
        vsw = vs_ref[pl.ds(row0, 3 * BLOCK), :]
        kslot = (gblk - 1) * BLOCK + kk
        row_mask = jnp.where((kslot >= META_START) & (kslot < lp), 0.0, NEG_INF)
        rows = slice(blk * BLOCK, (blk + 1) * BLOCK)
        qs = jnp.concatenate([q_ref[rows, h * LANES:(h + 1) * LANES] for h in _B_ORDER], axis=0)
        s = _dot_nt(qs, kw) + tab_ref[...] + row_mask
        m = jnp.maximum(jnp.max(s, axis=-1, keepdims=True), sink)
        p = jnp.exp(s - m)
        denom = jnp.sum(p, axis=-1, keepdims=True) + jnp.exp(sink - m)
        pb = p.astype(BF16)
        o03 = _dot(pb[0:half], vw) / denom[0:half]
        o12 = _dot(pb[half:], vsw) / denom[half:]
        o_ref[rows, 0:LANES] = jnp.where(lane < B_HEAD_DIM, o03[0:BLOCK], o12[0:BLOCK]).astype(BF16)
        o_ref[rows, LANES:2 * LANES] = jnp.where(lane < B_HEAD_DIM, o12[BLOCK:], o03[BLOCK:]).astype(BF16)


def _attn_b(bq, bk_pad, bv_pad, bvs_pad, tab, sinks):
    nb, lp, _ = bq.shape
    kvspec = pl.BlockSpec((None, lp + 2 * BLOCK, LANES), lambda b, t: (b, 0, 0))
    return pl.pallas_call(
        functools.partial(_attn_b_kernel, lp=lp),
        grid=(nb, lp // TILE),
        in_specs=[pl.BlockSpec(memory_space=pltpu.SMEM),
                  pl.BlockSpec((None, TILE, B_HEADS * LANES), lambda b, t: (b, t, 0)),
                  kvspec, kvspec, kvspec, _resident(tab.shape)],
        out_specs=pl.BlockSpec((None, TILE, B_HEADS * B_HEAD_DIM), lambda b, t: (b, t, 0)),
        out_shape=jax.ShapeDtypeStruct((nb, lp, B_HEADS * B_HEAD_DIM), BF16),
        compiler_params=_params(2),
        name="attn_b",
    )(sinks, bq, bk_pad, bv_pad, bvs_pad, tab)


def _t5_bucket(rel):
    half = N_BUCKETS // 2
    max_exact = half // 2
    ret = jnp.where(rel > 0, half, 0)
    n = jnp.abs(rel)
    nf = jnp.maximum(n, 1).astype(F32)
    large = max_exact + (jnp.log(nf / max_exact) / math.log(MAX_DISTANCE / max_exact)
                         * (half - max_exact)).astype(jnp.int32)
    large = jnp.minimum(large, half - 1)
    return ret + jnp.where(n < max_exact, n, large)


def _rot_cols(w):
    half = w.shape[-1] // 2
    return jnp.concatenate([-w[..., half:], w[..., :half]], axis=-1)


def _bias_of_rel(table, rel):
    bucket = _t5_bucket(rel)[None]
    out = jnp.zeros((table.shape[1],) + rel.shape, F32)
    for b in range(N_BUCKETS):
        out = jnp.where(bucket == b, table[b].reshape((-1,) + (1,) * rel.ndim), out)
    return out


def _bias_tables(rel_bias, lp):
    rb = rel_bias.astype(F32)
    rela = (jnp.arange(3 * TILE) - TILE)[None, :] - jnp.arange(TILE)[:, None]
    wide = _bias_of_rel(rb[:, :A_HEADS], rela)
    left, diag, right = (wide[:, :, d * TILE:(d + 1) * TILE] for d in range(3))
    far_a = rb[:, :A_HEADS][_t5_bucket(jnp.array([-2 * TILE, 2 * TILE]))].T
    lead = jnp.where(jnp.arange(TILE) >= META_START, 0.0, NEG_INF).astype(F32).reshape(1, 1, TILE)
    far_left = jnp.broadcast_to(far_a[:, 0][:, None, None], left.shape)
    far_right = jnp.broadcast_to(far_a[:, 1][:, None, None], left.shape)
    tab_a = jnp.stack([left, diag, right, far_left, far_right, left + lead, diag + lead, far_left + lead], axis=1)
    kmask = jnp.concatenate([jnp.zeros_like(lead), lead], axis=0)
    relb = (jnp.arange(3 * BLOCK) - BLOCK)[None, :] - jnp.arange(BLOCK)[:, None]
    tab_b = jnp.where((jnp.abs(relb) <= WINDOW)[None], _bias_of_rel(rb[:, A_HEADS:], relb), NEG_INF)
    tab_b = jnp.concatenate([tab_b[h] for h in _B_ORDER], axis=0)
    slot = jnp.arange(lp)
    inv = ROPE_THETA ** (-jnp.arange(0, C_ROPE_DIM, 2, dtype=F32) / C_ROPE_DIM)
    ang = (slot - META_START).astype(F32)[:, None] * inv[None, :]
    ang = jnp.concatenate([ang, ang], axis=-1)
    ones = jnp.ones((lp, C_NOPE_DIM), F32)
    zeros_n = jnp.zeros((lp, C_NOPE_DIM), F32)
    zeros_p = jnp.zeros((lp, LANES - C_NOPE_DIM - C_ROPE_DIM), F32)
    cos_t = jnp.concatenate([ones, jnp.cos(ang), zeros_p], axis=-1)
    sin_t = jnp.concatenate([zeros_n, jnp.sin(ang), zeros_p], axis=-1)
    return tab_a, tab_b, kmask, cos_t, sin_t


def _layer_weights(w_in, w_uq, w_ukv, w_out):
    d = w_in.shape[0]
    sizes = (512, 512, 512, 256, 128, 128, 256, 128, 32)
    offs = [0]
    for s in sizes:
        offs.append(offs[-1] + s)
    aq, ak, av, bq, bk, bv, cq, ckv, kr = (w_in[:, offs[i]:offs[i + 1]] for i in range(9))
    zeros64 = jnp.zeros((d, B_HEAD_DIM), F32)
    bq_pad = []
    for h in range(B_HEADS):
        qh = bq[:, h * B_HEAD_DIM:(h + 1) * B_HEAD_DIM] * (B_HEAD_DIM ** -0.5)
        bq_pad += [qh, zeros64] if h // (B_HEADS // B_KV_HEADS) == 0 else [zeros64, qh]
    bvs = jnp.concatenate([bv[:, B_HEAD_DIM:], bv[:, :B_HEAD_DIM]], axis=1)
    pad_lo = jnp.zeros((d, C_NOPE_DIM), F32)
    pad_hi = jnp.zeros((d, LANES - C_NOPE_DIM - C_ROPE_DIM), F32)
    w_big = jnp.concatenate(
        [aq * (A_QK_DIM ** -0.5), ak, av] + bq_pad + [bk, bv, bvs, cq, ckv,
                                                      pad_lo, kr, pad_hi, pad_lo, _rot_cols(kr), pad_hi],
        axis=1).astype(BF16)
    uq = w_uq.reshape(C_Q_RANK, C_HEADS, C_NOPE_DIM + C_ROPE_DIM)
    uq_nope, uq_rope = uq[..., :C_NOPE_DIM], uq[..., C_NOPE_DIM:]
    zq = jnp.zeros((C_Q_RANK, C_HEADS, LANES - C_NOPE_DIM - C_ROPE_DIM), F32)
    wuqm = jnp.concatenate([uq_nope, uq_rope, zq], axis=-1).reshape(C_Q_RANK, C_HEADS * LANES).astype(BF16)
    wuqr = jnp.concatenate([jnp.zeros_like(uq_nope), _rot_cols(uq_rope), zq], axis=-1
                           ).reshape(C_Q_RANK, C_HEADS * LANES).astype(BF16)
    ukv = w_ukv.reshape(C_KV_RANK, C_HEADS, C_NOPE_DIM + C_V_DIM)
    wukvk = jnp.concatenate([ukv[..., :C_NOPE_DIM], jnp.zeros((C_KV_RANK, C_HEADS, LANES - C_NOPE_DIM), F32)],
                            axis=-1).reshape(C_KV_RANK, C_HEADS * LANES).astype(BF16)
    wukvv = ukv[..., C_NOPE_DIM:].reshape(C_KV_RANK, C_HEADS * C_V_DIM).astype(BF16)
    n_a = A_HEADS * A_V_DIM
    n_b = B_HEADS * B_HEAD_DIM
    wo = w_out.astype(BF16)
    wuq = jnp.concatenate([wuqm, wuqr], axis=1)
    wukv = jnp.concatenate([wukvk, wukvv], axis=1)
    return w_big, wuq, wukv, wo[:n_a], wo[n_a:n_a + n_b], wo[n_a + n_b:]


def _trunk(x, meta, rel_bias, g_ffn1, w_ffn1_gu, w_ffn1_down, g_mix, w_in, diff_lambda, g_subln, sinks, g_cq,
           g_ckv, w_uq, w_ukv, w_out, g_ffn2, w_ffn2_gu, w_ffn2_down, g_final, groups=None):
    nb, seq, d = x.shape
    lp = BLOCK + seq
    assert lp % TILE == 0 and d == D_MODEL
    groups = ((0, nb),) if groups is None else groups
    lead = jnp.concatenate([jnp.zeros((META_START, d), x.dtype), meta.astype(x.dtype)], axis=0)
    h = jnp.concatenate([jnp.broadcast_to(lead[None], (nb, BLOCK, d)), x], axis=1)
    tab_a, tab_b, kmask, cos_t, sin_t = _bias_tables(rel_bias, lp)
    pad = ((0, 0), (BLOCK, BLOCK), (0, 0))
    for l in range(DEPTH):
        w_big, wuq, wukv, wo_a, wo_b, wo_c = _layer_weights(w_in[l], w_uq[l], w_ukv[l], w_out[l])
        h = _ffn(h, g_ffn1[l], w_ffn1_gu[l].astype(BF16), w_ffn1_down[l].astype(BF16))
        aq, ak, av, bq, bk, bv, bvs, qc, kc, vc = _inproj(h, g_mix[l], w_big, g_cq[l], g_ckv[l], wuq, wukv,
                                                           cos_t, sin_t)
        lam_init = 0.8 - 0.6 * math.exp(-0.3 * l)
        oa = _attn_a(aq, ak, av, tab_a, diff_lambda[l].astype(F32), g_subln[l], lam_init)
        ob = _attn_b(bq, jnp.pad(bk, pad), jnp.pad(bv, pad), jnp.pad(bvs, pad), tab_b, sinks[l].astype(F32))
        oc = _attn_c(qc, kc, vc, kmask)
        ffn2 = functools.partial(_ffn, h, g_ffn2[l], w_ffn2_gu[l].astype(BF16), w_ffn2_down[l].astype(BF16),
                                 mix=(oa, ob, oc, wo_a, wo_b, wo_c))
        if l < DEPTH - 1:
            h = ffn2()
    return tuple(ffn2(g_final=g_final, out_batches=grp) for grp in groups)


def kernel(x_prompt, x_sample, meta, rel_bias, g_ffn1, w_ffn1_gu, w_ffn1_down, g_mix, w_in, diff_lambda, g_subln,
           sinks, g_cq, g_ckv, w_uq, w_ukv, w_out, g_ffn2, w_ffn2_gu, w_ffn2_down, g_final):
    n_prompt = x_prompt.shape[0]
    x = jnp.concatenate([x_prompt, x_sample], axis=0)
    return _trunk(x, meta, rel_bias, g_ffn1, w_ffn1_gu, w_ffn1_down, g_mix, w_in, diff_lambda, g_subln, sinks, g_cq,
                  g_ckv, w_uq, w_ukv, w_out, g_ffn2, w_ffn2_gu, w_ffn2_down, g_final,
                  groups=((0, n_prompt), (n_prompt, x_sample.shape[0])))
```

```python
import functools
import math

import jax
import jax.numpy as jnp
from jax import lax
from jax.experimental import pallas as pl
from jax.experimental.pallas import tpu as pltpu

D_MODEL = 1024
DEPTH = 2
N_META = 16
BLOCK = 128
META_START = BLOCK - N_META
A_HEADS = 4
A_QK_DIM = 64
A_V_DIM = 2 * A_QK_DIM
B_HEADS = 4
B_KV_HEADS = 2
B_HEAD_DIM = 64
WINDOW = 128
C_HEADS = 4
C_Q_RANK = 256
C_KV_RANK = 128
C_NOPE_DIM = 64
C_ROPE_DIM = 32
C_V_DIM = 64
ROPE_THETA = 10000.0
N_BUCKETS = 32
MAX_DISTANCE = 128
D_FF = 2816
EPS = 1e-6

LANES = 128
TILE = 5 * BLOCK
OUT_TILE = 4 * BLOCK
MXU_COLS = 256
FF_SPLITS = (0, 6 * MXU_COLS, D_FF)
ROW_BLOCK = TILE // 4
A_PIPE_UNROLL = 104
C_PIPE_UNROLL = 104
VMEM_LIMIT = 56 * 1024 * 1024

F32 = jnp.float32
BF16 = jnp.bfloat16
NEG_INF = float("-inf")

_W_AQ, _W_AK, _W_AV, _W_BQ = 0, 512, 1024, 1536
_W_BK, _W_BV, _W_BVS = 2048, 2176, 2304
_W_CQ, _W_CKV, _W_KR, _W_KRR = 2432, 2688, 2816, 2944
_W_TOTAL = 3072


def _params(n_axes):
    return pltpu.CompilerParams(dimension_semantics=("arbitrary",) * n_axes, vmem_limit_bytes=VMEM_LIMIT)


def _resident(shape):
    return pl.BlockSpec(shape, lambda *_: (0,) * len(shape), pipeline_mode=pl.Buffered(1))


def _rms(x, g):
    return x * lax.rsqrt(jnp.mean(x * x, axis=-1, keepdims=True) + EPS) * g


def _dot(a, b):
    return jnp.dot(a, b, preferred_element_type=F32)


def _dot_nt(a, b):
    return lax.dot_general(a, b, (((1,), (1,)), ((), ())), preferred_element_type=F32)


def _ffn_kernel(x_ref, *refs, mixed, final_norm):
    refs = list(refs)
    x = x_ref[...]
    if mixed:
        oa_ref, ob_ref, oc_ref, wa_ref, wb_ref, wc_ref = refs[:6]
        del refs[:6]
        x = x + _dot(oa_ref[...], wa_ref[...]) + _dot(ob_ref[...], wb_ref[...]) + _dot(oc_ref[...], wc_ref[...])
    g_ref, wgu_ref, wd_ref = refs[:3]
    gf_ref = refs[3] if final_norm else None
    o_ref = refs[-1]
    xn = _rms(x, g_ref[...]).astype(BF16)
    acc = None
    for lo, hi in zip(FF_SPLITS[:-1], FF_SPLITS[1:]):
        g = _dot(xn, wgu_ref[:, lo:hi])
        u = _dot(xn, wgu_ref[:, D_FF + lo:D_FF + hi])
        a = (g * jax.nn.sigmoid(g) * u).astype(BF16)
        d = _dot(a, wd_ref[lo:hi, :])
        acc = d if acc is None else acc + d
    y = x + 0.5 * acc
    if final_norm:
        y = _rms(y, gf_ref[...])
    o_ref[...] = y


def _ffn(h, g, wgu, wd, mix=None, g_final=None, out_batches=None):
    nb, lp, d = h.shape
    if out_batches is None:
        first, count, n_rows, tile_rows = 0, nb, lp, TILE

        def rows(n):
            return pl.BlockSpec((None, TILE, n), lambda b, t: (b, t, 0))

        out_spec = rows(d)
    else:
        first, count = out_batches
        n_rows, tile_rows = lp - BLOCK, OUT_TILE

        def rows(n):
            return pl.BlockSpec((pl.Element(OUT_TILE), pl.Element(n)),
                                lambda b, t: (pl.multiple_of((b + first) * lp + BLOCK + t * OUT_TILE, BLOCK), 0))

        out_spec = pl.BlockSpec((None, OUT_TILE, d), lambda b, t: (b, t, 0))
        h = h.reshape(nb * lp, d)
        if mix is not None:
            mix = tuple(o.reshape(nb * lp, o.shape[-1]) for o in mix[:3]) + tuple(mix[3:])
    assert n_rows % tile_rows == 0

    tile = rows(d)
    in_specs = [tile]
    args = [h]
    if mix is not None:
        in_specs += [rows(o.shape[-1]) for o in mix[:3]] + [_resident(w.shape) for w in mix[3:]]
        args += list(mix)
    in_specs += [_resident((1, d)), _resident(wgu.shape), _resident(wd.shape)]
    args += [g.reshape(1, d), wgu, wd]
    if g_final is not None:
        in_specs.append(_resident((1, d)))
        args.append(g_final.reshape(1, d))
    return pl.pallas_call(
        functools.partial(_ffn_kernel, mixed=mix is not None, final_norm=g_final is not None),
        grid=(count, n_rows // tile_rows),
        in_specs=in_specs,
        out_specs=out_spec,
        out_shape=jax.ShapeDtypeStruct((count, n_rows, d), F32),
        compiler_params=_params(2),
        name="ffn",
    )(*args)


def _inproj_kernel(h_ref, g_ref, w_ref, gcq_ref, gckv_ref, wuq_ref, wukv_ref, cos_ref, sin_ref,
                   aq_ref, ak_ref, av_ref, bq_ref, bk_ref, bv_ref, bvs_ref, qc_ref, kc_ref, vc_ref, *, c_scale):
    xn = _rms(h_ref[...], g_ref[...]).astype(BF16)

    wide = 2 * _W_AK
    y = _dot(xn, w_ref[:, 0:wide])
    aq_ref[...] = y[:, _W_AQ:_W_AK].astype(BF16)
    ak_ref[...] = y[:, _W_AK:wide].astype(BF16)
    y = _dot(xn, w_ref[:, wide:2 * wide])
    av_ref[...] = y[:, 0:_W_BQ - _W_AV].astype(BF16)
    bq_ref[...] = y[:, _W_BQ - _W_AV:wide].astype(BF16)
    y = _dot(xn, w_ref[:, _W_BK:_W_TOTAL])

    def piece(lo, n):
        return y[:, lo - _W_BK:lo - _W_BK + n]

    bk_ref[...] = piece(_W_BK, LANES).astype(BF16)
    bv_ref[...] = piece(_W_BV, LANES).astype(BF16)
    bvs_ref[...] = piece(_W_BVS, LANES).astype(BF16)

    cos = cos_ref[...]
    sin = sin_ref[...]
    cos4 = jnp.concatenate([cos] * C_HEADS, axis=1)
    sin4 = jnp.concatenate([sin] * C_HEADS, axis=1)
    cqn = _rms(piece(_W_CQ, C_Q_RANK), gcq_ref[...]).astype(BF16)
    q = _dot(cqn, wuq_ref[...])
    n_q = C_HEADS * LANES
    qc_ref[...] = ((q[:, 0:n_q] * cos4 + q[:, n_q:2 * n_q] * sin4) * c_scale).astype(BF16)

    ckvn = _rms(piece(_W_CKV, C_KV_RANK), gckv_ref[...]).astype(BF16)
    k_rope = piece(_W_KR, LANES) * cos + piece(_W_KRR, LANES) * sin
    kv = _dot(ckvn, wukv_ref[...])
    kc_ref[...] = (kv[:, 0:n_q] + jnp.concatenate([k_rope] * C_HEADS, axis=1)).astype(BF16)
    vc_ref[...] = kv[:, n_q:n_q + C_HEADS * C_V_DIM].astype(BF16)


def _inproj(h, g, w_big, gcq, gckv, wuq, wukv, cos_t, sin_t):
    nb, lp, d = h.shape

    def tile(n):
        return pl.BlockSpec((None, TILE, n), lambda b, t: (b, t, 0))

    pos = pl.BlockSpec((TILE, LANES), lambda b, t: (t, 0))
    widths = (512, 512, 512, 512, 128, 128, 128, 512, 512, 256)
    return pl.pallas_call(
        functools.partial(_inproj_kernel, c_scale=(C_NOPE_DIM + C_ROPE_DIM) ** -0.5),
        grid=(nb, lp // TILE),
        in_specs=[tile(d), _resident((1, d)), _resident(w_big.shape), _resident((1, C_Q_RANK)),
                  _resident((1, C_KV_RANK)), _resident(wuq.shape), _resident(wukv.shape), pos, pos],
        out_specs=[tile(n) for n in widths],
        out_shape=[jax.ShapeDtypeStruct((nb, lp, n), BF16) for n in widths],
        compiler_params=_params(2),
        name="inproj",
    )(h, g.reshape(1, d), w_big, gcq.reshape(1, -1), gckv.reshape(1, -1), wuq, wukv, cos_t, sin_t)


def _static(x):
    return isinstance(x, (int, bool))


def _aligned(x, m):
    return x if _static(x) else pl.multiple_of(x, m)


def _imin(a, b):
    return min(a, b) if _static(a) else jnp.minimum(a, b)


def _imax(a, b):
    return max(a, b) if _static(a) else jnp.maximum(a, b)


def _iselect(c, a, b):
    return (a if c else b) if _static(c) else jnp.where(c, a, b)


def _flash_scratch():
    return ([pltpu.VMEM((ROW_BLOCK, TILE), F32)] * 2 + [pltpu.VMEM((ROW_BLOCK, LANES), F32)] * 2
            + [pltpu.VMEM((ROW_BLOCK, TILE), BF16)] * 2 + [pltpu.VMEM((ROW_BLOCK, LANES), F32)] * 2
            + [pltpu.VMEM((2 * TILE, LANES), F32), pltpu.VMEM((2 * TILE, 2 * LANES), F32)])


def _flash_pipeline(n_tiles, scores, values, stat_row, scratch, max_unroll):
    s0, s1, x0, x1, p0, p1, a0, a1, m_ref, acc_ref = scratch
    s_bufs, x_bufs, p_bufs, a_bufs = (s0, s1), (x0, x1), (p0, p1), (a0, a1)
    unroll = max(k for k in range(2, max_unroll + 1, 2) if n_tiles % k == 0)
    m_ref[...] = jnp.full(m_ref.shape, NEG_INF, F32)
    acc_ref[...] = jnp.zeros(acc_ref.shape, F32)
    p1[...] = jnp.zeros(p1.shape, BF16)
    a1[...] = jnp.ones(a1.shape, F32)

    def issue(t, par):
        s = scores(t, par)
        s_bufs[par][...] = s
        x_bufs[par][...] = jnp.broadcast_to(jnp.max(s, axis=-1, keepdims=True), (ROW_BLOCK, LANES))

    def accumulate(t, par):
        rows = pl.ds(stat_row(t, par), ROW_BLOCK)
        alpha = a_bufs[par][...]
        acc_ref[rows, :] = (jnp.concatenate([alpha, alpha], axis=1) * acc_ref[rows, :]
                            + _dot(p_bufs[par][...], values(t, par)))

    def step(t, par):
        issue(_imin(t + 1, n_tiles - 1), 1 - par)
        rows = pl.ds(stat_row(t, par), ROW_BLOCK)
        m_old = m_ref[rows, :]
        m_new = jnp.maximum(m_old, x_bufs[par][...])
        p_bufs[par][...] = jnp.exp((s_bufs[par][...] - jnp.concatenate([m_new] * (TILE // LANES), axis=1)
                                    ).astype(BF16))
        a_bufs[par][...] = jnp.exp(m_old - m_new)
        m_ref[rows, :] = m_new
        accumulate(_imax(t - 1, 0), 1 - par)

    issue(0, 0)

    def body(u, carry):
        for k in range(unroll):
            step(unroll * u + k, k % 2)
        return carry

    if unroll == n_tiles:
        for k in range(n_tiles):
            step(k, k % 2)
    else:
        lax.fori_loop(0, n_tiles // unroll, body, 0)
    accumulate(n_tiles - 1, 1)


(_TAB_LEFT, _TAB_DIAG, _TAB_RIGHT, _TAB_FAR_LEFT, _TAB_FAR_RIGHT,
 _TAB_LEFT_LEAD, _TAB_DIAG_LEAD, _TAB_FAR_LEAD) = range(8)


def _ones_right(v):
    return jnp.concatenate([v, jnp.ones(v.shape, v.dtype)], axis=1)


def _attn_a_kernel(q_ref, k_ref, v_ref, tab_ref, dl_ref, gs_ref, o_ref, qq_ref, *scratch, lam_init, n_chunks):
    i = pl.program_id(2)
    q = q_ref[...]
    lane = lax.broadcasted_iota(jnp.int32, q.shape, 1)
    zero = jnp.zeros_like(q)
    qq_ref[0:TILE, :] = jnp.where(lane < A_QK_DIM, q, zero)
    qq_ref[TILE:2 * TILE, :] = jnp.where(lane >= A_QK_DIM, q, zero)
    blocks = 2 * TILE // ROW_BLOCK

    def q_row(t):
        return _aligned((t % blocks) * ROW_BLOCK, ROW_BLOCK)

    def k_row(t):
        return _aligned((t // blocks) * TILE, TILE)

    def scores(t, par):
        j = t // blocks
        lead = jnp.where(i == 0, _TAB_DIAG_LEAD, jnp.where(i == 1, _TAB_LEFT_LEAD, _TAB_FAR_LEAD))
        rest = jnp.where(j < i - 1, _TAB_FAR_LEFT, jnp.where(j > i + 1, _TAB_FAR_RIGHT, j - i + 1))
        table = jnp.where(j == 0, lead, rest)
        row = _aligned((t % (TILE // ROW_BLOCK)) * ROW_BLOCK, ROW_BLOCK)
        return (_dot_nt(qq_ref[pl.ds(q_row(t), ROW_BLOCK), :], k_ref[pl.ds(k_row(t), TILE), :])
                + tab_ref[table, pl.ds(row, ROW_BLOCK), :])

    def values(t, par):
        return _ones_right(v_ref[pl.ds(k_row(t), TILE), :])

    _flash_pipeline(n_chunks * blocks, scores, values, lambda t, par: q_row(t), scratch, A_PIPE_UNROLL)

    acc_ref = scratch[-1]
    o = acc_ref[:, 0:A_V_DIM] / acc_ref[:, A_V_DIM:2 * A_V_DIM]
    dl = dl_ref[...]
    lam = (jnp.exp(jnp.sum(dl[0:1] * dl[1:2], axis=-1, keepdims=True))
           - jnp.exp(jnp.sum(dl[2:3] * dl[3:4], axis=-1, keepdims=True)) + lam_init)
    w = o[0:TILE] - lam * o[TILE:2 * TILE]
    o_ref[...] = (_rms(w, gs_ref[...]) * (1.0 - lam_init)).astype(BF16)


def _attn_a(aq, ak, av, tab, dl, gs, lam_init):
    nb, lp, _ = aq.shape
    n_chunks = lp // TILE
    qspec = pl.BlockSpec((None, TILE, LANES), lambda b, h, t: (b, t, h))
    kvspec = pl.BlockSpec((None, lp, LANES), lambda b, h, t: (b, 0, h))
    return pl.pallas_call(
        functools.partial(_attn_a_kernel, lam_init=lam_init, n_chunks=n_chunks),
        grid=(nb, A_HEADS, n_chunks),
        in_specs=[qspec, kvspec, kvspec,
                  pl.BlockSpec((None,) + tab.shape[1:], lambda b, h, t: (h, 0, 0, 0)),
                  _resident((4, A_QK_DIM)), _resident((1, A_V_DIM))],
        out_specs=qspec,
        out_shape=jax.ShapeDtypeStruct((nb, lp, A_HEADS * A_V_DIM), BF16),
        scratch_shapes=[pltpu.VMEM((2 * TILE, LANES), BF16)] + _flash_scratch(),
        compiler_params=_params(3),
        name="attn_a",
    )(aq, ak, av, tab, dl, gs.reshape(1, A_V_DIM))


def _attn_c_kernel(q_ref, k_ref, v_ref, kmask_ref, o_ref, *scratch, n_chunks):
    blocks = 2 * TILE // ROW_BLOCK

    def q_row(t):
        return _aligned(((t % blocks) // 2) * ROW_BLOCK, ROW_BLOCK)

    def k_row(t):
        return _aligned((t // blocks) * TILE, TILE)

    def scores(t, par):
        cols = slice(par * LANES, (par + 1) * LANES)
        return (_dot_nt(q_ref[pl.ds(q_row(t), ROW_BLOCK), cols], k_ref[pl.ds(k_row(t), TILE), cols])
                + kmask_ref[_iselect(t < blocks, 1, 0)])

    def values(t, par):
        return _ones_right(v_ref[pl.ds(k_row(t), TILE), :])

    def stat_row(t, par):
        return _aligned(par * TILE + q_row(t), ROW_BLOCK)

    _flash_pipeline(n_chunks * blocks, scores, values, stat_row, scratch, C_PIPE_UNROLL)

    acc_ref = scratch[-1]
    o = acc_ref[:, 0:LANES] / acc_ref[:, LANES:2 * LANES]
    lane = lax.broadcasted_iota(jnp.int32, (TILE, LANES), 1)
    o_ref[...] = jnp.where(lane < C_V_DIM, o[0:TILE], o[TILE:2 * TILE]).astype(BF16)


def _attn_c(qc, kc, vc, kmask):
    nb, lp, _ = qc.shape
    n_chunks = lp // TILE
    return pl.pallas_call(
        functools.partial(_attn_c_kernel, n_chunks=n_chunks),
        grid=(nb, C_HEADS // 2, n_chunks),
        in_specs=[pl.BlockSpec((None, TILE, 2 * LANES), lambda b, p, t: (b, t, p)),
                  pl.BlockSpec((None, lp, 2 * LANES), lambda b, p, t: (b, 0, p)),
                  pl.BlockSpec((None, lp, LANES), lambda b, p, t: (b, 0, p)),
                  _resident(kmask.shape)],
        out_specs=pl.BlockSpec((None, TILE, LANES), lambda b, p, t: (b, t, p)),
        out_shape=jax.ShapeDtypeStruct((nb, lp, C_HEADS * C_V_DIM), BF16),
        scratch_shapes=_flash_scratch(),
        compiler_params=_params(3),
        name="attn_c",
    )(qc, kc, vc, kmask)


_B_ORDER = (0, 3, 1, 2)


def _attn_b_kernel(sink_ref, q_ref, k_ref, v_ref, vs_ref, tab_ref, o_ref, *, lp):
    t = pl.program_id(1)
    lane = lax.broadcasted_iota(jnp.int32, (BLOCK, LANES), 1)
    kk = lax.broadcasted_iota(jnp.int32, (1, 3 * BLOCK), 1)
    row = lax.broadcasted_iota(jnp.int32, (B_HEADS * BLOCK, 1), 0)
    sink = jnp.full((B_HEADS * BLOCK, 1), sink_ref[_B_ORDER[0]], F32)
    for n in range(1, B_HEADS):
        sink = jnp.where(row >= n * BLOCK, sink_ref[_B_ORDER[n]], sink)
    half = B_HEADS * BLOCK // 2
    for blk in range(TILE // BLOCK):
        gblk = t * (TILE // BLOCK) + blk
        row0 = pl.multiple_of(gblk * BLOCK, BLOCK)
        kw = k_ref[pl.ds(row0, 3 * BLOCK), :]
        vw = v_ref[pl.ds(row0, 3 * BLOCK), :]
        vsw = vs_ref[pl.ds(row0, 3 * BLOCK), :]
        kslot = (gblk - 1) * BLOCK + kk
        row_mask = jnp.where((kslot >= META_START) & (kslot < lp), 0.0, NEG_INF)
        rows = slice(blk * BLOCK, (blk + 1) * BLOCK)
        qs = jnp.concatenate([q_ref[rows, h * LANES:(h + 1) * LANES] for h in _B_ORDER], axis=0)
        s = _dot_nt(qs, kw) + tab_ref[...] + row_mask
        m = jnp.maximum(jnp.max(s, axis=-1, keepdims=True), sink)
        p = jnp.exp(s - m)
        denom = jnp.sum(p, axis=-1, keepdims=True) + jnp.exp(sink - m)
        pb = p.astype(BF16)
        o03 = _dot(pb[0:half], vw) / denom[0:half]
        o12 = _dot(pb[half:], vsw) / denom[half:]
        o_ref[rows, 0:LANES] = jnp.where(lane < B_HEAD_DIM, o03[0:BLOCK], o12[0:BLOCK]).astype(BF16)
        o_ref[rows, LANES:2 * LANES] = jnp.where(lane < B_HEAD_DIM, o12[BLOCK:], o03[BLOCK:]).astype(BF16)


def _attn_b(bq, bk_pad, bv_pad, bvs_pad, tab, sinks):
    nb, lp, _ = bq.shape
    kvspec = pl.BlockSpec((None, lp + 2 * BLOCK, LANES), lambda b, t: (b, 0, 0))
    return pl.pallas_call(
        functools.partial(_attn_b_kernel, lp=lp),
        grid=(nb, lp // TILE),
        in_specs=[pl.BlockSpec(memory_space=pltpu.SMEM),
                  pl.BlockSpec((None, TILE, B_HEADS * LANES), lambda b, t: (b, t, 0)),
                  kvspec, kvspec, kvspec, _resident(tab.shape)],
        out_specs=pl.BlockSpec((None, TILE, B_HEADS * B_HEAD_DIM), lambda b, t: (b, t, 0)),
        out_shape=jax.ShapeDtypeStruct((nb, lp, B_HEADS * B_HEAD_DIM), BF16),
        compiler_params=_params(2),
        name="attn_b",
    )(sinks, bq, bk_pad, bv_pad, bvs_pad, tab)


def _t5_bucket(rel):
    half = N_BUCKETS // 2
    max_exact = half // 2
    ret = jnp.where(rel > 0, half, 0)
    n = jnp.abs(rel)
    nf = jnp.maximum(n, 1).astype(F32)
    large = max_exact + (jnp.log(nf / max_exact) / math.log(MAX_DISTANCE / max_exact)
                         * (half - max_exact)).astype(jnp.int32)
    large = jnp.minimum(large, half - 1)
    return ret + jnp.where(n < max_exact, n, large)


def _rot_cols(w):
    half = w.shape[-1] // 2
    return jnp.concatenate([-w[..., half:], w[..., :half]], axis=-1)


def _bias_of_rel(table, rel):
    bucket = _t5_bucket(rel)[None]
    out = jnp.zeros((table.shape[1],) + rel.shape, F32)
    for b in range(N_BUCKETS):
        out = jnp.where(bucket == b, table[b].reshape((-1,) + (1,) * rel.ndim), out)
    return out


def _bias_tables(rel_bias, lp):
    rb = rel_bias.astype(F32)
    rela = (jnp.arange(3 * TILE) - TILE)[None, :] - jnp.arange(TILE)[:, None]
    wide = _bias_of_rel(rb[:, :A_HEADS], rela)
    left, diag, right = (wide[:, :, d * TILE:(d + 1) * TILE] for d in range(3))
    far_a = rb[:, :A_HEADS][_t5_bucket(jnp.array([-2 * TILE, 2 * TILE]))].T
    lead = jnp.where(jnp.arange(TILE) >= META_START, 0.0, NEG_INF).astype(F32).reshape(1, 1, TILE)
    far_left = jnp.broadcast_to(far_a[:, 0][:, None, None], left.shape)
    far_right = jnp.broadcast_to(far_a[:, 1][:, None, None], left.shape)
    tab_a = jnp.stack([left, diag, right, far_left, far_right, left + lead, diag + lead, far_left + lead], axis=1)
    kmask = jnp.concatenate([jnp.zeros_like(lead), lead], axis=0)
    relb = (jnp.arange(3 * BLOCK) - BLOCK)[None, :] - jnp.arange(BLOCK)[:, None]
    tab_b = jnp.where((jnp.abs(relb) <= WINDOW)[None], _bias_of_rel(rb[:, A_HEADS:], relb), NEG_INF)
    tab_b = jnp.concatenate([tab_b[h] for h in _B_ORDER], axis=0)
    slot = jnp.arange(lp)
    inv = ROPE_THETA ** (-jnp.arange(0, C_ROPE_DIM, 2, dtype=F32) / C_ROPE_DIM)
    ang = (slot - META_START).astype(F32)[:, None] * inv[None, :]
    ang = jnp.concatenate([ang, ang], axis=-1)
    ones = jnp.ones((lp, C_NOPE_DIM), F32)
    zeros_n = jnp.zeros((lp, C_NOPE_DIM), F32)
    zeros_p = jnp.zeros((lp, LANES - C_NOPE_DIM - C_ROPE_DIM), F32)
    cos_t = jnp.concatenate([ones, jnp.cos(ang), zeros_p], axis=-1)
    sin_t = jnp.concatenate([zeros_n, jnp.sin(ang), zeros_p], axis=-1)
    return tab_a, tab_b, kmask, cos_t, sin_t


def _layer_weights(w_in, w_uq, w_ukv, w_out):
    d = w_in.shape[0]
    sizes = (512, 512, 512, 256, 128, 128, 256, 128, 32)
    offs = [0]
    for s in sizes:
        offs.append(offs[-1] + s)
    aq, ak, av, bq, bk, bv, cq, ckv, kr = (w_in[:, offs[i]:offs[i + 1]] for i in range(9))
    zeros64 = jnp.zeros((d, B_HEAD_DIM), F32)
    bq_pad = []
    for h in range(B_HEADS):
        qh = bq[:, h * B_HEAD_DIM:(h + 1) * B_HEAD_DIM] * (B_HEAD_DIM ** -0.5)
        bq_pad += [qh, zeros64] if h // (B_HEADS // B_KV_HEADS) == 0 else [zeros64, qh]
    bvs = jnp.concatenate([bv[:, B_HEAD_DIM:], bv[:, :B_HEAD_DIM]], axis=1)
    pad_lo = jnp.zeros((d, C_NOPE_DIM), F32)
    pad_hi = jnp.zeros((d, LANES - C_NOPE_DIM - C_ROPE_DIM), F32)
    w_big = jnp.concatenate(
        [aq * (A_QK_DIM ** -0.5), ak, av] + bq_pad + [bk, bv, bvs, cq, ckv,
                                                      pad_lo, kr, pad_hi, pad_lo, _rot_cols(kr), pad_hi],
        axis=1).astype(BF16)
    uq = w_uq.reshape(C_Q_RANK, C_HEADS, C_NOPE_DIM + C_ROPE_DIM)
    uq_nope, uq_rope = uq[..., :C_NOPE_DIM], uq[..., C_NOPE_DIM:]
    zq = jnp.zeros((C_Q_RANK, C_HEADS, LANES - C_NOPE_DIM - C_ROPE_DIM), F32)
    wuqm = jnp.concatenate([uq_nope, uq_rope, zq], axis=-1).reshape(C_Q_RANK, C_HEADS * LANES).astype(BF16)
    wuqr = jnp.concatenate([jnp.zeros_like(uq_nope), _rot_cols(uq_rope), zq], axis=-1
                           ).reshape(C_Q_RANK, C_HEADS * LANES).astype(BF16)
    ukv = w_ukv.reshape(C_KV_RANK, C_HEADS, C_NOPE_DIM + C_V_DIM)
    wukvk = jnp.concatenate([ukv[..., :C_NOPE_DIM], jnp.zeros((C_KV_RANK, C_HEADS, LANES - C_NOPE_DIM), F32)],
                            axis=-1).reshape(C_KV_RANK, C_HEADS * LANES).astype(BF16)
    wukvv = ukv[..., C_NOPE_DIM:].reshape(C_KV_RANK, C_HEADS * C_V_DIM).astype(BF16)
    n_a = A_HEADS * A_V_DIM
    n_b = B_HEADS * B_HEAD_DIM
    wo = w_out.astype(BF16)
    wuq = jnp.concatenate([wuqm, wuqr], axis=1)
    wukv = jnp.concatenate([wukvk, wukvv], axis=1)
    return w_big, wuq, wukv, wo[:n_a], wo[n_a:n_a + n_b], wo[n_a + n_b:]


def _trunk(x, meta, rel_bias, g_ffn1, w_ffn1_gu, w_ffn1_down, g_mix, w_in, diff_lambda, g_subln, sinks, g_cq,
           g_ckv, w_uq, w_ukv, w_out, g_ffn2, w_ffn2_gu, w_ffn2_down, g_final, groups=None):
    nb, seq, d = x.shape
    lp = BLOCK + seq
    assert lp % TILE == 0 and d == D_MODEL
    groups = ((0, nb),) if groups is None else groups
    lead = jnp.concatenate([jnp.zeros((META_START, d), x.dtype), meta.astype(x.dtype)], axis=0)
    h = jnp.concatenate([jnp.broadcast_to(lead[None], (nb, BLOCK, d)), x], axis=1)
    tab_a, tab_b, kmask, cos_t, sin_t = _bias_tables(rel_bias, lp)
    pad = ((0, 0), (BLOCK, BLOCK), (0, 0))
    for l in range(DEPTH):
        w_big, wuq, wukv, wo_a, wo_b, wo_c = _layer_weights(w_in[l], w_uq[l], w_ukv[l], w_out[l])
        h = _ffn(h, g_ffn1[l], w_ffn1_gu[l].astype(BF16), w_ffn1_down[l].astype(BF16))
        aq, ak, av, bq, bk, bv, bvs, qc, kc, vc = _inproj(h, g_mix[l], w_big, g_cq[l], g_ckv[l], wuq, wukv,
                                                           cos_t, sin_t)
        lam_init = 0.8 - 0.6 * math.exp(-0.3 * l)
        oa = _attn_a(aq, ak, av, tab_a, diff_lambda[l].astype(F32), g_subln[l], lam_init)
        ob = _attn_b(bq, jnp.pad(bk, pad), jnp.pad(bv, pad), jnp.pad(bvs, pad), tab_b, sinks[l].astype(F32))
        oc = _attn_c(qc, kc, vc, kmask)
        ffn2 = functools.partial(_ffn, h, g_ffn2[l], w_ffn2_gu[l].astype(BF16), w_ffn2_down[l].astype(BF16),
                                 mix=(oa, ob, oc, wo_a, wo_b, wo_c))
        if l < DEPTH - 1:
            h = ffn2()
    return tuple(ffn2(g_final=g_final, out_batches=grp) for grp in groups)


def kernel(x_prompt, x_sample, meta, rel_bias, g_ffn1, w_ffn1_gu, w_ffn1_down, g_mix, w_in, diff_lambda, g_subln,
           sinks, g_cq, g_ckv, w_uq, w_ukv, w_out, g_ffn2, w_ffn2_gu, w_ffn2_down, g_final):
    n_prompt = x_prompt.shape[0]
    x = jnp.concatenate([x_prompt, x_sample], axis=0)
    return _trunk(x, meta, rel_bias, g_ffn1, w_ffn1_gu, w_ffn1_down, g_mix, w_in, diff_lambda, g_subln, sinks, g_cq,
                  g_ckv, w_uq, w_ukv, w_out, g_ffn2, w_ffn2_gu, w_ffn2_down, g_final,
                  groups=((0, n_prompt), (n_prompt, x_sample.shape[0])))
```

```python
import functools
import math

import jax
import jax.numpy as jnp
from jax import lax
from jax.experimental import pallas as pl
from jax.experimental.pallas import tpu as pltpu

D_MODEL = 1024
DEPTH = 2
N_META = 16
BLOCK = 128
META_START = BLOCK - N_META
A_HEADS = 4
A_QK_DIM = 64
A_V_DIM = 2 * A_QK_DIM
B_HEADS = 4
B_KV_HEADS = 2
B_HEAD_DIM = 64
WINDOW = 128
C_HEADS = 4
C_Q_RANK = 256
C_KV_RANK = 128
C_NOPE_DIM = 64
C_ROPE_DIM = 32
C_V_DIM = 64
ROPE_THETA = 10000.0
N_BUCKETS = 32
MAX_DISTANCE = 128
D_FF = 2816
EPS = 1e-6

LANES = 128
TILE = 5 * BLOCK
OUT_TILE = 4 * BLOCK
MXU_COLS = 256
FF_SPLITS = (0, 6 * MXU_COLS, D_FF)
ROW_BLOCK = TILE // 4
A_PIPE_UNROLL = 52
C_PIPE_UNROLL = 52
VMEM_LIMIT = 56 * 1024 * 1024

F32 = jnp.float32
BF16 = jnp.bfloat16
NEG_INF = float("-inf")

_W_AQ, _W_AK, _W_AV, _W_BQ = 0, 512, 1024, 1536
_W_BK, _W_BV, _W_BVS = 2048, 2176, 2304
_W_CQ, _W_CKV, _W_KR, _W_KRR = 2432, 2688, 2816, 2944
_W_TOTAL = 3072


def _params(n_axes):
    return pltpu.CompilerParams(dimension_semantics=("arbitrary",) * n_axes, vmem_limit_bytes=VMEM_LIMIT)


def _resident(shape):
    return pl.BlockSpec(shape, lambda *_: (0,) * len(shape), pipeline_mode=pl.Buffered(1))


def _rms(x, g):
    return x * lax.rsqrt(jnp.mean(x * x, axis=-1, keepdims=True) + EPS) * g


def _dot(a, b):
    return jnp.dot(a, b, preferred_element_type=F32)


def _dot_nt(a, b):
    return lax.dot_general(a, b, (((1,), (1,)), ((), ())), preferred_element_type=F32)


def _ffn_kernel(x_ref, *refs, mixed, final_norm):
    refs = list(refs)
    x = x_ref[...]
    if mixed:
        oa_ref, ob_ref, oc_ref, wa_ref, wb_ref, wc_ref = refs[:6]
        del refs[:6]
        x = x + _dot(oa_ref[...], wa_ref[...]) + _dot(ob_ref[...], wb_ref[...]) + _dot(oc_ref[...], wc_ref[...])
    g_ref, wgu_ref, wd_ref = refs[:3]
    gf_ref = refs[3] if final_norm else None
    o_ref = refs[-1]
    xn = _rms(x, g_ref[...]).astype(BF16)
    acc = None
    for lo, hi in zip(FF_SPLITS[:-1], FF_SPLITS[1:]):
        g = _dot(xn, wgu_ref[:, lo:hi])
        u = _dot(xn, wgu_ref[:, D_FF + lo:D_FF + hi])
        a = (g * jax.nn.sigmoid(g) * u).astype(BF16)
        d = _dot(a, wd_ref[lo:hi, :])
        acc = d if acc is None else acc + d
    y = x + 0.5 * acc
    if final_norm:
        y = _rms(y, gf_ref[...])
    o_ref[...] = y


def _ffn(h, g, wgu, wd, mix=None, g_final=None, out_batches=None):
    nb, lp, d = h.shape
    if out_batches is None:
        first, count, n_rows, tile_rows = 0, nb, lp, TILE

        def rows(n):
            return pl.BlockSpec((None, TILE, n), lambda b, t: (b, t, 0))

        out_spec = rows(d)
    else:
        first, count = out_batches
        n_rows, tile_rows = lp - BLOCK, OUT_TILE

        def rows(n):
            return pl.BlockSpec((pl.Element(OUT_TILE), pl.Element(n)),
                                lambda b, t: (pl.multiple_of((b + first) * lp + BLOCK + t * OUT_TILE, BLOCK), 0))

        out_spec = pl.BlockSpec((None, OUT_TILE, d), lambda b, t: (b, t, 0))
        h = h.reshape(nb * lp, d)
        if mix is not None:
            mix = tuple(o.reshape(nb * lp, o.shape[-1]) for o in mix[:3]) + tuple(mix[3:])
    assert n_rows % tile_rows == 0

    tile = rows(d)
    in_specs = [tile]
    args = [h]
    if mix is not None:
        in_specs += [rows(o.shape[-1]) for o in mix[:3]] + [_resident(w.shape) for w in mix[3:]]
        args += list(mix)
    in_specs += [_resident((1, d)), _resident(wgu.shape), _resident(wd.shape)]
    args += [g.reshape(1, d), wgu, wd]
    if g_final is not None:
        in_specs.append(_resident((1, d)))
        args.append(g_final.reshape(1, d))
    return pl.pallas_call(
        functools.partial(_ffn_kernel, mixed=mix is not None, final_norm=g_final is not None),
        grid=(count, n_rows // tile_rows),
        in_specs=in_specs,
        out_specs=out_spec,
        out_shape=jax.ShapeDtypeStruct((count, n_rows, d), F32),
        compiler_params=_params(2),
        name="ffn",
    )(*args)


def _inproj_kernel(h_ref, g_ref, w_ref, gcq_ref, gckv_ref, wuq_ref, wukv_ref, cos_ref, sin_ref,
                   aq_ref, ak_ref, av_ref, bq_ref, bk_ref, bv_ref, bvs_ref, qc_ref, kc_ref, vc_ref, *, c_scale):
    xn = _rms(h_ref[...], g_ref[...]).astype(BF16)

    wide = 2 * _W_AK
    y = _dot(xn, w_ref[:, 0:wide])
    aq_ref[...] = y[:, _W_AQ:_W_AK].astype(BF16)
    ak_ref[...] = y[:, _W_AK:wide].astype(BF16)
    y = _dot(xn, w_ref[:, wide:2 * wide])
    av_ref[...] = y[:, 0:_W_BQ - _W_AV].astype(BF16)
    bq_ref[...] = y[:, _W_BQ - _W_AV:wide].astype(BF16)
    y = _dot(xn, w_ref[:, _W_BK:_W_TOTAL])

    def piece(lo, n):
        return y[:, lo - _W_BK:lo - _W_BK + n]

    bk_ref[...] = piece(_W_BK, LANES).astype(BF16)
    bv_ref[...] = piece(_W_BV, LANES).astype(BF16)
    bvs_ref[...] = piece(_W_BVS, LANES).astype(BF16)

    cos = cos_ref[...]
    sin = sin_ref[...]
    cos4 = jnp.concatenate([cos] * C_HEADS, axis=1)
    sin4 = jnp.concatenate([sin] * C_HEADS, axis=1)
    cqn = _rms(piece(_W_CQ, C_Q_RANK), gcq_ref[...]).astype(BF16)
    q = _dot(cqn, wuq_ref[...])
    n_q = C_HEADS * LANES
    qc_ref[...] = ((q[:, 0:n_q] * cos4 + q[:, n_q:2 * n_q] * sin4) * c_scale).astype(BF16)

    ckvn = _rms(piece(_W_CKV, C_KV_RANK), gckv_ref[...]).astype(BF16)
    k_rope = piece(_W_KR, LANES) * cos + piece(_W_KRR, LANES) * sin
    kv = _dot(ckvn, wukv_ref[...])
    kc_ref[...] = (kv[:, 0:n_q] + jnp.concatenate([k_rope] * C_HEADS, axis=1)).astype(BF16)
    vc_ref[...] = kv[:, n_q:n_q + C_HEADS * C_V_DIM].astype(BF16)


def _inproj(h, g, w_big, gcq, gckv, wuq, wukv, cos_t, sin_t):
    nb, lp, d = h.shape

    def tile(n):
        return pl.BlockSpec((None, TILE, n), lambda b, t: (b, t, 0))

    pos = pl.BlockSpec((TILE, LANES), lambda b, t: (t, 0))
    widths = (512, 512, 512, 512, 128, 128, 128, 512, 512, 256)
    return pl.pallas_call(
        functools.partial(_inproj_kernel, c_scale=(C_NOPE_DIM + C_ROPE_DIM) ** -0.5),
        grid=(nb, lp // TILE),
        in_specs=[tile(d), _resident((1, d)), _resident(w_big.shape), _resident((1, C_Q_RANK)),
                  _resident((1, C_KV_RANK)), _resident(wuq.shape), _resident(wukv.shape), pos, pos],
        out_specs=[tile(n) for n in widths],
        out_shape=[jax.ShapeDtypeStruct((nb, lp, n), BF16) for n in widths],
        compiler_params=_params(2),
        name="inproj",
    )(h, g.reshape(1, d), w_big, gcq.reshape(1, -1), gckv.reshape(1, -1), wuq, wukv, cos_t, sin_t)


def _static(x):
    return isinstance(x, (int, bool))


def _aligned(x, m):
    return x if _static(x) else pl.multiple_of(x, m)


def _imin(a, b):
    return min(a, b) if _static(a) else jnp.minimum(a, b)


def _imax(a, b):
    return max(a, b) if _static(a) else jnp.maximum(a, b)


def _iselect(c, a, b):
    return (a if c else b) if _static(c) else jnp.where(c, a, b)


def _flash_scratch():
    return ([pltpu.VMEM((ROW_BLOCK, TILE), BF16)] * 2 + [pltpu.VMEM((ROW_BLOCK, LANES), F32)] * 2
            + [pltpu.VMEM((ROW_BLOCK, TILE), BF16)] * 2 + [pltpu.VMEM((ROW_BLOCK, LANES), F32)] * 2
            + [pltpu.VMEM((2 * TILE, LANES), F32), pltpu.VMEM((2 * TILE, 2 * LANES), F32)])


def _flash_pipeline(n_tiles, scores, values, stat_row, scratch, max_unroll):
    s0, s1, x0, x1, p0, p1, a0, a1, m_ref, acc_ref = scratch
    s_bufs, x_bufs, p_bufs, a_bufs = (s0, s1), (x0, x1), (p0, p1), (a0, a1)
    unroll = max(k for k in range(2, max_unroll + 1, 2) if n_tiles % k == 0)
    m_ref[...] = jnp.full(m_ref.shape, NEG_INF, F32)
    acc_ref[...] = jnp.zeros(acc_ref.shape, F32)
    p1[...] = jnp.zeros(p1.shape, BF16)
    a1[...] = jnp.ones(a1.shape, F32)

    def issue(t, par):
        s = scores(t, par).astype(BF16)
        s_bufs[par][...] = s
        x_bufs[par][...] = jnp.broadcast_to(jnp.max(s, axis=-1, keepdims=True).astype(F32), (ROW_BLOCK, LANES))

    def accumulate(t, par):
        rows = pl.ds(stat_row(t, par), ROW_BLOCK)
        alpha = a_bufs[par][...]
        acc_ref[rows, :] = (jnp.concatenate([alpha, alpha], axis=1) * acc_ref[rows, :]
                            + _dot(p_bufs[par][...], values(t, par)))

    def step(t, par):
        issue(_imin(t + 1, n_tiles - 1), 1 - par)
        rows = pl.ds(stat_row(t, par), ROW_BLOCK)
        m_old = m_ref[rows, :]
        m_new = jnp.maximum(m_old, x_bufs[par][...])
        p_bufs[par][...] = jnp.exp(s_bufs[par][...]
                                   - jnp.concatenate([m_new.astype(BF16)] * (TILE // LANES), axis=1))
        a_bufs[par][...] = jnp.exp(m_old - m_new)
        m_ref[rows, :] = m_new
        accumulate(_imax(t - 1, 0), 1 - par)

    issue(0, 0)

    def body(u, carry):
        for k in range(unroll):
            step(unroll * u + k, k % 2)
        return carry

    if unroll == n_tiles:
        for k in range(n_tiles):
            step(k, k % 2)
    else:
        lax.fori_loop(0, n_tiles // unroll, body, 0)
    accumulate(n_tiles - 1, 1)


(_TAB_LEFT, _TAB_DIAG, _TAB_RIGHT, _TAB_FAR_LEFT, _TAB_FAR_RIGHT,
 _TAB_LEFT_LEAD, _TAB_DIAG_LEAD, _TAB_FAR_LEAD) = range(8)


def _ones_right(v):
    return jnp.concatenate([v, jnp.ones(v.shape, v.dtype)], axis=1)


def _attn_a_kernel(q_ref, k_ref, v_ref, tab_ref, dl_ref, gs_ref, o_ref, qq_ref, *scratch, lam_init, n_chunks):
    i = pl.program_id(2)
    q = q_ref[...]
    lane = lax.broadcasted_iota(jnp.int32, q.shape, 1)
    zero = jnp.zeros_like(q)
    qq_ref[0:TILE, :] = jnp.where(lane < A_QK_DIM, q, zero)
    qq_ref[TILE:2 * TILE, :] = jnp.where(lane >= A_QK_DIM, q, zero)
    blocks = 2 * TILE // ROW_BLOCK

    def q_row(t):
        return _aligned((t % blocks) * ROW_BLOCK, ROW_BLOCK)

    def k_row(t):
        return _aligned((t // blocks) * TILE, TILE)

    def scores(t, par):
        j = t // blocks
        lead = jnp.where(i == 0, _TAB_DIAG_LEAD, jnp.where(i == 1, _TAB_LEFT_LEAD, _TAB_FAR_LEAD))
        rest = jnp.where(j < i - 1, _TAB_FAR_LEFT, jnp.where(j > i + 1, _TAB_FAR_RIGHT, j - i + 1))
        table = jnp.where(j == 0, lead, rest)
        row = _aligned((t % (TILE // ROW_BLOCK)) * ROW_BLOCK, ROW_BLOCK)
        return (_dot_nt(qq_ref[pl.ds(q_row(t), ROW_BLOCK), :], k_ref[pl.ds(k_row(t), TILE), :])
                + tab_ref[table, pl.ds(row, ROW_BLOCK), :])

    def values(t, par):
        return _ones_right(v_ref[pl.ds(k_row(t), TILE), :])

    _flash_pipeline(n_chunks * blocks, scores, values, lambda t, par: q_row(t), scratch, A_PIPE_UNROLL)

    acc_ref = scratch[-1]
    o = acc_ref[:, 0:A_V_DIM] / acc_ref[:, A_V_DIM:2 * A_V_DIM]
    dl = dl_ref[...]
    lam = (jnp.exp(jnp.sum(dl[0:1] * dl[1:2], axis=-1, keepdims=True))
           - jnp.exp(jnp.sum(dl[2:3] * dl[3:4], axis=-1, keepdims=True)) + lam_init)
    w = o[0:TILE] - lam * o[TILE:2 * TILE]
    o_ref[...] = (_rms(w, gs_ref[...]) * (1.0 - lam_init)).astype(BF16)


def _attn_a(aq, ak, av, tab, dl, gs, lam_init):
    nb, lp, _ = aq.shape
    n_chunks = lp // TILE
    qspec = pl.BlockSpec((None, TILE, LANES), lambda b, h, t: (b, t, h))
    kvspec = pl.BlockSpec((None, lp, LANES), lambda b, h, t: (b, 0, h))
    return pl.pallas_call(
        functools.partial(_attn_a_kernel, lam_init=lam_init, n_chunks=n_chunks),
        grid=(nb, A_HEADS, n_chunks),
        in_specs=[qspec, kvspec, kvspec,
                  pl.BlockSpec((None,) + tab.shape[1:], lambda b, h, t: (h, 0, 0, 0)),
                  _resident((4, A_QK_DIM)), _resident((1, A_V_DIM))],
        out_specs=qspec,
        out_shape=jax.ShapeDtypeStruct((nb, lp, A_HEADS * A_V_DIM), BF16),
        scratch_shapes=[pltpu.VMEM((2 * TILE, LANES), BF16)] + _flash_scratch(),
        compiler_params=_params(3),
        name="attn_a",
    )(aq, ak, av, tab, dl, gs.reshape(1, A_V_DIM))


def _attn_c_kernel(q_ref, k_ref, v_ref, kmask_ref, o_ref, *scratch, n_chunks):
    blocks = 2 * TILE // ROW_BLOCK

    def q_row(t):
        return _aligned(((t % blocks) // 2) * ROW_BLOCK, ROW_BLOCK)

    def k_row(t):
        return _aligned((t // blocks) * TILE, TILE)

    def scores(t, par):
        cols = slice(par * LANES, (par + 1) * LANES)
        return (_dot_nt(q_ref[pl.ds(q_row(t), ROW_BLOCK), cols], k_ref[pl.ds(k_row(t), TILE), cols])
                + kmask_ref[_iselect(t < blocks, 1, 0)])

    def values(t, par):
        return _ones_right(v_ref[pl.ds(k_row(t), TILE), :])

    def stat_row(t, par):
        return _aligned(par * TILE + q_row(t), ROW_BLOCK)

    _flash_pipeline(n_chunks * blocks, scores, values, stat_row, scratch, C_PIPE_UNROLL)

    acc_ref = scratch[-1]
    o = acc_ref[:, 0:LANES] / acc_ref[:, LANES:2 * LANES]
    lane = lax.broadcasted_iota(jnp.int32, (TILE, LANES), 1)
    o_ref[...] = jnp.where(lane < C_V_DIM, o[0:TILE], o[TILE:2 * TILE]).astype(BF16)


def _attn_c(qc, kc, vc, kmask):
    nb, lp, _ = qc.shape
    n_chunks = lp // TILE
    return pl.pallas_call(
        functools.partial(_attn_c_kernel, n_chunks=n_chunks),
        grid=(nb, C_HEADS // 2, n_chunks),
        in_specs=[pl.BlockSpec((None, TILE, 2 * LANES), lambda b, p, t: (b, t, p)),
                  pl.BlockSpec((None, lp, 2 * LANES), lambda b, p, t: (b, 0, p)),
                  pl.BlockSpec((None, lp, LANES), lambda b, p, t: (b, 0, p)),
                  _resident(kmask.shape)],
        out_specs=pl.BlockSpec((None, TILE, LANES), lambda b, p, t: (b, t, p)),
        out_shape=jax.ShapeDtypeStruct((nb, lp, C_HEADS * C_V_DIM), BF16),
        scratch_shapes=_flash_scratch(),
        compiler_params=_params(3),
        name="attn_c",
    )(qc, kc, vc, kmask)


_B_ORDER = (0, 3, 1, 2)


def _attn_b_kernel(sink_ref, q_ref, k_ref, v_ref, vs_ref, tab_ref, o_ref, *, lp):
    t = pl.program_id(1)
    lane = lax.broadcasted_iota(jnp.int32, (BLOCK, LANES), 1)
    kk = lax.broadcasted_iota(jnp.int32, (1, 3 * BLOCK), 1)
    row = lax.broadcasted_iota(jnp.int32, (B_HEADS * BLOCK, 1), 0)
    sink = jnp.full((B_HEADS * BLOCK, 1), sink_ref[_B_ORDER[0]], F32)
    for n in range(1, B_HEADS):
        sink = jnp.where(row >= n * BLOCK, sink_ref[_B_ORDER[n]], sink)
    half = B_HEADS * BLOCK // 2
    for blk in range(TILE // BLOCK):
        gblk = t * (TILE // BLOCK) + blk
        row0 = pl.multiple_of(gblk * BLOCK, BLOCK)
        kw = k_ref[pl.ds(row0, 3 * BLOCK), :]
        vw = v_ref[pl.ds(row0, 3 * BLOCK), :]
        vsw = vs_ref[pl.ds(row0, 3 * BLOCK), :]
        kslot = (gblk - 1) * BLOCK + kk
        row_mask = jnp.where((kslot >= META_START) & (kslot < lp), 0.0, NEG_INF)
        rows = slice(blk * BLOCK, (blk + 1) * BLOCK)
        qs = jnp.concatenate([q_ref[rows, h * LANES:(h + 1) * LANES] for h in _B_ORDER], axis=0)
        s = _dot_nt(qs, kw) + tab_ref[...] + row_mask
        m = jnp.maximum(jnp.max(s, axis=-1, keepdims=True), sink)
        p = jnp.exp(s - m)
        denom = jnp.sum(p, axis=-1, keepdims=True) + jnp.exp(sink - m)
        pb = p.astype(BF16)
        o03 = _dot(pb[0:half], vw) / denom[0:half]
        o12 = _dot(pb[half:], vsw) / denom[half:]
        o_ref[rows, 0:LANES] = jnp.where(lane < B_HEAD_DIM, o03[0:BLOCK], o12[0:BLOCK]).astype(BF16)
        o_ref[rows, LANES:2 * LANES] = jnp.where(lane < B_HEAD_DIM, o12[BLOCK:], o03[BLOCK:]).astype(BF16)


def _attn_b(bq, bk_pad, bv_pad, bvs_pad, tab, sinks):
    nb, lp, _ = bq.shape
    kvspec = pl.BlockSpec((None, lp + 2 * BLOCK, LANES), lambda b, t: (b, 0, 0))
    return pl.pallas_call(
        functools.partial(_attn_b_kernel, lp=lp),
        grid=(nb, lp // TILE),
        in_specs=[pl.BlockSpec(memory_space=pltpu.SMEM),
                  pl.BlockSpec((None, TILE, B_HEADS * LANES), lambda b, t: (b, t, 0)),
                  kvspec, kvspec, kvspec, _resident(tab.shape)],
        out_specs=pl.BlockSpec((None, TILE, B_HEADS * B_HEAD_DIM), lambda b, t: (b, t, 0)),
        out_shape=jax.ShapeDtypeStruct((nb, lp, B_HEADS * B_HEAD_DIM), BF16),
        compiler_params=_params(2),
        name="attn_b",
    )(sinks, bq, bk_pad, bv_pad, bvs_pad, tab)


def _t5_bucket(rel):
    half = N_BUCKETS // 2
    max_exact = half // 2
    ret = jnp.where(rel > 0, half, 0)
    n = jnp.abs(rel)
    nf = jnp.maximum(n, 1).astype(F32)
    large = max_exact + (jnp.log(nf / max_exact) / math.log(MAX_DISTANCE / max_exact)
                         * (half - max_exact)).astype(jnp.int32)
    large = jnp.minimum(large, half - 1)
    return ret + jnp.where(n < max_exact, n, large)


def _rot_cols(w):
    half = w.shape[-1] // 2
    return jnp.concatenate([-w[..., half:], w[..., :half]], axis=-1)


def _bias_of_rel(table, rel):
    bucket = _t5_bucket(rel)[None]
    out = jnp.zeros((table.shape[1],) + rel.shape, F32)
    for b in range(N_BUCKETS):
        out = jnp.where(bucket == b, table[b].reshape((-1,) + (1,) * rel.ndim), out)
    return out


def _bias_tables(rel_bias, lp):
    rb = rel_bias.astype(F32)
    rela = (jnp.arange(3 * TILE) - TILE)[None, :] - jnp.arange(TILE)[:, None]
    wide = _bias_of_rel(rb[:, :A_HEADS], rela)
    left, diag, right = (wide[:, :, d * TILE:(d + 1) * TILE] for d in range(3))
    far_a = rb[:, :A_HEADS][_t5_bucket(jnp.array([-2 * TILE, 2 * TILE]))].T
    lead = jnp.where(jnp.arange(TILE) >= META_START, 0.0, NEG_INF).astype(F32).reshape(1, 1, TILE)
    far_left = jnp.broadcast_to(far_a[:, 0][:, None, None], left.shape)
    far_right = jnp.broadcast_to(far_a[:, 1][:, None, None], left.shape)
    tab_a = jnp.stack([left, diag, right, far_left, far_right, left + lead, diag + lead, far_left + lead], axis=1)
    kmask = jnp.concatenate([jnp.zeros_like(lead), lead], axis=0)
    relb = (jnp.arange(3 * BLOCK) - BLOCK)[None, :] - jnp.arange(BLOCK)[:, None]
    tab_b = jnp.where((jnp.abs(relb) <= WINDOW)[None], _bias_of_rel(rb[:, A_HEADS:], relb), NEG_INF)
    tab_b = jnp.concatenate([tab_b[h] for h in _B_ORDER], axis=0)
    slot = jnp.arange(lp)
    inv = ROPE_THETA ** (-jnp.arange(0, C_ROPE_DIM, 2, dtype=F32) / C_ROPE_DIM)
    ang = (slot - META_START).astype(F32)[:, None] * inv[None, :]
    ang = jnp.concatenate([ang, ang], axis=-1)
    ones = jnp.ones((lp, C_NOPE_DIM), F32)
    zeros_n = jnp.zeros((lp, C_NOPE_DIM), F32)
    zeros_p = jnp.zeros((lp, LANES - C_NOPE_DIM - C_ROPE_DIM), F32)
    cos_t = jnp.concatenate([ones, jnp.cos(ang), zeros_p], axis=-1)
    sin_t = jnp.concatenate([zeros_n, jnp.sin(ang), zeros_p], axis=-1)
    return tab_a, tab_b, kmask, cos_t, sin_t


def _layer_weights(w_in, w_uq, w_ukv, w_out):
    d = w_in.shape[0]
    sizes = (512, 512, 512, 256, 128, 128, 256, 128, 32)
    offs = [0]
    for s in sizes:
        offs.append(offs[-1] + s)
    aq, ak, av, bq, bk, bv, cq, ckv, kr = (w_in[:, offs[i]:offs[i + 1]] for i in range(9))
    zeros64 = jnp.zeros((d, B_HEAD_DIM), F32)
    bq_pad = []
    for h in range(B_HEADS):
        qh = bq[:, h * B_HEAD_DIM:(h + 1) * B_HEAD_DIM] * (B_HEAD_DIM ** -0.5)
        bq_pad += [qh, zeros64] if h // (B_HEADS // B_KV_HEADS) == 0 else [zeros64, qh]
    bvs = jnp.concatenate([bv[:, B_HEAD_DIM:], bv[:, :B_HEAD_DIM]], axis=1)
    pad_lo = jnp.zeros((d, C_NOPE_DIM), F32)
    pad_hi = jnp.zeros((d, LANES - C_NOPE_DIM - C_ROPE_DIM), F32)
    w_big = jnp.concatenate(
        [aq * (A_QK_DIM ** -0.5), ak, av] + bq_pad + [bk, bv, bvs, cq, ckv,
                                                      pad_lo, kr, pad_hi, pad_lo, _rot_cols(kr), pad_hi],
        axis=1).astype(BF16)
    uq = w_uq.reshape(C_Q_RANK, C_HEADS, C_NOPE_DIM + C_ROPE_DIM)
    uq_nope, uq_rope = uq[..., :C_NOPE_DIM], uq[..., C_NOPE_DIM:]
    zq = jnp.zeros((C_Q_RANK, C_HEADS, LANES - C_NOPE_DIM - C_ROPE_DIM), F32)
    wuqm = jnp.concatenate([uq_nope, uq_rope, zq], axis=-1).reshape(C_Q_RANK, C_HEADS * LANES).astype(BF16)
    wuqr = jnp.concatenate([jnp.zeros_like(uq_nope), _rot_cols(uq_rope), zq], axis=-1
                           ).reshape(C_Q_RANK, C_HEADS * LANES).astype(BF16)
    ukv = w_ukv.reshape(C_KV_RANK, C_HEADS, C_NOPE_DIM + C_V_DIM)
    wukvk = jnp.concatenate([ukv[..., :C_NOPE_DIM], jnp.zeros((C_KV_RANK, C_HEADS, LANES - C_NOPE_DIM), F32)],
                            axis=-1).reshape(C_KV_RANK, C_HEADS * LANES).astype(BF16)
    wukvv = ukv[..., C_NOPE_DIM:].reshape(C_KV_RANK, C_HEADS * C_V_DIM).astype(BF16)
    n_a = A_HEADS * A_V_DIM
    n_b = B_HEADS * B_HEAD_DIM
    wo = w_out.astype(BF16)
    wuq = jnp.concatenate([wuqm, wuqr], axis=1)
    wukv = jnp.concatenate([wukvk, wukvv], axis=1)
    return w_big, wuq, wukv, wo[:n_a], wo[n_a:n_a + n_b], wo[n_a + n_b:]


def _trunk(x, meta, rel_bias, g_ffn1, w_ffn1_gu, w_ffn1_down, g_mix, w_in, diff_lambda, g_subln, sinks, g_cq,
           g_ckv, w_uq, w_ukv, w_out, g_ffn2, w_ffn2_gu, w_ffn2_down, g_final, groups=None):
    nb, seq, d = x.shape
    lp = BLOCK + seq
    assert lp % TILE == 0 and d == D_MODEL
    groups = ((0, nb),) if groups is None else groups
    lead = jnp.concatenate([jnp.zeros((META_START, d), x.dtype), meta.astype(x.dtype)], axis=0)
    h = jnp.concatenate([jnp.broadcast_to(lead[None], (nb, BLOCK, d)), x], axis=1)
    tab_a, tab_b, kmask, cos_t, sin_t = _bias_tables(rel_bias, lp)
    pad = ((0, 0), (BLOCK, BLOCK), (0, 0))
    for l in range(DEPTH):
        w_big, wuq, wukv, wo_a, wo_b, wo_c = _layer_weights(w_in[l], w_uq[l], w_ukv[l], w_out[l])
        h = _ffn(h, g_ffn1[l], w_ffn1_gu[l].astype(BF16), w_ffn1_down[l].astype(BF16))
        aq, ak, av, bq, bk, bv, bvs, qc, kc, vc = _inproj(h, g_mix[l], w_big, g_cq[l], g_ckv[l], wuq, wukv,
                                                           cos_t, sin_t)
        lam_init = 0.8 - 0.6 * math.exp(-0.3 * l)
        oa = _attn_a(aq, ak, av, tab_a, diff_lambda[l].astype(F32), g_subln[l], lam_init)
        ob = _attn_b(bq, jnp.pad(bk, pad), jnp.pad(bv, pad), jnp.pad(bvs, pad), tab_b, sinks[l].astype(F32))
        oc = _attn_c(qc, kc, vc, kmask)
        ffn2 = functools.partial(_ffn, h, g_ffn2[l], w_ffn2_gu[l].astype(BF16), w_ffn2_down[l].astype(BF16),
                                 mix=(oa, ob, oc, wo_a, wo_b, wo_c))
        if l < DEPTH - 1:
            h = ffn2()
    return tuple(ffn2(g_final=g_final, out_batches=grp) for grp in groups)


def kernel(x_prompt, x_sample, meta, rel_bias, g_ffn1, w_ffn1_gu, w_ffn1_down, g_mix, w_in, diff_lambda, g_subln,
           sinks, g_cq, g_ckv, w_uq, w_ukv, w_out, g_ffn2, w_ffn2_gu, w_ffn2_down, g_final):
    n_prompt = x_prompt.shape[0]
    x = jnp.concatenate([x_prompt, x_sample], axis=0)
    return _trunk(x, meta, rel_bias, g_ffn1, w_ffn1_gu, w_ffn1_down, g_mix, w_in, diff_lambda, g_subln, sinks, g_cq,
                  g_ckv, w_uq, w_ukv, w_out, g_ffn2, w_ffn2_gu, w_ffn2_down, g_final,
                  groups=((0, n_prompt), (n_prompt, x_sample.shape[0])))
```

```python
import functools
import math

import jax
import jax.numpy as jnp
from jax import lax
from jax.experimental import pallas as pl
from jax.experimental.pallas import tpu as pltpu

D_MODEL = 1024
DEPTH = 2
N_META = 16
BLOCK = 128
META_START = BLOCK - N_META
A_HEADS = 4
A_QK_DIM = 64
A_V_DIM = 2 * A_QK_DIM
B_HEADS = 4
B_KV_HEADS = 2
B_HEAD_DIM = 64
WINDOW = 128
C_HEADS = 4
C_Q_RANK = 256
C_KV_RANK = 128
C_NOPE_DIM = 64
C_ROPE_DIM = 32
C_V_DIM = 64
ROPE_THETA = 10000.0
N_BUCKETS = 32
MAX_DISTANCE = 128
D_FF = 2816
EPS = 1e-6

LANES = 128
TILE = 5 * BLOCK
OUT_TILE = 4 * BLOCK
MXU_COLS = 256
FF_SPLITS = (0, 6 * MXU_COLS, D_FF)
ROW_BLOCK = TILE // 4
A_PIPE_UNROLL = 52
C_PIPE_UNROLL = 52
VMEM_LIMIT = 56 * 1024 * 1024

F32 = jnp.float32
BF16 = jnp.bfloat16
NEG_INF = float("-inf")

_W_AQ, _W_AK, _W_AV, _W_BQ = 0, 512, 1024, 1536
_W_BK, _W_BV, _W_BVS = 2048, 2176, 2304
_W_CQ, _W_CKV, _W_KR, _W_KRR = 2432, 2688, 2816, 2944
_W_TOTAL = 3072


def _params(n_axes):
    return pltpu.CompilerParams(dimension_semantics=("arbitrary",) * n_axes, vmem_limit_bytes=VMEM_LIMIT)


def _resident(shape):
    return pl.BlockSpec(shape, lambda *_: (0,) * len(shape), pipeline_mode=pl.Buffered(1))


def _rms(x, g):
    return x * lax.rsqrt(jnp.mean(x * x, axis=-1, keepdims=True) + EPS) * g


def _dot(a, b):
    return jnp.dot(a, b, preferred_element_type=F32)


def _dot_nt(a, b):
    return lax.dot_general(a, b, (((1,), (1,)), ((), ())), preferred_element_type=F32)


def _ffn_kernel(x_ref, *refs, mixed, final_norm):
    refs = list(refs)
    x = x_ref[...]
    if mixed:
        oa_ref, ob_ref, oc_ref, wa_ref, wb_ref, wc_ref = refs[:6]
        del refs[:6]
        x = x + _dot(oa_ref[...], wa_ref[...]) + _dot(ob_ref[...], wb_ref[...]) + _dot(oc_ref[...], wc_ref[...])
    g_ref, wgu_ref, wd_ref = refs[:3]
    gf_ref = refs[3] if final_norm else None
    o_ref = refs[-1]
    xn = _rms(x, g_ref[...]).astype(BF16)
    acc = None
    for lo, hi in zip(FF_SPLITS[:-1], FF_SPLITS[1:]):
        g = _dot(xn, wgu_ref[:, lo:hi])
        u = _dot(xn, wgu_ref[:, D_FF + lo:D_FF + hi])
        a = (g * jax.nn.sigmoid(g) * u).astype(BF16)
        d = _dot(a, wd_ref[lo:hi, :])
        acc = d if acc is None else acc + d
    y = x + 0.5 * acc
    if final_norm:
        y = _rms(y, gf_ref[...])
    o_ref[...] = y


def _ffn(h, g, wgu, wd, mix=None, g_final=None, out_batches=None):
    nb, lp, d = h.shape
    if out_batches is None:
        first, count, n_rows, tile_rows = 0, nb, lp, TILE

        def rows(n):
            return pl.BlockSpec((None, TILE, n), lambda b, t: (b, t, 0))

        out_spec = rows(d)
    else:
        first, count = out_batches
        n_rows, tile_rows = lp - BLOCK, OUT_TILE

        def rows(n):
            return pl.BlockSpec((pl.Element(OUT_TILE), pl.Element(n)),
                                lambda b, t: (pl.multiple_of((b + first) * lp + BLOCK + t * OUT_TILE, BLOCK), 0))

        out_spec = pl.BlockSpec((None, OUT_TILE, d), lambda b, t: (b, t, 0))
        h = h.reshape(nb * lp, d)
        if mix is not None:
            mix = tuple(o.reshape(nb * lp, o.shape[-1]) for o in mix[:3]) + tuple(mix[3:])
    assert n_rows % tile_rows == 0

    tile = rows(d)
    in_specs = [tile]
    args = [h]
    if mix is not None:
        in_specs += [rows(o.shape[-1]) for o in mix[:3]] + [_resident(w.shape) for w in mix[3:]]
        args += list(mix)
    in_specs += [_resident((1, d)), _resident(wgu.shape), _resident(wd.shape)]
    args += [g.reshape(1, d), wgu, wd]
    if g_final is not None:
        in_specs.append(_resident((1, d)))
        args.append(g_final.reshape(1, d))
    return pl.pallas_call(
        functools.partial(_ffn_kernel, mixed=mix is not None, final_norm=g_final is not None),
        grid=(count, n_rows // tile_rows),
        in_specs=in_specs,
        out_specs=out_spec,
        out_shape=jax.ShapeDtypeStruct((count, n_rows, d), F32),
        compiler_params=_params(2),
        name="ffn",
    )(*args)


def _inproj_kernel(h_ref, g_ref, w_ref, gcq_ref, gckv_ref, wuq_ref, wukv_ref, cos_ref, sin_ref,
                   aq_ref, ak_ref, av_ref, bq_ref, bk_ref, bv_ref, bvs_ref, qc_ref, kc_ref, vc_ref, *, c_scale):
    xn = _rms(h_ref[...], g_ref[...]).astype(BF16)

    wide = 2 * _W_AK
    y = _dot(xn, w_ref[:, 0:wide])
    aq_ref[...] = y[:, _W_AQ:_W_AK].astype(BF16)
    ak_ref[...] = y[:, _W_AK:wide].astype(BF16)
    y = _dot(xn, w_ref[:, wide:2 * wide])
    av_ref[...] = y[:, 0:_W_BQ - _W_AV].astype(BF16)
    bq_ref[...] = y[:, _W_BQ - _W_AV:wide].astype(BF16)
    y = _dot(xn, w_ref[:, _W_BK:_W_TOTAL])

    def piece(lo, n):
        return y[:, lo - _W_BK:lo - _W_BK + n]

    bk_ref[...] = piece(_W_BK, LANES).astype(BF16)
    bv_ref[...] = piece(_W_BV, LANES).astype(BF16)
    bvs_ref[...] = piece(_W_BVS, LANES).astype(BF16)

    cos = cos_ref[...]
    sin = sin_ref[...]
    cos4 = jnp.concatenate([cos] * C_HEADS, axis=1)
    sin4 = jnp.concatenate([sin] * C_HEADS, axis=1)
    cqn = _rms(piece(_W_CQ, C_Q_RANK), gcq_ref[...]).astype(BF16)
    q = _dot(cqn, wuq_ref[...])
    n_q = C_HEADS * LANES
    qc_ref[...] = ((q[:, 0:n_q] * cos4 + q[:, n_q:2 * n_q] * sin4) * c_scale).astype(BF16)

    ckvn = _rms(piece(_W_CKV, C_KV_RANK), gckv_ref[...]).astype(BF16)
    k_rope = piece(_W_KR, LANES) * cos + piece(_W_KRR, LANES) * sin
    kv = _dot(ckvn, wukv_ref[...])
    kc_ref[...] = (kv[:, 0:n_q] + jnp.concatenate([k_rope] * C_HEADS, axis=1)).astype(BF16)
    vc_ref[...] = kv[:, n_q:n_q + C_HEADS * C_V_DIM].astype(BF16)


def _inproj(h, g, w_big, gcq, gckv, wuq, wukv, cos_t, sin_t):
    nb, lp, d = h.shape

    def tile(n):
        return pl.BlockSpec((None, TILE, n), lambda b, t: (b, t, 0))

    pos = pl.BlockSpec((TILE, LANES), lambda b, t: (t, 0))
    widths = (512, 512, 512, 512, 128, 128, 128, 512, 512, 256)
    return pl.pallas_call(
        functools.partial(_inproj_kernel, c_scale=(C_NOPE_DIM + C_ROPE_DIM) ** -0.5),
        grid=(nb, lp // TILE),
        in_specs=[tile(d), _resident((1, d)), _resident(w_big.shape), _resident((1, C_Q_RANK)),
                  _resident((1, C_KV_RANK)), _resident(wuq.shape), _resident(wukv.shape), pos, pos],
        out_specs=[tile(n) for n in widths],
        out_shape=[jax.ShapeDtypeStruct((nb, lp, n), BF16) for n in widths],
        compiler_params=_params(2),
        name="inproj",
    )(h, g.reshape(1, d), w_big, gcq.reshape(1, -1), gckv.reshape(1, -1), wuq, wukv, cos_t, sin_t)


def _static(x):
    return isinstance(x, (int, bool))


def _aligned(x, m):
    return x if _static(x) else pl.multiple_of(x, m)


def _imin(a, b):
    return min(a, b) if _static(a) else jnp.minimum(a, b)


def _imax(a, b):
    return max(a, b) if _static(a) else jnp.maximum(a, b)


def _iselect(c, a, b):
    return (a if c else b) if _static(c) else jnp.where(c, a, b)


def _flash_scratch():
    return ([pltpu.VMEM((ROW_BLOCK, TILE), BF16)] * 2 + [pltpu.VMEM((ROW_BLOCK, LANES), F32)] * 2
            + [pltpu.VMEM((ROW_BLOCK, TILE), BF16)] * 2 + [pltpu.VMEM((ROW_BLOCK, LANES), F32)] * 2
            + [pltpu.VMEM((2 * TILE, LANES), F32), pltpu.VMEM((2 * TILE, 2 * LANES), F32)])


def _flash_pipeline(n_tiles, scores, values, stat_row, scratch, max_unroll):
    s0, s1, x0, x1, p0, p1, a0, a1, m_ref, acc_ref = scratch
    s_bufs, x_bufs, p_bufs, a_bufs = (s0, s1), (x0, x1), (p0, p1), (a0, a1)
    unroll = max(k for k in range(2, max_unroll + 1, 2) if n_tiles % k == 0)
    m_ref[...] = jnp.full(m_ref.shape, NEG_INF, F32)
    acc_ref[...] = jnp.zeros(acc_ref.shape, F32)
    p1[...] = jnp.zeros(p1.shape, BF16)
    a1[...] = jnp.ones(a1.shape, F32)

    def issue(t, par):
        s = scores(t, par).astype(BF16)
        s_bufs[par][...] = s
        x_bufs[par][...] = jnp.broadcast_to(jnp.max(s, axis=-1, keepdims=True).astype(F32), (ROW_BLOCK, LANES))

    def accumulate(t, par):
        rows = pl.ds(stat_row(t, par), ROW_BLOCK)
        alpha = a_bufs[par][...]
        acc_ref[rows, :] = (jnp.concatenate([alpha, alpha], axis=1) * acc_ref[rows, :]
                            + _dot(p_bufs[par][...], values(t, par)))

    def step(t, par):
        issue(_imin(t + 1, n_tiles - 1), 1 - par)
        rows = pl.ds(stat_row(t, par), ROW_BLOCK)
        m_old = m_ref[rows, :]
        m_new = jnp.maximum(m_old, x_bufs[par][...])
        p_bufs[par][...] = jnp.exp(s_bufs[par][...]
                                   - jnp.concatenate([m_new.astype(BF16)] * (TILE // LANES), axis=1))
        a_bufs[par][...] = jnp.exp(m_old - m_new)
        m_ref[rows, :] = m_new
        accumulate(_imax(t - 1, 0), 1 - par)

    issue(0, 0)

    def body(u, carry):
        for k in range(unroll):
            step(unroll * u + k, k % 2)
        return carry

    if unroll == n_tiles:
        for k in range(n_tiles):
            step(k, k % 2)
    else:
        lax.fori_loop(0, n_tiles // unroll, body, 0)
    accumulate(n_tiles - 1, 1)


(_TAB_LEFT, _TAB_DIAG, _TAB_RIGHT, _TAB_FAR_LEFT, _TAB_FAR_RIGHT,
 _TAB_LEFT_LEAD, _TAB_DIAG_LEAD, _TAB_FAR_LEAD) = range(8)


def _ones_right(v):
    return jnp.concatenate([v, jnp.ones(v.shape, v.dtype)], axis=1)


def _attn_a_kernel(q_ref, k_ref, v_ref, tab_ref, dl_ref, gs_ref, o_ref, qq_ref, *scratch, lam_init, n_chunks):
    i = pl.program_id(2)
    q = q_ref[...]
    lane = lax.broadcasted_iota(jnp.int32, q.shape, 1)
    zero = jnp.zeros_like(q)
    qq_ref[0:TILE, :] = jnp.where(lane < A_QK_DIM, q, zero)
    qq_ref[TILE:2 * TILE, :] = jnp.where(lane >= A_QK_DIM, q, zero)
    blocks = 2 * TILE // ROW_BLOCK

    def q_row(t):
        return _aligned((t % blocks) * ROW_BLOCK, ROW_BLOCK)

    def k_row(t):
        return _aligned((t // blocks) * TILE, TILE)

    def scores(t, par):
        j = t // blocks
        lead = jnp.where(i == 0, _TAB_DIAG_LEAD, jnp.where(i == 1, _TAB_LEFT_LEAD, _TAB_FAR_LEAD))
        rest = jnp.where(j < i - 1, _TAB_FAR_LEFT, jnp.where(j > i + 1, _TAB_FAR_RIGHT, j - i + 1))
        table = jnp.where(j == 0, lead, rest)
        row = _aligned((t % (TILE // ROW_BLOCK)) * ROW_BLOCK, ROW_BLOCK)
        return (_dot_nt(qq_ref[pl.ds(q_row(t), ROW_BLOCK), :], k_ref[pl.ds(k_row(t), TILE), :]).astype(BF16)
                + tab_ref[table, pl.ds(row, ROW_BLOCK), :])

    def values(t, par):
        return _ones_right(v_ref[pl.ds(k_row(t), TILE), :])

    _flash_pipeline(n_chunks * blocks, scores, values, lambda t, par: q_row(t), scratch, A_PIPE_UNROLL)

    acc_ref = scratch[-1]
    o = acc_ref[:, 0:A_V_DIM] / acc_ref[:, A_V_DIM:2 * A_V_DIM]
    dl = dl_ref[...]
    lam = (jnp.exp(jnp.sum(dl[0:1] * dl[1:2], axis=-1, keepdims=True))
           - jnp.exp(jnp.sum(dl[2:3] * dl[3:4], axis=-1, keepdims=True)) + lam_init)
    w = o[0:TILE] - lam * o[TILE:2 * TILE]
    o_ref[...] = (_rms(w, gs_ref[...]) * (1.0 - lam_init)).astype(BF16)


def _attn_a(aq, ak, av, tab, dl, gs, lam_init):
    nb, lp, _ = aq.shape
    n_chunks = lp // TILE
    qspec = pl.BlockSpec((None, TILE, LANES), lambda b, h, t: (b, t, h))
    kvspec = pl.BlockSpec((None, lp, LANES), lambda b, h, t: (b, 0, h))
    return pl.pallas_call(
        functools.partial(_attn_a_kernel, lam_init=lam_init, n_chunks=n_chunks),
        grid=(nb, A_HEADS, n_chunks),
        in_specs=[qspec, kvspec, kvspec,
                  pl.BlockSpec((None,) + tab.shape[1:], lambda b, h, t: (h, 0, 0, 0)),
                  _resident((4, A_QK_DIM)), _resident((1, A_V_DIM))],
        out_specs=qspec,
        out_shape=jax.ShapeDtypeStruct((nb, lp, A_HEADS * A_V_DIM), BF16),
        scratch_shapes=[pltpu.VMEM((2 * TILE, LANES), BF16)] + _flash_scratch(),
        compiler_params=_params(3),
        name="attn_a",
    )(aq, ak, av, tab, dl, gs.reshape(1, A_V_DIM))


def _attn_c_kernel(q_ref, k_ref, v_ref, kmask_ref, o_ref, *scratch, n_chunks):
    blocks = 2 * TILE // ROW_BLOCK

    def q_row(t):
        return _aligned(((t % blocks) // 2) * ROW_BLOCK, ROW_BLOCK)

    def k_row(t):
        return _aligned((t // blocks) * TILE, TILE)

    def scores(t, par):
        cols = slice(par * LANES, (par + 1) * LANES)
        return (_dot_nt(q_ref[pl.ds(q_row(t), ROW_BLOCK), cols], k_ref[pl.ds(k_row(t), TILE), cols])
                + kmask_ref[_iselect(t < blocks, 1, 0)])

    def values(t, par):
        return _ones_right(v_ref[pl.ds(k_row(t), TILE), :])

    def stat_row(t, par):
        return _aligned(par * TILE + q_row(t), ROW_BLOCK)

    _flash_pipeline(n_chunks * blocks, scores, values, stat_row, scratch, C_PIPE_UNROLL)

    acc_ref = scratch[-1]
    o = acc_ref[:, 0:LANES] / acc_ref[:, LANES:2 * LANES]
    lane = lax.broadcasted_iota(jnp.int32, (TILE, LANES), 1)
    o_ref[...] = jnp.where(lane < C_V_DIM, o[0:TILE], o[TILE:2 * TILE]).astype(BF16)


def _attn_c(qc, kc, vc, kmask):
    nb, lp, _ = qc.shape
    n_chunks = lp // TILE
    return pl.pallas_call(
        functools.partial(_attn_c_kernel, n_chunks=n_chunks),
        grid=(nb, C_HEADS // 2, n_chunks),
        in_specs=[pl.BlockSpec((None, TILE, 2 * LANES), lambda b, p, t: (b, t, p)),
                  pl.BlockSpec((None, lp, 2 * LANES), lambda b, p, t: (b, 0, p)),
                  pl.BlockSpec((None, lp, LANES), lambda b, p, t: (b, 0, p)),
                  _resident(kmask.shape)],
        out_specs=pl.BlockSpec((None, TILE, LANES), lambda b, p, t: (b, t, p)),
        out_shape=jax.ShapeDtypeStruct((nb, lp, C_HEADS * C_V_DIM), BF16),
        scratch_shapes=_flash_scratch(),
        compiler_params=_params(3),
        name="attn_c",
    )(qc, kc, vc, kmask)


_B_ORDER = (0, 3, 1, 2)


def _attn_b_kernel(sink_ref, q_ref, k_ref, v_ref, vs_ref, tab_ref, o_ref, *, lp):
    t = pl.program_id(1)
    lane = lax.broadcasted_iota(jnp.int32, (BLOCK, LANES), 1)
    kk = lax.broadcasted_iota(jnp.int32, (1, 3 * BLOCK), 1)
    row = lax.broadcasted_iota(jnp.int32, (B_HEADS * BLOCK, 1), 0)
    sink = jnp.full((B_HEADS * BLOCK, 1), sink_ref[_B_ORDER[0]], F32)
    for n in range(1, B_HEADS):
        sink = jnp.where(row >= n * BLOCK, sink_ref[_B_ORDER[n]], sink)
    half = B_HEADS * BLOCK // 2
    for blk in range(TILE // BLOCK):
        gblk = t * (TILE // BLOCK) + blk
        row0 = pl.multiple_of(gblk * BLOCK, BLOCK)
        kw = k_ref[pl.ds(row0, 3 * BLOCK), :]
        vw = v_ref[pl.ds(row0, 3 * BLOCK), :]
        vsw = vs_ref[pl.ds(row0, 3 * BLOCK), :]
        kslot = (gblk - 1) * BLOCK + kk
        row_mask = jnp.where((kslot >= META_START) & (kslot < lp), 0.0, NEG_INF)
        rows = slice(blk * BLOCK, (blk + 1) * BLOCK)
        qs = jnp.concatenate([q_ref[rows, h * LANES:(h + 1) * LANES] for h in _B_ORDER], axis=0)
        s = _dot_nt(qs, kw) + tab_ref[...] + row_mask
        m = jnp.maximum(jnp.max(s, axis=-1, keepdims=True), sink)
        p = jnp.exp(s - m)
        denom = jnp.sum(p, axis=-1, keepdims=True) + jnp.exp(sink - m)
        pb = p.astype(BF16)
        o03 = _dot(pb[0:half], vw) / denom[0:half]
        o12 = _dot(pb[half:], vsw) / denom[half:]
        o_ref[rows, 0:LANES] = jnp.where(lane < B_HEAD_DIM, o03[0:BLOCK], o12[0:BLOCK]).astype(BF16)
        o_ref[rows, LANES:2 * LANES] = jnp.where(lane < B_HEAD_DIM, o12[BLOCK:], o03[BLOCK:]).astype(BF16)


def _attn_b(bq, bk_pad, bv_pad, bvs_pad, tab, sinks):
    nb, lp, _ = bq.shape
    kvspec = pl.BlockSpec((None, lp + 2 * BLOCK, LANES), lambda b, t: (b, 0, 0))
    return pl.pallas_call(
        functools.partial(_attn_b_kernel, lp=lp),
        grid=(nb, lp // TILE),
        in_specs=[pl.BlockSpec(memory_space=pltpu.SMEM),
                  pl.BlockSpec((None, TILE, B_HEADS * LANES), lambda b, t: (b, t, 0)),
                  kvspec, kvspec, kvspec, _resident(tab.shape)],
        out_specs=pl.BlockSpec((None, TILE, B_HEADS * B_HEAD_DIM), lambda b, t: (b, t, 0)),
        out_shape=jax.ShapeDtypeStruct((nb, lp, B_HEADS * B_HEAD_DIM), BF16),
        compiler_params=_params(2),
        name="attn_b",
    )(sinks, bq, bk_pad, bv_pad, bvs_pad, tab)


def _t5_bucket(rel):
    half = N_BUCKETS // 2
    max_exact = half // 2
    ret = jnp.where(rel > 0, half, 0)
    n = jnp.abs(rel)
    nf = jnp.maximum(n, 1).astype(F32)
    large = max_exact + (jnp.log(nf / max_exact) / math.log(MAX_DISTANCE / max_exact)
                         * (half - max_exact)).astype(jnp.int32)
    large = jnp.minimum(large, half - 1)
    return ret + jnp.where(n < max_exact, n, large)


def _rot_cols(w):
    half = w.shape[-1] // 2
    return jnp.concatenate([-w[..., half:], w[..., :half]], axis=-1)


def _bias_of_rel(table, rel):
    bucket = _t5_bucket(rel)[None]
    out = jnp.zeros((table.shape[1],) + rel.shape, F32)
    for b in range(N_BUCKETS):
        out = jnp.where(bucket == b, table[b].reshape((-1,) + (1,) * rel.ndim), out)
    return out


def _bias_tables(rel_bias, lp):
    rb = rel_bias.astype(F32)
    rela = (jnp.arange(3 * TILE) - TILE)[None, :] - jnp.arange(TILE)[:, None]
    wide = _bias_of_rel(rb[:, :A_HEADS], rela)
    left, diag, right = (wide[:, :, d * TILE:(d + 1) * TILE] for d in range(3))
    far_a = rb[:, :A_HEADS][_t5_bucket(jnp.array([-2 * TILE, 2 * TILE]))].T
    lead = jnp.where(jnp.arange(TILE) >= META_START, 0.0, NEG_INF).astype(F32).reshape(1, 1, TILE)
    far_left = jnp.broadcast_to(far_a[:, 0][:, None, None], left.shape)
    far_right = jnp.broadcast_to(far_a[:, 1][:, None, None], left.shape)
    tab_a = jnp.stack([left, diag, right, far_left, far_right, left + lead, diag + lead, far_left + lead], axis=1)
    tab_a = tab_a.astype(BF16)
    kmask = jnp.concatenate([jnp.zeros_like(lead), lead], axis=0)
    relb = (jnp.arange(3 * BLOCK) - BLOCK)[None, :] - jnp.arange(BLOCK)[:, None]
    tab_b = jnp.where((jnp.abs(relb) <= WINDOW)[None], _bias_of_rel(rb[:, A_HEADS:], relb), NEG_INF)
    tab_b = jnp.concatenate([tab_b[h] for h in _B_ORDER], axis=0)
    slot = jnp.arange(lp)
    inv = ROPE_THETA ** (-jnp.arange(0, C_ROPE_DIM, 2, dtype=F32) / C_ROPE_DIM)
    ang = (slot - META_START).astype(F32)[:, None] * inv[None, :]
    ang = jnp.concatenate([ang, ang], axis=-1)
    ones = jnp.ones((lp, C_NOPE_DIM), F32)
    zeros_n = jnp.zeros((lp, C_NOPE_DIM), F32)
    zeros_p = jnp.zeros((lp, LANES - C_NOPE_DIM - C_ROPE_DIM), F32)
    cos_t = jnp.concatenate([ones, jnp.cos(ang), zeros_p], axis=-1)
    sin_t = jnp.concatenate([zeros_n, jnp.sin(ang), zeros_p], axis=-1)
    return tab_a, tab_b, kmask, cos_t, sin_t


def _layer_weights(w_in, w_uq, w_ukv, w_out):
    d = w_in.shape[0]
    sizes = (512, 512, 512, 256, 128, 128, 256, 128, 32)
    offs = [0]
    for s in sizes:
        offs.append(offs[-1] + s)
    aq, ak, av, bq, bk, bv, cq, ckv, kr = (w_in[:, offs[i]:offs[i + 1]] for i in range(9))
    zeros64 = jnp.zeros((d, B_HEAD_DIM), F32)
    bq_pad = []
    for h in range(B_HEADS):
        qh = bq[:, h * B_HEAD_DIM:(h + 1) * B_HEAD_DIM] * (B_HEAD_DIM ** -0.5)
        bq_pad += [qh, zeros64] if h // (B_HEADS // B_KV_HEADS) == 0 else [zeros64, qh]
    bvs = jnp.concatenate([bv[:, B_HEAD_DIM:], bv[:, :B_HEAD_DIM]], axis=1)
    pad_lo = jnp.zeros((d, C_NOPE_DIM), F32)
    pad_hi = jnp.zeros((d, LANES - C_NOPE_DIM - C_ROPE_DIM), F32)
    w_big = jnp.concatenate(
        [aq * (A_QK_DIM ** -0.5), ak, av] + bq_pad + [bk, bv, bvs, cq, ckv,
                                                      pad_lo, kr, pad_hi, pad_lo, _rot_cols(kr), pad_hi],
        axis=1).astype(BF16)
    uq = w_uq.reshape(C_Q_RANK, C_HEADS, C_NOPE_DIM + C_ROPE_DIM)
    uq_nope, uq_rope = uq[..., :C_NOPE_DIM], uq[..., C_NOPE_DIM:]
    zq = jnp.zeros((C_Q_RANK, C_HEADS, LANES - C_NOPE_DIM - C_ROPE_DIM), F32)
    wuqm = jnp.concatenate([uq_nope, uq_rope, zq], axis=-1).reshape(C_Q_RANK, C_HEADS * LANES).astype(BF16)
    wuqr = jnp.concatenate([jnp.zeros_like(uq_nope), _rot_cols(uq_rope), zq], axis=-1
                           ).reshape(C_Q_RANK, C_HEADS * LANES).astype(BF16)
    ukv = w_ukv.reshape(C_KV_RANK, C_HEADS, C_NOPE_DIM + C_V_DIM)
    wukvk = jnp.concatenate([ukv[..., :C_NOPE_DIM], jnp.zeros((C_KV_RANK, C_HEADS, LANES - C_NOPE_DIM), F32)],
                            axis=-1).reshape(C_KV_RANK, C_HEADS * LANES).astype(BF16)
    wukvv = ukv[..., C_NOPE_DIM:].reshape(C_KV_RANK, C_HEADS * C_V_DIM).astype(BF16)
    n_a = A_HEADS * A_V_DIM
    n_b = B_HEADS * B_HEAD_DIM
    wo = w_out.astype(BF16)
    wuq = jnp.concatenate([wuqm, wuqr], axis=1)
    wukv = jnp.concatenate([wukvk, wukvv], axis=1)
    return w_big, wuq, wukv, wo[:n_a], wo[n_a:n_a + n_b], wo[n_a + n_b:]


def _trunk(x, meta, rel_bias, g_ffn1, w_ffn1_gu, w_ffn1_down, g_mix, w_in, diff_lambda, g_subln, sinks, g_cq,
           g_ckv, w_uq, w_ukv, w_out, g_ffn2, w_ffn2_gu, w_ffn2_down, g_final, groups=None):
    nb, seq, d = x.shape
    lp = BLOCK + seq
    assert lp % TILE == 0 and d == D_MODEL
    groups = ((0, nb),) if groups is None else groups
    lead = jnp.concatenate([jnp.zeros((META_START, d), x.dtype), meta.astype(x.dtype)], axis=0)
    h = jnp.concatenate([jnp.broadcast_to(lead[None], (nb, BLOCK, d)), x], axis=1)
    tab_a, tab_b, kmask, cos_t, sin_t = _bias_tables(rel_bias, lp)
    pad = ((0, 0), (BLOCK, BLOCK), (0, 0))
    for l in range(DEPTH):
        w_big, wuq, wukv, wo_a, wo_b, wo_c = _layer_weights(w_in[l], w_uq[l], w_ukv[l], w_out[l])
        h = _ffn(h, g_ffn1[l], w_ffn1_gu[l].astype(BF16), w_ffn1_down[l].astype(BF16))
        aq, ak, av, bq, bk, bv, bvs, qc, kc, vc = _inproj(h, g_mix[l], w_big, g_cq[l], g_ckv[l], wuq, wukv,
                                                           cos_t, sin_t)
        lam_init = 0.8 - 0.6 * math.exp(-0.3 * l)
        oa = _attn_a(aq, ak, av, tab_a, diff_lambda[l].astype(F32), g_subln[l], lam_init)
        ob = _attn_b(bq, jnp.pad(bk, pad), jnp.pad(bv, pad), jnp.pad(bvs, pad), tab_b, sinks[l].astype(F32))
        oc = _attn_c(qc, kc, vc, kmask)
        ffn2 = functools.partial(_ffn, h, g_ffn2[l], w_ffn2_gu[l].astype(BF16), w_ffn2_down[l].astype(BF16),
                                 mix=(oa, ob, oc, wo_a, wo_b, wo_c))
        if l < DEPTH - 1:
            h = ffn2()
    return tuple(ffn2(g_final=g_final, out_batches=grp) for grp in groups)


def kernel(x_prompt, x_sample, meta, rel_bias, g_ffn1, w_ffn1_gu, w_ffn1_down, g_mix, w_in, diff_lambda, g_subln,
           sinks, g_cq, g_ckv, w_uq, w_ukv, w_out, g_ffn2, w_ffn2_gu, w_ffn2_down, g_final):
    n_prompt = x_prompt.shape[0]
    x = jnp.concatenate([x_prompt, x_sample], axis=0)
    return _trunk(x, meta, rel_bias, g_ffn1, w_ffn1_gu, w_ffn1_down, g_mix, w_in, diff_lambda, g_subln, sinks, g_cq,
                  g_ckv, w_uq, w_ukv, w_out, g_ffn2, w_ffn2_gu, w_ffn2_down, g_final,
                  groups=((0, n_prompt), (n_prompt, x_sample.shape[0])))
```

```python
import functools
import math

import jax
import jax.numpy as jnp
from jax import lax
from jax.experimental import pallas as pl
from jax.experimental.pallas import tpu as pltpu

D_MODEL = 1024
DEPTH = 2
N_META = 16
BLOCK = 128
META_START = BLOCK - N_META
A_HEADS = 4
A_QK_DIM = 64
A_V_DIM = 2 * A_QK_DIM
B_HEADS = 4
B_KV_HEADS = 2
B_HEAD_DIM = 64
WINDOW = 128
C_HEADS = 4
C_Q_RANK = 256
C_KV_RANK = 128
C_NOPE_DIM = 64
C_ROPE_DIM = 32
C_V_DIM = 64
ROPE_THETA = 10000.0
N_BUCKETS = 32
MAX_DISTANCE = 128
D_FF = 2816
EPS = 1e-6

LANES = 128
TILE = 5 * BLOCK
OUT_TILE = 4 * BLOCK
MXU_COLS = 256
FF_SPLITS = (0, 6 * MXU_COLS, D_FF)
ROW_BLOCK = TILE // 4
A_PIPE_UNROLL = 52
C_PIPE_UNROLL = 52
VMEM_LIMIT = 56 * 1024 * 1024

F32 = jnp.float32
BF16 = jnp.bfloat16
NEG_INF = float("-inf")

_W_AQ, _W_AK, _W_AV, _W_BQ = 0, 512, 1024, 1536
_W_BK, _W_BV, _W_BVS = 2048, 2176, 2304
_W_CQ, _W_CKV, _W_KR, _W_KRR = 2432, 2688, 2816, 2944
_W_TOTAL = 3072


def _params(n_axes):
    return pltpu.CompilerParams(dimension_semantics=("arbitrary",) * n_axes, vmem_limit_bytes=VMEM_LIMIT)


def _resident(shape):
    return pl.BlockSpec(shape, lambda *_: (0,) * len(shape), pipeline_mode=pl.Buffered(1))


def _rms(x, g):
    return x * lax.rsqrt(jnp.mean(x * x, axis=-1, keepdims=True) + EPS) * g


def _dot(a, b):
    return jnp.dot(a, b, preferred_element_type=F32)


def _dot_nt(a, b):
    return lax.dot_general(a, b, (((1,), (1,)), ((), ())), preferred_element_type=F32)


def _ffn_kernel(x_ref, *refs, mixed, final_norm, group_starts=None):
    refs = list(refs)
    x = x_ref[...]
    if group_starts is not None:
        b = pl.program_id(0)
        for first in group_starts[1:]:
            x = jnp.where(b >= first, refs.pop(0)[...], x)
        lead = refs.pop(0)[...]
        x = jnp.where(pl.program_id(1) == 0, jnp.concatenate([lead, x[0:TILE - BLOCK]], axis=0), x)
    if mixed:
        oa_ref, ob_ref, oc_ref, wa_ref, wb_ref, wc_ref = refs[:6]
        del refs[:6]
        x = x + _dot(oa_ref[...], wa_ref[...]) + _dot(ob_ref[...], wb_ref[...]) + _dot(oc_ref[...], wc_ref[...])
    g_ref, wgu_ref, wd_ref = refs[:3]
    gf_ref = refs[3] if final_norm else None
    o_ref = refs[-1]
    xn = _rms(x, g_ref[...]).astype(BF16)
    acc = None
    for lo, hi in zip(FF_SPLITS[:-1], FF_SPLITS[1:]):
        g = _dot(xn, wgu_ref[:, lo:hi])
        u = _dot(xn, wgu_ref[:, D_FF + lo:D_FF + hi])
        a = (g * jax.nn.sigmoid(g) * u).astype(BF16)
        d = _dot(a, wd_ref[lo:hi, :])
        acc = d if acc is None else acc + d
    y = x + 0.5 * acc
    if final_norm:
        y = _rms(y, gf_ref[...])
    o_ref[...] = y


def _ffn(h, g, wgu, wd, mix=None, g_final=None, out_batches=None):
    nb, lp, d = h.shape
    if out_batches is None:
        first, count, n_rows, tile_rows = 0, nb, lp, TILE

        def rows(n):
            return pl.BlockSpec((None, TILE, n), lambda b, t: (b, t, 0))

        out_spec = rows(d)
    else:
        first, count = out_batches
        n_rows, tile_rows = lp - BLOCK, OUT_TILE

        def rows(n):
            return pl.BlockSpec((pl.Element(OUT_TILE), pl.Element(n)),
                                lambda b, t: (pl.multiple_of((b + first) * lp + BLOCK + t * OUT_TILE, BLOCK), 0))

        out_spec = pl.BlockSpec((None, OUT_TILE, d), lambda b, t: (b, t, 0))
        h = h.reshape(nb * lp, d)
        if mix is not None:
            mix = tuple(o.reshape(nb * lp, o.shape[-1]) for o in mix[:3]) + tuple(mix[3:])
    assert n_rows % tile_rows == 0

    tile = rows(d)
    in_specs = [tile]
    args = [h]
    if mix is not None:
        in_specs += [rows(o.shape[-1]) for o in mix[:3]] + [_resident(w.shape) for w in mix[3:]]
        args += list(mix)
    in_specs += [_resident((1, d)), _resident(wgu.shape), _resident(wd.shape)]
    args += [g.reshape(1, d), wgu, wd]
    if g_final is not None:
        in_specs.append(_resident((1, d)))
        args.append(g_final.reshape(1, d))
    return pl.pallas_call(
        functools.partial(_ffn_kernel, mixed=mix is not None, final_norm=g_final is not None),
        grid=(count, n_rows // tile_rows),
        in_specs=in_specs,
        out_specs=out_spec,
        out_shape=jax.ShapeDtypeStruct((count, n_rows, d), F32),
        compiler_params=_params(2),
        name="ffn",
    )(*args)


def _ffn_first(xs, lead, g, wgu, wd):
    seq, d = xs[0].shape[1:]
    lp = BLOCK + seq
    counts = [x.shape[0] for x in xs]
    starts = [sum(counts[:i]) for i in range(len(xs))]
    n_t = lp // TILE

    def window(first, count):
        last = (count - 1) * seq + (n_t - 1) * TILE - BLOCK

        def index(b, t):
            own = (b - first) * seq + jnp.maximum(t * TILE - BLOCK, 0)
            off = jnp.where(b < first, 0, jnp.where(b >= first + count, last, own))
            return (pl.multiple_of(off, BLOCK), 0)

        return pl.BlockSpec((pl.Element(TILE), pl.Element(d)), index)

    return pl.pallas_call(
        functools.partial(_ffn_kernel, mixed=False, final_norm=False, group_starts=tuple(starts)),
        grid=(sum(counts), n_t),
        in_specs=[window(f, c) for f, c in zip(starts, counts)]
        + [_resident((BLOCK, d)), _resident((1, d)), _resident(wgu.shape), _resident(wd.shape)],
        out_specs=pl.BlockSpec((None, TILE, d), lambda b, t: (b, t, 0)),
        out_shape=jax.ShapeDtypeStruct((sum(counts), lp, d), F32),
        compiler_params=_params(2),
        name="ffn",
    )(*[x.reshape(-1, d) for x in xs], lead, g.reshape(1, d), wgu, wd)


def _inproj_kernel(h_ref, g_ref, w_ref, gcq_ref, gckv_ref, wuq_ref, wukv_ref, cos_ref, sin_ref,
                   aq_ref, ak_ref, av_ref, bq_ref, bk_ref, bv_ref, bvs_ref, qc_ref, kc_ref, vc_ref, *, c_scale):
    xn = _rms(h_ref[...], g_ref[...]).astype(BF16)

    wide = 2 * _W_AK
    y = _dot(xn, w_ref[:, 0:wide])
    aq_ref[...] = y[:, _W_AQ:_W_AK].astype(BF16)
    ak_ref[...] = y[:, _W_AK:wide].astype(BF16)
    y = _dot(xn, w_ref[:, wide:2 * wide])
    av_ref[...] = y[:, 0:_W_BQ - _W_AV].astype(BF16)
    bq_ref[...] = y[:, _W_BQ - _W_AV:wide].astype(BF16)
    y = _dot(xn, w_ref[:, _W_BK:_W_TOTAL])

    def piece(lo, n):
        return y[:, lo - _W_BK:lo - _W_BK + n]

    bk_ref[...] = piece(_W_BK, LANES).astype(BF16)
    bv_ref[...] = piece(_W_BV, LANES).astype(BF16)
    bvs_ref[...] = piece(_W_BVS, LANES).astype(BF16)

    cos = cos_ref[...]
    sin = sin_ref[...]
    cos4 = jnp.concatenate([cos] * C_HEADS, axis=1)
    sin4 = jnp.concatenate([sin] * C_HEADS, axis=1)
    cqn = _rms(piece(_W_CQ, C_Q_RANK), gcq_ref[...]).astype(BF16)
    q = _dot(cqn, wuq_ref[...])
    n_q = C_HEADS * LANES
    qc_ref[...] = ((q[:, 0:n_q] * cos4 + q[:, n_q:2 * n_q] * sin4) * c_scale).astype(BF16)

    ckvn = _rms(piece(_W_CKV, C_KV_RANK), gckv_ref[...]).astype(BF16)
    k_rope = piece(_W_KR, LANES) * cos + piece(_W_KRR, LANES) * sin
    kv = _dot(ckvn, wukv_ref[...])
    kc_ref[...] = (kv[:, 0:n_q] + jnp.concatenate([k_rope] * C_HEADS, axis=1)).astype(BF16)
    vc_ref[...] = kv[:, n_q:n_q + C_HEADS * C_V_DIM].astype(BF16)


def _inproj(h, g, w_big, gcq, gckv, wuq, wukv, cos_t, sin_t):
    nb, lp, d = h.shape

    def tile(n):
        return pl.BlockSpec((None, TILE, n), lambda b, t: (b, t, 0))

    pos = pl.BlockSpec((TILE, LANES), lambda b, t: (t, 0))
    widths = (512, 512, 512, 512, 128, 128, 128, 512, 512, 256)
    return pl.pallas_call(
        functools.partial(_inproj_kernel, c_scale=(C_NOPE_DIM + C_ROPE_DIM) ** -0.5),
        grid=(nb, lp // TILE),
        in_specs=[tile(d), _resident((1, d)), _resident(w_big.shape), _resident((1, C_Q_RANK)),
                  _resident((1, C_KV_RANK)), _resident(wuq.shape), _resident(wukv.shape), pos, pos],
        out_specs=[tile(n) for n in widths],
        out_shape=[jax.ShapeDtypeStruct((nb, lp, n), BF16) for n in widths],
        compiler_params=_params(2),
        name="inproj",
    )(h, g.reshape(1, d), w_big, gcq.reshape(1, -1), gckv.reshape(1, -1), wuq, wukv, cos_t, sin_t)


def _static(x):
    return isinstance(x, (int, bool))


def _aligned(x, m):
    return x if _static(x) else pl.multiple_of(x, m)


def _imin(a, b):
    return min(a, b) if _static(a) else jnp.minimum(a, b)


def _imax(a, b):
    return max(a, b) if _static(a) else jnp.maximum(a, b)


def _iselect(c, a, b):
    return (a if c else b) if _static(c) else jnp.where(c, a, b)


def _flash_scratch():
    return ([pltpu.VMEM((ROW_BLOCK, TILE), BF16)] * 2 + [pltpu.VMEM((ROW_BLOCK, LANES), F32)] * 2
            + [pltpu.VMEM((ROW_BLOCK, TILE), BF16)] * 2 + [pltpu.VMEM((ROW_BLOCK, LANES), F32)] * 2
            + [pltpu.VMEM((2 * TILE, LANES), F32), pltpu.VMEM((2 * TILE, 2 * LANES), F32)])


def _flash_pipeline(n_tiles, scores, values, stat_row, scratch, max_unroll):
    s0, s1, x0, x1, p0, p1, a0, a1, m_ref, acc_ref = scratch
    s_bufs, x_bufs, p_bufs, a_bufs = (s0, s1), (x0, x1), (p0, p1), (a0, a1)
    unroll = max(k for k in range(2, max_unroll + 1, 2) if n_tiles % k == 0)
    m_ref[...] = jnp.full(m_ref.shape, NEG_INF, F32)
    acc_ref[...] = jnp.zeros(acc_ref.shape, F32)
    p1[...] = jnp.zeros(p1.shape, BF16)
    a1[...] = jnp.ones(a1.shape, F32)

    def issue(t, par):
        s = scores(t, par).astype(BF16)
        s_bufs[par][...] = s
        x_bufs[par][...] = jnp.broadcast_to(jnp.max(s, axis=-1, keepdims=True).astype(F32), (ROW_BLOCK, LANES))

    def accumulate(t, par):
        rows = pl.ds(stat_row(t, par), ROW_BLOCK)
        alpha = a_bufs[par][...]
        acc_ref[rows, :] = (jnp.concatenate([alpha, alpha], axis=1) * acc_ref[rows, :]
                            + _dot(p_bufs[par][...], values(t, par)))

    def step(t, par):
        issue(_imin(t + 1, n_tiles - 1), 1 - par)
        rows = pl.ds(stat_row(t, par), ROW_BLOCK)
        m_old = m_ref[rows, :]
        m_new = jnp.maximum(m_old, x_bufs[par][...])
        p_bufs[par][...] = jnp.exp(s_bufs[par][...]
                                   - jnp.concatenate([m_new.astype(BF16)] * (TILE // LANES), axis=1))
        a_bufs[par][...] = jnp.exp(m_old - m_new)
        m_ref[rows, :] = m_new
        accumulate(_imax(t - 1, 0), 1 - par)

    issue(0, 0)

    def body(u, carry):
        for k in range(unroll):
            step(unroll * u + k, k % 2)
        return carry

    if unroll == n_tiles:
        for k in range(n_tiles):
            step(k, k % 2)
    else:
        lax.fori_loop(0, n_tiles // unroll, body, 0)
    accumulate(n_tiles - 1, 1)


(_TAB_LEFT, _TAB_DIAG, _TAB_RIGHT, _TAB_FAR_LEFT, _TAB_FAR_RIGHT,
 _TAB_LEFT_LEAD, _TAB_DIAG_LEAD, _TAB_FAR_LEAD) = range(8)


def _ones_right(v):
    return jnp.concatenate([v, jnp.ones(v.shape, v.dtype)], axis=1)


def _attn_a_kernel(q_ref, k_ref, v_ref, tab_ref, dl_ref, gs_ref, o_ref, qq_ref, *scratch, lam_init, n_chunks):
    i = pl.program_id(2)
    q = q_ref[...]
    lane = lax.broadcasted_iota(jnp.int32, q.shape, 1)
    zero = jnp.zeros_like(q)
    qq_ref[0:TILE, :] = jnp.where(lane < A_QK_DIM, q, zero)
    qq_ref[TILE:2 * TILE, :] = jnp.where(lane >= A_QK_DIM, q, zero)
    blocks = 2 * TILE // ROW_BLOCK

    def q_row(t):
        return _aligned((t % blocks) * ROW_BLOCK, ROW_BLOCK)

    def k_row(t):
        return _aligned((t // blocks) * TILE, TILE)

    def scores(t, par):
        j = t // blocks
        lead = jnp.where(i == 0, _TAB_DIAG_LEAD, jnp.where(i == 1, _TAB_LEFT_LEAD, _TAB_FAR_LEAD))
        rest = jnp.where(j < i - 1, _TAB_FAR_LEFT, jnp.where(j > i + 1, _TAB_FAR_RIGHT, j - i + 1))
        table = jnp.where(j == 0, lead, rest)
        row = _aligned((t % (TILE // ROW_BLOCK)) * ROW_BLOCK, ROW_BLOCK)
        return (_dot_nt(qq_ref[pl.ds(q_row(t), ROW_BLOCK), :], k_ref[pl.ds(k_row(t), TILE), :])
                + tab_ref[table, pl.ds(row, ROW_BLOCK), :])

    def values(t, par):
        return _ones_right(v_ref[pl.ds(k_row(t), TILE), :])

    _flash_pipeline(n_chunks * blocks, scores, values, lambda t, par: q_row(t), scratch, A_PIPE_UNROLL)

    acc_ref = scratch[-1]
    o = acc_ref[:, 0:A_V_DIM] / acc_ref[:, A_V_DIM:2 * A_V_DIM]
    dl = dl_ref[...]
    lam = (jnp.exp(jnp.sum(dl[0:1] * dl[1:2], axis=-1, keepdims=True))
           - jnp.exp(jnp.sum(dl[2:3] * dl[3:4], axis=-1, keepdims=True)) + lam_init)
    w = o[0:TILE] - lam * o[TILE:2 * TILE]
    o_ref[...] = (_rms(w, gs_ref[...]) * (1.0 - lam_init)).astype(BF16)


def _attn_a(aq, ak, av, tab, dl, gs, lam_init):
    nb, lp, _ = aq.shape
    n_chunks = lp // TILE
    qspec = pl.BlockSpec((None, TILE, LANES), lambda b, h, t: (b, t, h))
    kvspec = pl.BlockSpec((None, lp, LANES), lambda b, h, t: (b, 0, h))
    return pl.pallas_call(
        functools.partial(_attn_a_kernel, lam_init=lam_init, n_chunks=n_chunks),
        grid=(nb, A_HEADS, n_chunks),
        in_specs=[qspec, kvspec, kvspec,
                  pl.BlockSpec((None,) + tab.shape[1:], lambda b, h, t: (h, 0, 0, 0)),
                  _resident((4, A_QK_DIM)), _resident((1, A_V_DIM))],
        out_specs=qspec,
        out_shape=jax.ShapeDtypeStruct((nb, lp, A_HEADS * A_V_DIM), BF16),
        scratch_shapes=[pltpu.VMEM((2 * TILE, LANES), BF16)] + _flash_scratch(),
        compiler_params=_params(3),
        name="attn_a",
    )(aq, ak, av, tab, dl, gs.reshape(1, A_V_DIM))


def _attn_c_kernel(q_ref, k_ref, v_ref, kmask_ref, o_ref, *scratch, n_chunks):
    blocks = 2 * TILE // ROW_BLOCK

    def q_row(t):
        return _aligned(((t % blocks) // 2) * ROW_BLOCK, ROW_BLOCK)

    def k_row(t):
        return _aligned((t // blocks) * TILE, TILE)

    def scores(t, par):
        cols = slice(par * LANES, (par + 1) * LANES)
        return (_dot_nt(q_ref[pl.ds(q_row(t), ROW_BLOCK), cols], k_ref[pl.ds(k_row(t), TILE), cols])
                + kmask_ref[_iselect(t < blocks, 1, 0)])

    def values(t, par):
        return _ones_right(v_ref[pl.ds(k_row(t), TILE), :])

    def stat_row(t, par):
        return _aligned(par * TILE + q_row(t), ROW_BLOCK)

    _flash_pipeline(n_chunks * blocks, scores, values, stat_row, scratch, C_PIPE_UNROLL)

    acc_ref = scratch[-1]
    o = acc_ref[:, 0:LANES] / acc_ref[:, LANES:2 * LANES]
    lane = lax.broadcasted_iota(jnp.int32, (TILE, LANES), 1)
    o_ref[...] = jnp.where(lane < C_V_DIM, o[0:TILE], o[TILE:2 * TILE]).astype(BF16)


def _attn_c(qc, kc, vc, kmask):
    nb, lp, _ = qc.shape
    n_chunks = lp // TILE
    return pl.pallas_call(
        functools.partial(_attn_c_kernel, n_chunks=n_chunks),
        grid=(nb, C_HEADS // 2, n_chunks),
        in_specs=[pl.BlockSpec((None, TILE, 2 * LANES), lambda b, p, t: (b, t, p)),
                  pl.BlockSpec((None, lp, 2 * LANES), lambda b, p, t: (b, 0, p)),
                  pl.BlockSpec((None, lp, LANES), lambda b, p, t: (b, 0, p)),
                  _resident(kmask.shape)],
        out_specs=pl.BlockSpec((None, TILE, LANES), lambda b, p, t: (b, t, p)),
        out_shape=jax.ShapeDtypeStruct((nb, lp, C_HEADS * C_V_DIM), BF16),
        scratch_shapes=_flash_scratch(),
        compiler_params=_params(3),
        name="attn_c",
    )(qc, kc, vc, kmask)


_B_ORDER = (0, 3, 1, 2)


def _attn_b_kernel(sink_ref, q_ref, k_ref, v_ref, vs_ref, tab_ref, o_ref, *, lp):
    t = pl.program_id(1)
    lane = lax.broadcasted_iota(jnp.int32, (BLOCK, LANES), 1)
    kk = lax.broadcasted_iota(jnp.int32, (1, 3 * BLOCK), 1)
    row = lax.broadcasted_iota(jnp.int32, (B_HEADS * BLOCK, 1), 0)
    sink = jnp.full((B_HEADS * BLOCK, 1), sink_ref[_B_ORDER[0]], F32)
    for n in range(1, B_HEADS):
        sink = jnp.where(row >= n * BLOCK, sink_ref[_B_ORDER[n]], sink)
    half = B_HEADS * BLOCK // 2
    for blk in range(TILE // BLOCK):
        gblk = t * (TILE // BLOCK) + blk
        row0 = pl.multiple_of(gblk * BLOCK, BLOCK)
        kw = k_ref[pl.ds(row0, 3 * BLOCK), :]
        vw = v_ref[pl.ds(row0, 3 * BLOCK), :]
        vsw = vs_ref[pl.ds(row0, 3 * BLOCK), :]
        kslot = (gblk - 1) * BLOCK + kk
        row_mask = jnp.where((kslot >= META_START) & (kslot < lp), 0.0, NEG_INF)
        rows = slice(blk * BLOCK, (blk + 1) * BLOCK)
        qs = jnp.concatenate([q_ref[rows, h * LANES:(h + 1) * LANES] for h in _B_ORDER], axis=0)
        s = _dot_nt(qs, kw) + tab_ref[...] + row_mask
        m = jnp.maximum(jnp.max(s, axis=-1, keepdims=True), sink)
        p = jnp.exp(s - m)
        denom = jnp.sum(p, axis=-1, keepdims=True) + jnp.exp(sink - m)
        pb = p.astype(BF16)
        o03 = _dot(pb[0:half], vw) / denom[0:half]
        o12 = _dot(pb[half:], vsw) / denom[half:]
        o_ref[rows, 0:LANES] = jnp.where(lane < B_HEAD_DIM, o03[0:BLOCK], o12[0:BLOCK]).astype(BF16)
        o_ref[rows, LANES:2 * LANES] = jnp.where(lane < B_HEAD_DIM, o12[BLOCK:], o03[BLOCK:]).astype(BF16)


def _attn_b(bq, bk_pad, bv_pad, bvs_pad, tab, sinks):
    nb, lp, _ = bq.shape
    kvspec = pl.BlockSpec((None, lp + 2 * BLOCK, LANES), lambda b, t: (b, 0, 0))
    return pl.pallas_call(
        functools.partial(_attn_b_kernel, lp=lp),
        grid=(nb, lp // TILE),
        in_specs=[pl.BlockSpec(memory_space=pltpu.SMEM),
                  pl.BlockSpec((None, TILE, B_HEADS * LANES), lambda b, t: (b, t, 0)),
                  kvspec, kvspec, kvspec, _resident(tab.shape)],
        out_specs=pl.BlockSpec((None, TILE, B_HEADS * B_HEAD_DIM), lambda b, t: (b, t, 0)),
        out_shape=jax.ShapeDtypeStruct((nb, lp, B_HEADS * B_HEAD_DIM), BF16),
        compiler_params=_params(2),
        name="attn_b",
    )(sinks, bq, bk_pad, bv_pad, bvs_pad, tab)


def _t5_bucket(rel):
    half = N_BUCKETS // 2
    max_exact = half // 2
    ret = jnp.where(rel > 0, half, 0)
    n = jnp.abs(rel)
    nf = jnp.maximum(n, 1).astype(F32)
    large = max_exact + (jnp.log(nf / max_exact) / math.log(MAX_DISTANCE / max_exact)
                         * (half - max_exact)).astype(jnp.int32)
    large = jnp.minimum(large, half - 1)
    return ret + jnp.where(n < max_exact, n, large)


def _rot_cols(w):
    half = w.shape[-1] // 2
    return jnp.concatenate([-w[..., half:], w[..., :half]], axis=-1)


def _bias_of_rel(table, rel):
    bucket = _t5_bucket(rel)[None]
    out = jnp.zeros((table.shape[1],) + rel.shape, F32)
    for b in range(N_BUCKETS):
        out = jnp.where(bucket == b, table[b].reshape((-1,) + (1,) * rel.ndim), out)
    return out


def _bias_tables(rel_bias, lp):
    rb = rel_bias.astype(F32)
    rela = (jnp.arange(3 * TILE) - TILE)[None, :] - jnp.arange(TILE)[:, None]
    wide = _bias_of_rel(rb[:, :A_HEADS], rela)
    left, diag, right = (wide[:, :, d * TILE:(d + 1) * TILE] for d in range(3))
    far_a = rb[:, :A_HEADS][_t5_bucket(jnp.array([-2 * TILE, 2 * TILE]))].T
    lead = jnp.where(jnp.arange(TILE) >= META_START, 0.0, NEG_INF).astype(F32).reshape(1, 1, TILE)
    far_left = jnp.broadcast_to(far_a[:, 0][:, None, None], left.shape)
    far_right = jnp.broadcast_to(far_a[:, 1][:, None, None], left.shape)
    tab_a = jnp.stack([left, diag, right, far_left, far_right, left + lead, diag + lead, far_left + lead], axis=1)
    kmask = jnp.concatenate([jnp.zeros_like(lead), lead], axis=0)
    relb = (jnp.arange(3 * BLOCK) - BLOCK)[None, :] - jnp.arange(BLOCK)[:, None]
    tab_b = jnp.where((jnp.abs(relb) <= WINDOW)[None], _bias_of_rel(rb[:, A_HEADS:], relb), NEG_INF)
    tab_b = jnp.concatenate([tab_b[h] for h in _B_ORDER], axis=0)
    slot = jnp.arange(lp)
    inv = ROPE_THETA ** (-jnp.arange(0, C_ROPE_DIM, 2, dtype=F32) / C_ROPE_DIM)
    ang = (slot - META_START).astype(F32)[:, None] * inv[None, :]
    ang = jnp.concatenate([ang, ang], axis=-1)
    ones = jnp.ones((lp, C_NOPE_DIM), F32)
    zeros_n = jnp.zeros((lp, C_NOPE_DIM), F32)
    zeros_p = jnp.zeros((lp, LANES - C_NOPE_DIM - C_ROPE_DIM), F32)
    cos_t = jnp.concatenate([ones, jnp.cos(ang), zeros_p], axis=-1)
    sin_t = jnp.concatenate([zeros_n, jnp.sin(ang), zeros_p], axis=-1)
    return tab_a, tab_b, kmask, cos_t, sin_t


def _layer_weights(w_in, w_uq, w_ukv, w_out):
    d = w_in.shape[0]
    sizes = (512, 512, 512, 256, 128, 128, 256, 128, 32)
    offs = [0]
    for s in sizes:
        offs.append(offs[-1] + s)
    aq, ak, av, bq, bk, bv, cq, ckv, kr = (w_in[:, offs[i]:offs[i + 1]] for i in range(9))
    zeros64 = jnp.zeros((d, B_HEAD_DIM), F32)
    bq_pad = []
    for h in range(B_HEADS):
        qh = bq[:, h * B_HEAD_DIM:(h + 1) * B_HEAD_DIM] * (B_HEAD_DIM ** -0.5)
        bq_pad += [qh, zeros64] if h // (B_HEADS // B_KV_HEADS) == 0 else [zeros64, qh]
    bvs = jnp.concatenate([bv[:, B_HEAD_DIM:], bv[:, :B_HEAD_DIM]], axis=1)
    pad_lo = jnp.zeros((d, C_NOPE_DIM), F32)
    pad_hi = jnp.zeros((d, LANES - C_NOPE_DIM - C_ROPE_DIM), F32)
    w_big = jnp.concatenate(
        [aq * (A_QK_DIM ** -0.5), ak, av] + bq_pad + [bk, bv, bvs, cq, ckv,
                                                      pad_lo, kr, pad_hi, pad_lo, _rot_cols(kr), pad_hi],
        axis=1).astype(BF16)
    uq = w_uq.reshape(C_Q_RANK, C_HEADS, C_NOPE_DIM + C_ROPE_DIM)
    uq_nope, uq_rope = uq[..., :C_NOPE_DIM], uq[..., C_NOPE_DIM:]
    zq = jnp.zeros((C_Q_RANK, C_HEADS, LANES - C_NOPE_DIM - C_ROPE_DIM), F32)
    wuqm = jnp.concatenate([uq_nope, uq_rope, zq], axis=-1).reshape(C_Q_RANK, C_HEADS * LANES).astype(BF16)
    wuqr = jnp.concatenate([jnp.zeros_like(uq_nope), _rot_cols(uq_rope), zq], axis=-1
                           ).reshape(C_Q_RANK, C_HEADS * LANES).astype(BF16)
    ukv = w_ukv.reshape(C_KV_RANK, C_HEADS, C_NOPE_DIM + C_V_DIM)
    wukvk = jnp.concatenate([ukv[..., :C_NOPE_DIM], jnp.zeros((C_KV_RANK, C_HEADS, LANES - C_NOPE_DIM), F32)],
                            axis=-1).reshape(C_KV_RANK, C_HEADS * LANES).astype(BF16)
    wukvv = ukv[..., C_NOPE_DIM:].reshape(C_KV_RANK, C_HEADS * C_V_DIM).astype(BF16)
    n_a = A_HEADS * A_V_DIM
    n_b = B_HEADS * B_HEAD_DIM
    wo = w_out.astype(BF16)
    wuq = jnp.concatenate([wuqm, wuqr], axis=1)
    wukv = jnp.concatenate([wukvk, wukvv], axis=1)
    return w_big, wuq, wukv, wo[:n_a], wo[n_a:n_a + n_b], wo[n_a + n_b:]


def _trunk(xs, meta, rel_bias, g_ffn1, w_ffn1_gu, w_ffn1_down, g_mix, w_in, diff_lambda, g_subln, sinks, g_cq,
           g_ckv, w_uq, w_ukv, w_out, g_ffn2, w_ffn2_gu, w_ffn2_down, g_final):
    xs = tuple(xs) if isinstance(xs, (tuple, list)) else (xs,)
    seq, d = xs[0].shape[1:]
    lp = BLOCK + seq
    assert lp % TILE == 0 and d == D_MODEL and all(x.shape[1:] == (seq, d) for x in xs)
    counts = [x.shape[0] for x in xs]
    groups = tuple((sum(counts[:i]), c) for i, c in enumerate(counts))
    lead = jnp.concatenate([jnp.zeros((META_START, d), F32), meta.astype(F32)], axis=0)
    tab_a, tab_b, kmask, cos_t, sin_t = _bias_tables(rel_bias, lp)
    pad = ((0, 0), (BLOCK, BLOCK), (0, 0))
    for l in range(DEPTH):
        w_big, wuq, wukv, wo_a, wo_b, wo_c = _layer_weights(w_in[l], w_uq[l], w_ukv[l], w_out[l])
        if l == 0:
            h = _ffn_first(xs, lead, g_ffn1[l], w_ffn1_gu[l].astype(BF16), w_ffn1_down[l].astype(BF16))
        else:
            h = _ffn(h, g_ffn1[l], w_ffn1_gu[l].astype(BF16), w_ffn1_down[l].astype(BF16))
        aq, ak, av, bq, bk, bv, bvs, qc, kc, vc = _inproj(h, g_mix[l], w_big, g_cq[l], g_ckv[l], wuq, wukv,
                                                           cos_t, sin_t)
        lam_init = 0.8 - 0.6 * math.exp(-0.3 * l)
        oa = _attn_a(aq, ak, av, tab_a, diff_lambda[l].astype(F32), g_subln[l], lam_init)
        ob = _attn_b(bq, jnp.pad(bk, pad), jnp.pad(bv, pad), jnp.pad(bvs, pad), tab_b, sinks[l].astype(F32))
        oc = _attn_c(qc, kc, vc, kmask)
        ffn2 = functools.partial(_ffn, h, g_ffn2[l], w_ffn2_gu[l].astype(BF16), w_ffn2_down[l].astype(BF16),
                                 mix=(oa, ob, oc, wo_a, wo_b, wo_c))
        if l < DEPTH - 1:
            h = ffn2()
    return tuple(ffn2(g_final=g_final, out_batches=grp) for grp in groups)


def kernel(x_prompt, x_sample, meta, rel_bias, g_ffn1, w_ffn1_gu, w_ffn1_down, g_mix, w_in, diff_lambda, g_subln,
           sinks, g_cq, g_ckv, w_uq, w_ukv, w_out, g_ffn2, w_ffn2_gu, w_ffn2_down, g_final):
    return _trunk((x_prompt, x_sample), meta, rel_bias, g_ffn1, w_ffn1_gu, w_ffn1_down, g_mix, w_in, diff_lambda,
                  g_subln, sinks, g_cq, g_ckv, w_uq, w_ukv, w_out, g_ffn2, w_ffn2_gu, w_ffn2_down, g_final)
```
